```python
import math
import jax, jax.numpy as jnp
from jax import lax
import numpy as np

D_MODEL = 1024
BATCH = 2
SEQ = 8192
DEPTH = 2

MEM_LEN = 256
MLA_HEADS = 8
MLA_NOPE = 64
MLA_ROPE = 32
MLA_V = 64
MLA_Q_RANK = 384
MLA_KV_RANK = 256
ROPE_BASE = 10000.0
NSA_HEADS = 8
NSA_KV_HEADS = 2
NSA_GROUP = NSA_HEADS // NSA_KV_HEADS
NSA_DH = 64
CMP_STRIDE = 16
CMP_BLOCK = 2 * CMP_STRIDE
CMP_HIDDEN = 128
SEL_BLOCK = 64
N_SEL = 16
WINDOW = 512
FORCED_SCORE = 1e4
XA_HEADS = 4
XA_DH = D_MODEL // XA_HEADS
D_FF = 2816
CONV_W = 3
QB = 128
DEEPNORM_ALPHA = (2.0 * DEPTH) ** 0.25
DEEPNORM_BETA = (8.0 * DEPTH) ** -0.25
LN_EPS = 1e-5
RMS_EPS = 1e-6
NEG = -1e30

IN_SPLITS = (
    MLA_Q_RANK,
    MLA_KV_RANK,
    MLA_ROPE,
    NSA_HEADS * NSA_DH,
    NSA_KV_HEADS * NSA_DH,
    NSA_KV_HEADS * NSA_DH,
    NSA_KV_HEADS * NSA_DH,
    NSA_KV_HEADS * NSA_DH,
    NSA_KV_HEADS * NSA_DH,
    NSA_KV_HEADS * NSA_DH,
    NSA_HEADS * 3,
)
IN_COLS = sum(IN_SPLITS)
MIX_WIDTH = MLA_HEADS * MLA_V + NSA_HEADS * NSA_DH

kernel_name = 'hybrid_mla_nsa_deepnorm_trunk'


def layer_norm(x, g, b):
    xf = x.astype(jnp.float32)
    mu = jnp.mean(xf, -1, keepdims=True)
    var = jnp.mean(jnp.square(xf - mu), -1, keepdims=True)
    return ((xf - mu) * lax.rsqrt(var + LN_EPS) * g + b).astype(x.dtype)


def rms_norm(x, g):
    xf = x.astype(jnp.float32)
    return (xf * lax.rsqrt(jnp.mean(xf * xf, -1, keepdims=True) + RMS_EPS) * g).astype(x.dtype)


def masked_softmax(s, mask):
    s = jnp.where(mask, s, NEG)
    m = jnp.max(s, -1, keepdims=True)
    e = jnp.where(mask, jnp.exp(s - m), 0.0)
    d = jnp.sum(e, -1, keepdims=True)
    return e / jnp.maximum(d, 1e-30)


def alibi_slopes(n):
    return 2.0 ** (-8.0 * jnp.arange(1, n + 1, dtype=jnp.float32) / n)


def rope_tables(S):
    inv = ROPE_BASE ** (-jnp.arange(0, MLA_ROPE, 2, dtype=jnp.float32) / MLA_ROPE)
    ang = jnp.arange(S, dtype=jnp.float32)[:, None] * inv[None, :]
    return jnp.cos(ang), jnp.sin(ang)


def apply_rope(x, cos, sin):
    x1, x2 = jnp.split(x, 2, axis=-1)
    c = cos[None, :, None, :].astype(x.dtype)
    s = sin[None, :, None, :].astype(x.dtype)
    return jnp.concatenate([x1 * c - x2 * s, x1 * s + x2 * c], axis=-1)


def to_blocks(x):
    B, S = x.shape[:2]
    return x.reshape(B, S // QB, QB, *x.shape[2:]).swapaxes(0, 1)


def from_blocks(y):
    nq, B = y.shape[:2]
    return y.swapaxes(0, 1).reshape(B, nq * QB, *y.shape[3:])


def mla_mixer(cq, ckv, kr, cq_g, w_uq, ckv_g, w_ukv, cos, sin):
    B, S, _ = cq.shape
    q = (rms_norm(cq, cq_g) @ w_uq).reshape(B, S, MLA_HEADS, MLA_NOPE + MLA_ROPE)
    kv = (rms_norm(ckv, ckv_g) @ w_ukv).reshape(B, S, MLA_HEADS, MLA_NOPE + MLA_V)
    q_nope = q[..., :MLA_NOPE]
    q_rope = apply_rope(q[..., MLA_NOPE:], cos, sin)
    k_nope, v = kv[..., :MLA_NOPE], kv[..., MLA_NOPE:]
    k_rope = apply_rope(kr[:, :, None, :], cos, sin)[:, :, 0]
    scale = (MLA_NOPE + MLA_ROPE) ** -0.5
    kpos = jnp.arange(S)

    def block(args):
        i, qn, qr = args
        s = (jnp.einsum('bqhd,bkhd->bhqk', qn, k_nope)
             + jnp.einsum('bqhd,bkd->bhqk', qr, k_rope)).astype(jnp.float32) * scale
        qpos = i * QB + jnp.arange(QB)
        p = masked_softmax(s, kpos[None, :] <= qpos[:, None])
        return jnp.einsum('bhqk,bkhd->bqhd', p.astype(v.dtype), v)

    o = lax.map(block, (jnp.arange(S // QB), to_blocks(q_nope), to_blocks(q_rope)))
    return from_blocks(o).reshape(B, S, MLA_HEADS * MLA_V)


def compress_blocks(kv, pe, w1, b1, w2):
    B, S, G, D = kv.shape
    ch = kv.reshape(B, S // CMP_STRIDE, CMP_STRIDE, G, D)
    blk = jnp.concatenate([ch[:, :-1], ch[:, 1:]], axis=2) + pe[None, None, :, None, :]
    flat = blk.transpose(0, 1, 3, 2, 4).reshape(B, -1, G, CMP_BLOCK * D)
    return jax.nn.gelu(flat @ w1 + b1) @ w2


def nsa_mixer(q, k_cmp, v_cmp, k_slc, v_slc, k_win, v_win, gate_logits,
              pe_k, w1_k, b1_k, w2_k, pe_v, w1_v, b1_v, w2_v):
    B, S, _ = q.shape
    G, R, D = NSA_KV_HEADS, NSA_GROUP, NSA_DH
    scale = D ** -0.5
    slopes = alibi_slopes(NSA_HEADS).reshape(G, R)
    kv4 = lambda a: a.reshape(B, S, G, D)

    kc = compress_blocks(kv4(k_cmp), pe_k, w1_k, b1_k, w2_k)
    vc = compress_blocks(kv4(v_cmp), pe_v, w1_v, b1_v, w2_v)
    NC = kc.shape[1]
    c_start = jnp.arange(NC) * CMP_STRIDE
    c_end = c_start + CMP_BLOCK - 1

    NS = S // SEL_BLOCK
    n_sel = min(N_SEL, NS)
    s_start = jnp.arange(NS) * SEL_BLOCK
    overlap = jnp.clip(jnp.minimum(c_start[:, None] + CMP_BLOCK, s_start[None, :] + SEL_BLOCK)
                       - jnp.maximum(c_start[:, None], s_start[None, :]), 0).astype(jnp.float32) / CMP_BLOCK
    ks_blk = kv4(k_slc).reshape(B, NS, SEL_BLOCK, G, D).transpose(0, 3, 1, 2, 4).reshape(B, G, NS, SEL_BLOCK * D)
    vs_blk = kv4(v_slc).reshape(B, NS, SEL_BLOCK, G, D).transpose(0, 3, 1, 2, 4).reshape(B, G, NS, SEL_BLOCK * D)

    kw_pad = jnp.pad(kv4(k_win), ((0, 0), (WINDOW, 0), (0, 0), (0, 0)))
    vw_pad = jnp.pad(kv4(v_win), ((0, 0), (WINDOW, 0), (0, 0), (0, 0)))

    gates = jax.nn.sigmoid(gate_logits.astype(jnp.float32)).reshape(B, S, G, R, 3).astype(q.dtype)
    qg = q.reshape(B, S, G, R, D)
    M = n_sel * SEL_BLOCK
    blk_ids = jnp.arange(NS)

    def block(args):
        i, qi, gi = args
        t = i * QB + jnp.arange(QB)
        s = jnp.einsum('bqgrd,bngd->bgrqn', qi, kc).astype(jnp.float32) * scale
        s = s - slopes[:, :, None, None] * (t[:, None] - c_end[None, :]).astype(jnp.float32)
        p_c = masked_softmax(s, c_end[None, :] <= t[:, None])
        o_c = jnp.einsum('bgrqn,bngd->bqgrd', p_c.astype(vc.dtype), vc)
        imp = jnp.einsum('bgrqn,nj->bgqj', p_c, overlap)
        cur = t // SEL_BLOCK
        valid = blk_ids[None, :] <= cur[:, None]
        forced = (blk_ids[None, :] == 0) | (blk_ids[None, :] == cur[:, None]) | (blk_ids[None, :] == cur[:, None] - 1)
        score = jnp.where(forced, FORCED_SCORE, jnp.where(valid, imp, -1.0))
        top_score, idx = lax.top_k(score, n_sel)
        sel_ok = top_score >= 0
        flat_idx = idx.reshape(B, G, QB * n_sel)[..., None]
        kg = jnp.take_along_axis(ks_blk, flat_idx, axis=2).reshape(B, G, QB, M, D)
        vg = jnp.take_along_axis(vs_blk, flat_idx, axis=2).reshape(B, G, QB, M, D)
        pos5 = idx[..., None] * SEL_BLOCK + jnp.arange(SEL_BLOCK)
        dist5 = t[None, None, :, None, None] - pos5
        mask_s = (sel_ok[..., None] & (dist5 >= 0)).reshape(B, G, 1, QB, M)
        dist_s = dist5.reshape(B, G, 1, QB, M).astype(jnp.float32)
        s = jnp.einsum('bqgrd,bgqmd->bgrqm', qi, kg).astype(jnp.float32) * scale
        s = s - slopes[None, :, :, None, None] * dist_s
        p_s = masked_softmax(s, mask_s)
        o_s = jnp.einsum('bgrqm,bgqmd->bqgrd', p_s.astype(vg.dtype), vg)
        kw = lax.dynamic_slice_in_dim(kw_pad, i * QB, WINDOW + QB, axis=1)
        vw = lax.dynamic_slice_in_dim(vw_pad, i * QB, WINDOW + QB, axis=1)
        kpos = i * QB - WINDOW + jnp.arange(WINDOW + QB)
        dist = t[:, None] - kpos[None, :]
        s = jnp.einsum('bqgrd,bkgd->bgrqk', qi, kw).astype(jnp.float32) * scale
        s = s - slopes[:, :, None, None] * dist.astype(jnp.float32)
        p_w = masked_softmax(s, (dist >= 0) & (dist < WINDOW) & (kpos[None, :] >= 0))
        o_w = jnp.einsum('bgrqk,bkgd->bqgrd', p_w.astype(vw.dtype), vw)
        return gi[..., 0:1] * o_c + gi[..., 1:2] * o_s + gi[..., 2:3] * o_w

    o = lax.map(block, (jnp.arange(S // QB), to_blocks(qg), to_blocks(gates)))
    return from_blocks(o).reshape(B, S, NSA_HEADS * NSA_DH)


def memory_cross_attn(x, memn, wq, wkv, wo):
    B, S, _ = x.shape
    Mm = memn.shape[1]
    q = (x @ wq).reshape(B, S, XA_HEADS, XA_DH)
    k, v = jnp.split(memn @ wkv, 2, axis=-1)
    k = k.reshape(B, Mm, XA_HEADS, XA_DH)
    v = v.reshape(B, Mm, XA_HEADS, XA_DH)
    s = jnp.einsum('bshd,bmhd->bhsm', q, k).astype(jnp.float32) * (XA_DH ** -0.5)
    p = jax.nn.softmax(s, axis=-1)
    o = jnp.einsum('bhsm,bmhd->bshd', p.astype(v.dtype), v).reshape(B, S, XA_HEADS * XA_DH)
    return o @ wo


def conv_ffn(x, w_up, conv_w, conv_b, w_down):
    u = x @ w_up
    C = u.shape[-1]
    u = lax.conv_general_dilated(u, conv_w[:, None, :].astype(u.dtype), window_strides=(1,),
                                 padding=[(CONV_W - 1, 0)],
                                 dimension_numbers=('NWC', 'WIO', 'NWC'),
                                 feature_group_count=C) + conv_b
    a, g = jnp.split(u, 2, axis=-1)
    return (a * jax.nn.silu(g)) @ w_down


def setup_inputs(seed: int = 0) -> dict:
    key = jax.random.key(seed)
    keys = iter(jax.random.split(key, 40))
    L = DEPTH

    def nrm(shape, scale):
        return jax.random.normal(next(keys), shape, jnp.float32) * scale

    def gain(shape):
        return 1.0 + nrm(shape, 0.02)

    cmp_in = CMP_BLOCK * NSA_DH
    return {
        'x': nrm((BATCH, SEQ, D_MODEL), 1.0),
        'mem': nrm((BATCH, MEM_LEN, D_MODEL), 1.0),
        'ln_in_g': gain((D_MODEL,)),
        'ln_in_b': nrm((D_MODEL,), 0.02),
        'ln_mem_g': gain((D_MODEL,)),
        'ln_mem_b': nrm((D_MODEL,), 0.02),
        'w_in': nrm((L, D_MODEL, IN_COLS), D_MODEL ** -0.5),
        'mla_cq_g': gain((L, MLA_Q_RANK)),
        'mla_w_uq': nrm((L, MLA_Q_RANK, MLA_HEADS * (MLA_NOPE + MLA_ROPE)), MLA_Q_RANK ** -0.5),
        'mla_ckv_g': gain((L, MLA_KV_RANK)),
        'mla_w_ukv': nrm((L, MLA_KV_RANK, MLA_HEADS * (MLA_NOPE + MLA_V)), MLA_KV_RANK ** -0.5),
        'nsa_pe_k': nrm((L, CMP_BLOCK, NSA_DH), 0.02),
        'nsa_w1_k': nrm((L, cmp_in, CMP_HIDDEN), cmp_in ** -0.5),
        'nsa_b1_k': nrm((L, CMP_HIDDEN), 0.02),
        'nsa_w2_k': nrm((L, CMP_HIDDEN, NSA_DH), CMP_HIDDEN ** -0.5),
        'nsa_pe_v': nrm((L, CMP_BLOCK, NSA_DH), 0.02),
        'nsa_w1_v': nrm((L, cmp_in, CMP_HIDDEN), cmp_in ** -0.5),
        'nsa_b1_v': nrm((L, CMP_HIDDEN), 0.02),
        'nsa_w2_v': nrm((L, CMP_HIDDEN, NSA_DH), CMP_HIDDEN ** -0.5),
        'w_out': nrm((L, MIX_WIDTH, D_MODEL), DEEPNORM_BETA * MIX_WIDTH ** -0.5),
        'ln1_g': gain((L, D_MODEL)),
        'ln1_b': nrm((L, D_MODEL), 0.02),
        'xa_wq': nrm((L, D_MODEL, XA_HEADS * XA_DH), D_MODEL ** -0.5),
        'xa_wkv': nrm((L, D_MODEL, 2 * XA_HEADS * XA_DH), D_MODEL ** -0.5),
        'xa_wo': nrm((L, XA_HEADS * XA_DH, D_MODEL), DEEPNORM_BETA * (XA_HEADS * XA_DH) ** -0.5),
        'ln2_g': gain((L, D_MODEL)),
        'ln2_b': nrm((L, D_MODEL), 0.02),
        'ffn_w_up': nrm((L, D_MODEL, 2 * D_FF), D_MODEL ** -0.5),
        'ffn_conv_w': nrm((L, CONV_W, 2 * D_FF), CONV_W ** -0.5),
        'ffn_conv_b': nrm((L, 2 * D_FF), 0.02),
        'ffn_w_down': nrm((L, D_FF, D_MODEL), DEEPNORM_BETA * D_FF ** -0.5),
        'ln3_g': gain((L, D_MODEL)),
        'ln3_b': nrm((L, D_MODEL), 0.02),
    }


def reference(x, mem, ln_in_g, ln_in_b, ln_mem_g, ln_mem_b, w_in,
              mla_cq_g, mla_w_uq, mla_ckv_g, mla_w_ukv,
              nsa_pe_k, nsa_w1_k, nsa_b1_k, nsa_w2_k,
              nsa_pe_v, nsa_w1_v, nsa_b1_v, nsa_w2_v,
              w_out, ln1_g, ln1_b, xa_wq, xa_wkv, xa_wo, ln2_g, ln2_b,
              ffn_w_up, ffn_conv_w, ffn_conv_b, ffn_w_down, ln3_g, ln3_b):
    S = x.shape[1]
    x = layer_norm(x, ln_in_g, ln_in_b)
    memn = layer_norm(mem, ln_mem_g, ln_mem_b)
    cos, sin = rope_tables(S)
    split_at = [int(v) for v in np.cumsum(IN_SPLITS)[:-1]]
    for l in range(DEPTH):
        h = x @ w_in[l]
        cq, ckv, kr, nq, kc, vc, ks, vs, kw, vw, gl = jnp.split(h, split_at, axis=-1)
        o_a = mla_mixer(cq, ckv, kr, mla_cq_g[l], mla_w_uq[l], mla_ckv_g[l], mla_w_ukv[l], cos, sin)
        o_b = nsa_mixer(nq, kc, vc, ks, vs, kw, vw, gl,
                        nsa_pe_k[l], nsa_w1_k[l], nsa_b1_k[l], nsa_w2_k[l],
                        nsa_pe_v[l], nsa_w1_v[l], nsa_b1_v[l], nsa_w2_v[l])
        mix = jnp.concatenate([o_a, o_b], axis=-1) @ w_out[l]
        x = layer_norm(DEEPNORM_ALPHA * x + mix, ln1_g[l], ln1_b[l])
        xa = memory_cross_attn(x, memn, xa_wq[l], xa_wkv[l], xa_wo[l])
        x = layer_norm(DEEPNORM_ALPHA * x + xa, ln2_g[l], ln2_b[l])
        f = conv_ffn(x, ffn_w_up[l], ffn_conv_w[l], ffn_conv_b[l], ffn_w_down[l])
        x = layer_norm(DEEPNORM_ALPHA * x + f, ln3_g[l], ln3_b[l])
    return x
```

```python
import functools

import jax
import jax.numpy as jnp
from jax import lax
from jax.experimental import pallas as pl
from jax.experimental.pallas import tpu as pltpu

F32 = jnp.float32
MXU_DT = jnp.bfloat16

D_MODEL = 1024
MEM_LEN = 256
MLA_HEADS = 8
MLA_NOPE = 64
MLA_ROPE = 32
MLA_V = 64
MLA_Q_RANK = 384
MLA_KV_RANK = 256
ROPE_BASE = 10000.0
NSA_HEADS = 8
NSA_G = 2
NSA_R = 4
NSA_DH = 64
CMP_STRIDE = 16
CMP_BLOCK = 32
CMP_HIDDEN = 128
SEL_BLOCK = 64
N_SEL = 16
WINDOW = 512
FORCED_SCORE = 1e4
XA_HEADS = 4
XA_DH = 256
D_FF = 2816
DEPTH = 2
ALPHA = (2.0 * DEPTH) ** 0.25
LN_EPS = 1e-5
RMS_EPS = 1e-6
NEG = -1e30

LANE = 128
HP = 128
VMEM_LIMIT = 56 * 1024 * 1024

_O_CQ, _O_CKV, _O_KR, _O_NQ = 0, 384, 640, 672
_O_KV6, _O_GL, _IN_COLS = 1184, 1952, 1976
_E_CQ, _E_CKV, _E_NQ, _E_KCVC, _E_KVSW, _E_KRA, _E_KRB, _E_GL, _E_END = (
    0, 384, 640, 1152, 1408, 1920, 2048, 2176, 2304)

_NT = (((1,), (1,)), ((), ()))


def _params(n_parallel=1):
    return pltpu.CompilerParams(
        dimension_semantics=("arbitrary",) * n_parallel,
        vmem_limit_bytes=VMEM_LIMIT)


def _const_spec(shape):
    nd = len(shape)
    return pl.BlockSpec(shape, lambda *_: (0,) * nd)


def _ln_rows(v, g, b):
    mu = jnp.mean(v, -1, keepdims=True)
    d = v - mu
    var = jnp.mean(d * d, -1, keepdims=True)
    return d * lax.rsqrt(var + LN_EPS) * g + b


def _dot(a, b):
    return jnp.dot(a, b, preferred_element_type=F32)


def _dot_nt(a, b):
    return lax.dot_general(a, b, _NT, preferred_element_type=F32)


def _ln_body(x_ref, g_ref, b_ref, o_ref):
    o_ref[...] = _ln_rows(x_ref[...], g_ref[...], b_ref[...])


def _layer_norm(x2d, g, b, tm):
    t, d = x2d.shape
    return pl.pallas_call(
        _ln_body,
        grid=(t // tm,),
        in_specs=[pl.BlockSpec((tm, d), lambda i: (i, 0)),
                  _const_spec((1, d)), _const_spec((1, d))],
        out_specs=pl.BlockSpec((tm, d), lambda i: (i, 0)),
        out_shape=jax.ShapeDtypeStruct((t, d), F32),
        compiler_params=_params(1),
        name="ln_rows",
    )(x2d, g.reshape(1, d), b.reshape(1, d))


def _matmul_body(a_ref, w_ref, o_ref):
    o_ref[...] = _dot(a_ref[...].astype(MXU_DT), w_ref[...]).astype(o_ref.dtype)


def _matmul(a, w, tn, out_dtype):
    m, k = a.shape
    n = w.shape[1]
    return pl.pallas_call(
        _matmul_body,
        grid=(n // tn,),
        in_specs=[_const_spec((m, k)), pl.BlockSpec((k, tn), lambda j: (0, j))],
        out_specs=pl.BlockSpec((m, tn), lambda j: (0, j)),
        out_shape=jax.ShapeDtypeStruct((m, n), out_dtype),
        compiler_params=_params(1),
        name="mem_kv_proj",
    )(a, w)


def _rms_rows(v, g):
    return v * lax.rsqrt(jnp.mean(v * v, -1, keepdims=True) + RMS_EPS) * g


def _inproj_body(x_ref, win_ref, cqg_ref, ckvg_ref, wqa_ref, wqb_ref, wkk_ref, wkv_ref,
                 tcq_ref, tsq_ref, tck_ref, tsk_ref,
                 q_ref, k_ref, v_ref, nq_ref, kcvc_ref, kvsw_ref, gate_ref):
    xb = x_ref[...].astype(MXU_DT)
    h = _dot(xb, win_ref[...])
    cqn = _rms_rows(h[:, _E_CQ:_E_CKV], cqg_ref[...]).astype(MXU_DT)
    qa = _dot(cqn, wqa_ref[...])
    qb = _dot(cqn, wqb_ref[...])
    ckvn = _rms_rows(h[:, _E_CKV:_E_NQ], ckvg_ref[...]).astype(MXU_DT)
    kn = _dot(ckvn, wkk_ref[...])
    v_ref[...] = _dot(ckvn, wkv_ref[...]).astype(v_ref.dtype)
    krr = h[:, _E_KRA:_E_KRB] * tck_ref[...] + h[:, _E_KRB:_E_GL] * tsk_ref[...]
    tcq = tcq_ref[...]
    tsq = tsq_ref[...]
    for hh in range(MLA_HEADS):
        sl = slice(hh * HP, (hh + 1) * HP)
        q_ref[:, sl] = (qa[:, sl] * tcq + qb[:, sl] * tsq).astype(q_ref.dtype)
        k_ref[:, sl] = (kn[:, sl] + krr).astype(k_ref.dtype)
    nq_ref[...] = h[:, _E_NQ:_E_KCVC].astype(nq_ref.dtype)
    kcvc_ref[...] = h[:, _E_KCVC:_E_KVSW]
    kvsw_ref[...] = h[:, _E_KVSW:_E_KRA].astype(kvsw_ref.dtype)
    gate_ref[...] = jax.nn.sigmoid(h[:, _E_GL:_E_END])


def _inproj(x2d, lw, tabs, seq, tm):
    t = x2d.shape[0]
    npos = seq // tm
    row = lambda i: (i, 0)
    pos = lambda i: (i % npos, 0)
    hw = MLA_HEADS * HP
    out_shape = [
        jax.ShapeDtypeStruct((t, hw), MXU_DT),
        jax.ShapeDtypeStruct((t, hw), MXU_DT),
        jax.ShapeDtypeStruct((t, hw), MXU_DT),
        jax.ShapeDtypeStruct((t, 512), MXU_DT),
        jax.ShapeDtypeStruct((t, 256), F32),
        jax.ShapeDtypeStruct((t, 512), MXU_DT),
        jax.ShapeDtypeStruct((t, LANE), F32),
    ]
    return pl.pallas_call(
        _inproj_body,
        grid=(t // tm,),
        in_specs=[pl.BlockSpec((tm, D_MODEL), row),
                  _const_spec((D_MODEL, _E_END)),
                  _const_spec((1, MLA_Q_RANK)), _const_spec((1, MLA_KV_RANK)),
                  _const_spec((MLA_Q_RANK, hw)), _const_spec((MLA_Q_RANK, hw)),
                  _const_spec((MLA_KV_RANK, hw)), _const_spec((MLA_KV_RANK, hw)),
                  pl.BlockSpec((tm, HP), pos), pl.BlockSpec((tm, HP), pos),
                  pl.BlockSpec((tm, HP), pos), pl.BlockSpec((tm, HP), pos)],
        out_specs=[pl.BlockSpec((tm, s.shape[1]), row) for s in out_shape],
        out_shape=out_shape,
        compiler_params=_params(1),
        name="in_proj",
    )(x2d, lw["w_in"], lw["cq_g"], lw["ckv_g"], lw["wqa"], lw["wqb"], lw["wkk"], lw["wkv"],
      tabs["cq"], tabs["sq"], tabs["ck"], tabs["sk"])


def _mla_attn_body(q_ref, k_ref, v_ref, o_ref, *, tq):
    qi = pl.program_id(2)
    q = q_ref[...]

    def step(kt, carry, diag):
        m, l, acc = carry
        off = pl.multiple_of(kt * tq, tq)
        s = _dot_nt(q, k_ref[pl.ds(off, tq), :])
        if diag:
            row = lax.broadcasted_iota(jnp.int32, (tq, tq), 0)
            col = lax.broadcasted_iota(jnp.int32, (tq, tq), 1)
            s = jnp.where(col <= row, s, NEG)
        m_new = jnp.maximum(m, jnp.max(s, -1, keepdims=True))
        alpha = jnp.exp(m - m_new)
        p = jnp.exp(s - m_new)
        l = alpha * l + jnp.sum(p, -1, keepdims=True)
        acc = alpha * acc + _dot(p.astype(MXU_DT), v_ref[pl.ds(off, tq), :])
        return m_new, l, acc

    init = (jnp.full((tq, 1), NEG, F32), jnp.zeros((tq, 1), F32), jnp.zeros((tq, HP), F32))
    carry = lax.fori_loop(0, qi, lambda kt, c: step(kt, c, False), init)
    _, l, acc = step(qi, carry, True)
    o_ref[...] = (acc / l)[:, :MLA_V]


def _mla_attn(q, k, v, batch, seq, tq):
    nq = seq // tq
    t = batch * seq
    return pl.pallas_call(
        functools.partial(_mla_attn_body, tq=tq),
        grid=(batch, MLA_HEADS, nq),
        in_specs=[pl.BlockSpec((tq, HP), lambda b, h, i: (b * nq + i, h)),
                  pl.BlockSpec((seq, HP), lambda b, h, i: (b, h)),
                  pl.BlockSpec((seq, HP), lambda b, h, i: (b, h))],
        out_specs=pl.BlockSpec((None, tq, MLA_V), lambda b, h, i: (h, b * nq + i, 0)),
        out_shape=jax.ShapeDtypeStruct((MLA_HEADS, t, MLA_V), F32),
        compiler_params=_params(3),
        name="mla_attn",
    )(q, k, v)


def _compress_body(c_ref, pe_ref, w1_ref, b1_ref, w2_ref, o_ref):
    c = c_ref[...]
    half = CMP_STRIDE * NSA_DH
    lo = (c + pe_ref[0:1, :]).astype(MXU_DT)
    hi = (c + pe_ref[1:2, :]).astype(MXU_DT)
    u = _dot(lo, w1_ref[0:half, :])
    w = _dot(hi, w1_ref[half:2 * half, :])
    w = pltpu.roll(w, w.shape[0] - 1, 0)
    hid = jax.nn.gelu(u + w + b1_ref[...])
    o_ref[...] = _dot(hid.astype(MXU_DT), w2_ref[...]).astype(o_ref.dtype)


def _compress(chunks, pe, w1, b1, w2):
    _, bg, nch, cw = chunks.shape
    return pl.pallas_call(
        _compress_body,
        grid=(2, bg),
        in_specs=[pl.BlockSpec((None, None, nch, cw), lambda s, j: (s, j, 0, 0)),
                  pl.BlockSpec((None, 2, cw), lambda s, j: (s, 0, 0)),
                  pl.BlockSpec((None, 2 * cw, CMP_HIDDEN), lambda s, j: (s, 0, 0)),
                  pl.BlockSpec((None, 1, CMP_HIDDEN), lambda s, j: (s, 0, 0)),
                  pl.BlockSpec((None, CMP_HIDDEN, NSA_DH), lambda s, j: (s, 0, 0))],
        out_specs=pl.BlockSpec((None, None, nch, NSA_DH), lambda s, j: (s, j, 0, 0)),
        out_shape=jax.ShapeDtypeStruct((2, bg, nch, NSA_DH), MXU_DT),
        compiler_params=_params(2),
        name="nsa_compress",
    )(chunks, pe, w1, b1, w2)


def _cmp_attn_body(slope_ref, q_ref, kc_ref, vct_ref, ovl_ref, oc_ref, sel_ref, *, tq, n_sel):
    g = pl.program_id(1)
    q0 = pl.program_id(2) * tq
    nch = kc_ref.shape[0]
    ns = sel_ref.shape[0]
    kc = kc_ref[...]
    vct = vct_ref[...]
    n_i = lax.broadcasted_iota(jnp.int32, (nch, tq), 0)
    t_i = q0 + lax.broadcasted_iota(jnp.int32, (nch, tq), 1)
    dist = t_i - (n_i * CMP_STRIDE + (CMP_BLOCK - 1))
    valid = dist >= 0
    distf = dist.astype(F32)
    ps = []
    for r in range(NSA_R):
        s = _dot_nt(kc, q_ref[r])
        s = jnp.where(valid, s - slope_ref[g * NSA_R + r] * distf, NEG)
        m = jnp.max(s, 0, keepdims=True)
        e = jnp.where(valid, jnp.exp(s - m), 0.0)
        d = jnp.sum(e, 0, keepdims=True)
        p = (e / jnp.maximum(d, 1e-30)).astype(MXU_DT)
        oc_ref[r] = _dot(vct, p)
        ps.append(p)
    imp = _dot(ovl_ref[...], jnp.concatenate(ps, axis=0))

    blk = lax.broadcasted_iota(jnp.int32, (ns, tq), 0)
    cur = (q0 + lax.broadcasted_iota(jnp.int32, (ns, tq), 1)) // SEL_BLOCK
    blkf = blk.astype(F32)
    validb = blk <= cur
    score = jnp.where(validb, imp, -1.0)
    score = jnp.where(blk == 0, FORCED_SCORE, score)
    score = jnp.where(blk == cur, FORCED_SCORE, score)
    score = jnp.where(blk == cur - 1, FORCED_SCORE, score)
    sel = jnp.zeros((ns, tq), F32)
    for _ in range(n_sel):
        mx = jnp.max(score, 0, keepdims=True)
        first = jnp.min(jnp.where(score == mx, blkf, float(ns)), 0, keepdims=True)
        hit = blkf == first
        sel = jnp.where(hit, 1.0, sel)
        score = jnp.where(hit, -3e38, score)
    sel_ref[...] = jnp.where(validb, sel, 0.0).astype(sel_ref.dtype)


def _cmp_attn(slopes, qh, kc, vct, ovl4, batch, seq, tq):
    nq = seq // tq
    nch = kc.shape[-2]
    ns = seq // SEL_BLOCK
    n_sel = min(N_SEL, ns)
    return pl.pallas_call(
        functools.partial(_cmp_attn_body, tq=tq, n_sel=n_sel),
        grid=(batch, NSA_G, nq),
        in_specs=[pl.BlockSpec(memory_space=pltpu.SMEM),
                  pl.BlockSpec((None, None, NSA_R, tq, NSA_DH), lambda b, g, i: (b, g, 0, i, 0)),
                  pl.BlockSpec((None, None, nch, NSA_DH), lambda b, g, i: (b, g, 0, 0)),
                  pl.BlockSpec((None, None, NSA_DH, nch), lambda b, g, i: (b, g, 0, 0)),
                  _const_spec((ns, NSA_R * nch))],
        out_specs=[pl.BlockSpec((None, None, NSA_R, NSA_DH, tq), lambda b, g, i: (b, g, 0, 0, i)),
                   pl.BlockSpec((None, None, ns, tq), lambda b, g, i: (b, g, 0, i))],
        out_shape=[jax.ShapeDtypeStruct((batch, NSA_G, NSA_R, NSA_DH, seq), F32),
                   jax.ShapeDtypeStruct((batch, NSA_G, ns, seq), MXU_DT)],
        compiler_params=_params(3),
        name="nsa_cmp_attn_topk",
    )(slopes, qh, kc, vct, ovl4)


def _sel_attn_body(slope_ref, q_ref, k_ref, v_ref, sel_ref, e_ref, o_ref, *, tq, tk):
    g = pl.program_id(1)
    q0 = pl.program_id(2) * tq
    rows = NSA_R * tq
    q = q_ref[...].reshape(rows, NSA_DH)
    selq = sel_ref[...]
    t_q = q0 + lax.broadcasted_iota(jnp.int32, (tq, tk), 0)
    col = lax.broadcasted_iota(jnp.int32, (tq, tk), 1)
    slope = jnp.concatenate(
        [jnp.full((1, 1, 1), slope_ref[g * NSA_R + r], F32) for r in range(NSA_R)], axis=0)

    def step(kt, carry):
        m, l, acc = carry
        off = pl.multiple_of(kt * tk, tk)
        s = _dot_nt(q, k_ref[pl.ds(off, tk), :]).reshape(NSA_R, tq, tk)
        selm = _dot(selq, e_ref[:, pl.ds(off, tk)])
        dist = t_q - (off + col)
        ok = (selm > 0.5) & (dist >= 0)
        s = jnp.where(ok[None], s - slope * dist.astype(F32)[None], NEG)
        m_new = jnp.maximum(m, jnp.max(s, -1, keepdims=True))
        alpha = jnp.exp(m - m_new)
        p = jnp.exp(s - m_new)
        l = alpha * l + jnp.sum(p, -1, keepdims=True)
        pv = _dot(p.reshape(rows, tk).astype(MXU_DT), v_ref[pl.ds(off, tk), :])
        acc = alpha.reshape(rows, 1) * acc + pv
        return m_new, l, acc

    init = (jnp.full((NSA_R, tq, 1), NEG, F32), jnp.zeros((NSA_R, tq, 1), F32),
            jnp.zeros((rows, NSA_DH), F32))
    nkt = (q0 + tq - 1) // tk + 1
    _, l, acc = lax.fori_loop(0, nkt, step, init)
    o_ref[...] = (acc / l.reshape(rows, 1)).reshape(NSA_R, tq, NSA_DH)


def _sel_attn(slopes, qh, ks, vs, sel, emat, batch, seq, tq, tk):
    nq = seq // tq
    ns = seq // SEL_BLOCK
    kv_spec = pl.BlockSpec((None, None, seq, NSA_DH), lambda b, g, i: (b, g, 0, 0))
    qo_spec = pl.BlockSpec((None, None, NSA_R, tq, NSA_DH), lambda b, g, i: (b, g, 0, i, 0))
    return pl.pallas_call(
        functools.partial(_sel_attn_body, tq=tq, tk=tk),
        grid=(batch, NSA_G, nq),
        in_specs=[pl.BlockSpec(memory_space=pltpu.SMEM), qo_spec, kv_spec, kv_spec,
                  pl.BlockSpec((None, None, tq, ns), lambda b, g, i: (b, g, i, 0)),
                  _const_spec((ns, seq))],
        out_specs=qo_spec,
        out_shape=jax.ShapeDtypeStruct((batch, NSA_G, NSA_R, seq, NSA_DH), F32),
        compiler_params=_params(3),
        name="nsa_sel_attn",
    )(slopes, qh, ks, vs, sel, emat)


def _win_attn_body(slope_ref, q_ref, k_ref, v_ref, o_ref, *, tq, wk):
    g = pl.program_id(1)
    q0 = pl.program_id(2) * tq
    seq = k_ref.shape[0]
    rows = NSA_R * tq
    q = q_ref[...].reshape(rows, NSA_DH)
    start = pl.multiple_of(jnp.clip(q0 - WINDOW, 0, seq - wk), LANE)
    s = _dot_nt(q, k_ref[pl.ds(start, wk), :]).reshape(NSA_R, tq, wk)
    t_q = q0 + lax.broadcasted_iota(jnp.int32, (tq, wk), 0)
    dist = t_q - (start + lax.broadcasted_iota(jnp.int32, (tq, wk), 1))
    ok = (dist >= 0) & (dist < WINDOW)
    slope = jnp.concatenate(
        [jnp.full((1, 1, 1), slope_ref[g * NSA_R + r], F32) for r in range(NSA_R)], axis=0)
    s = jnp.where(ok[None], s - slope * dist.astype(F32)[None], NEG)
    m = jnp.max(s, -1, keepdims=True)
    e = jnp.exp(s - m)
    d = jnp.sum(e, -1, keepdims=True)
    pv = _dot(e.reshape(rows, wk).astype(MXU_DT), v_ref[pl.ds(start, wk), :])
    o_ref[...] = (pv / d.reshape(rows, 1)).reshape(NSA_R, tq, NSA_DH)


def _win_attn(slopes, qh, kw, vw, batch, seq, tq):
    nq = seq // tq
    wk = WINDOW + tq
    kv_spec = pl.BlockSpec((None, None, seq, NSA_DH), lambda b, g, i: (b, g, 0, 0))
    qo_spec = pl.BlockSpec((None, None, NSA_R, tq, NSA_DH), lambda b, g, i: (b, g, 0, i, 0))
    return pl.pallas_call(
        functools.partial(_win_attn_body, tq=tq, wk=wk),
        grid=(batch, NSA_G, nq),
        in_specs=[pl.BlockSpec(memory_space=pltpu.SMEM), qo_spec, kv_spec, kv_spec],
        out_specs=qo_spec,
        out_shape=jax.ShapeDtypeStruct((batch, NSA_G, NSA_R, seq, NSA_DH), F32),
        compiler_params=_params(3),
        name="nsa_win_attn",
    )(slopes, qh, kw, vw)


def _outproj_body(x_ref, oa_ref, oc_ref, os_ref, ow_ref, gate_ref, ge_ref, wo_ref, g_ref, b_ref,
                  y_ref):
    gt = gate_ref[...]
    g1 = gt.astype(MXU_DT)
    r1 = gt - g1.astype(F32)
    g2 = r1.astype(MXU_DT)
    g3 = (r1 - g2.astype(F32)).astype(MXU_DT)
    gexp = _dot(jnp.concatenate([g1, g2, g3], axis=1), ge_ref[...])
    hw = NSA_HEADS * NSA_DH
    nsa = (gexp[:, 0:hw] * oc_ref[...] + gexp[:, hw:2 * hw] * os_ref[...]
           + gexp[:, 2 * hw:3 * hw] * ow_ref[...])
    wa = MLA_HEADS * MLA_V
    mix = _dot(oa_ref[...].astype(MXU_DT), wo_ref[0:wa, :])
    mix = mix + _dot(nsa.astype(MXU_DT), wo_ref[wa:wa + hw, :])
    y_ref[...] = _ln_rows(ALPHA * x_ref[...] + mix, g_ref[...], b_ref[...])


def _outproj(x2d, oa, oc, osel, ow, gates, gexp_mat, wo, g, b, tm):
    t = x2d.shape[0]
    row = lambda i: (i, 0)
    return pl.pallas_call(
        _outproj_body,
        grid=(t // tm,),
        in_specs=[pl.BlockSpec((tm, D_MODEL), row),
                  pl.BlockSpec((tm, 512), row), pl.BlockSpec((tm, 512), row),
                  pl.BlockSpec((tm, 512), row), pl.BlockSpec((tm, 512), row),
                  pl.BlockSpec((tm, LANE), row),
                  _const_spec(gexp_mat.shape), _const_spec(wo.shape),
                  _const_spec((1, D_MODEL)), _const_spec((1, D_MODEL))],
        out_specs=pl.BlockSpec((tm, D_MODEL), row),
        out_shape=jax.ShapeDtypeStruct((t, D_MODEL), F32),
        compiler_params=_params(1),
        name="out_proj_ln",
    )(x2d, oa, oc, osel, ow, gates, gexp_mat, wo, g.reshape(1, -1), b.reshape(1, -1))


def _xattn_body(x_ref, wq_ref, kv_ref, wo_ref, g_ref, b_ref, y_ref):
    x = x_ref[...]
    q = _dot(x.astype(MXU_DT), wq_ref[...]) * (XA_DH ** -0.5)
    hw = XA_HEADS * XA_DH
    y = jnp.zeros_like(x)
    for h in range(XA_HEADS):
        sl = slice(h * XA_DH, (h + 1) * XA_DH)
        s = _dot_nt(q[:, sl].astype(MXU_DT), kv_ref[:, sl])
        e = jnp.exp(s - jnp.max(s, -1, keepdims=True))
        p = e / jnp.sum(e, -1, keepdims=True)
        o = _dot(p.astype(MXU_DT), kv_ref[:, hw + h * XA_DH:hw + (h + 1) * XA_DH])
        y = y + _dot(o.astype(MXU_DT), wo_ref[sl, :])
    y_ref[...] = _ln_rows(ALPHA * x + y, g_ref[...], b_ref[...])


def _xattn(x2d, wq, memkv, wo, g, b, batch, seq, tm):
    ns = seq // tm
    mem = memkv.shape[0] // batch
    row = lambda bb, i: (bb * ns + i, 0)
    return pl.pallas_call(
        _xattn_body,
        grid=(batch, ns),
        in_specs=[pl.BlockSpec((tm, D_MODEL), row),
                  _const_spec(wq.shape),
                  pl.BlockSpec((mem, memkv.shape[1]), lambda bb, i: (bb, 0)),
                  _const_spec(wo.shape),
                  _const_spec((1, D_MODEL)), _const_spec((1, D_MODEL))],
        out_specs=pl.BlockSpec((tm, D_MODEL), row),
        out_shape=jax.ShapeDtypeStruct((batch * seq, D_MODEL), F32),
        compiler_params=_params(2),
        name="mem_xattn_ln",
    )(x2d, wq, memkv, wo, g.reshape(1, -1), b.reshape(1, -1))


_FF_CHUNK = 256
_HALO = 8


def _ffn_body(x_ref, xp_ref, wup_ref, cw_ref, cb_ref, wdn_ref, g_ref, b_ref, y_ref):
    first = pl.program_id(1) == 0
    x = x_ref[...]
    xp = jnp.where(first, 0.0, xp_ref[...])
    xe = jnp.concatenate([xp, x], axis=0).astype(MXU_DT)

    def conv(c0):
        u = _dot(xe, wup_ref[:, c0:c0 + _FF_CHUNK])
        u1 = pltpu.roll(u, 1, 0)
        u2 = pltpu.roll(u, 2, 0)
        cw = cw_ref[:, c0:c0 + _FF_CHUNK]
        r = u * cw[2:3, :] + u1 * cw[1:2, :] + u2 * cw[0:1, :] + cb_ref[:, c0:c0 + _FF_CHUNK]
        return r[_HALO:, :]

    acc = jnp.zeros_like(x)
    for c in range(D_FF // _FF_CHUNK):
        a = conv(c * _FF_CHUNK)
        gg = conv(D_FF + c * _FF_CHUNK)
        act = (a * jax.nn.silu(gg)).astype(MXU_DT)
        acc = acc + _dot(act, wdn_ref[c * _FF_CHUNK:(c + 1) * _FF_CHUNK, :])
    y_ref[...] = _ln_rows(ALPHA * x + acc, g_ref[...], b_ref[...])


def _ffn(x2d, wup, cw, cb, wdn, g, b, batch, seq, tm):
    ns = seq // tm
    hb = tm // _HALO
    row = lambda bb, i: (bb * ns + i, 0)
    prev = lambda bb, i: (jnp.maximum((bb * ns + i) * hb - 1, 0), 0)
    return pl.pallas_call(
        _ffn_body,
        grid=(batch, ns),
        in_specs=[pl.BlockSpec((tm, D_MODEL), row),
                  pl.BlockSpec((_HALO, D_MODEL), prev),
                  _const_spec(wup.shape), _const_spec(cw.shape), _const_spec(cb.shape),
                  _const_spec(wdn.shape),
                  _const_spec((1, D_MODEL)), _const_spec((1, D_MODEL))],
        out_specs=pl.BlockSpec((tm, D_MODEL), row),
        out_shape=jax.ShapeDtypeStruct((batch * seq, D_MODEL), F32),
        compiler_params=_params(2),
        name="conv_ffn_ln",
    )(x2d, x2d, wup, cw, cb, wdn, g.reshape(1, -1), b.reshape(1, -1))


def _rot_half_cols(w):
    half = MLA_ROPE // 2
    return jnp.concatenate([-w[..., half:], w[..., :half]], axis=-1)


def _prep_layer(w_in, cq_g, w_uq, ckv_g, w_ukv):
    d = w_in.shape[0]
    z = lambda n: jnp.zeros((d, n), F32)
    kr = w_in[:, _O_KR:_O_NQ]
    w_ext = jnp.concatenate([
        w_in[:, _O_CQ:_O_CKV], w_in[:, _O_CKV:_O_KR],
        w_in[:, _O_NQ:_O_KV6] * (NSA_DH ** -0.5),
        w_in[:, _O_KV6:_O_GL],
        z(MLA_NOPE), kr, z(HP - MLA_NOPE - MLA_ROPE),
        z(MLA_NOPE), _rot_half_cols(kr), z(HP - MLA_NOPE - MLA_ROPE),
        w_in[:, _O_GL:_IN_COLS], z(_E_END - _E_GL - (_IN_COLS - _O_GL))], axis=1)
    wq = w_uq.reshape(MLA_Q_RANK, MLA_HEADS, MLA_NOPE + MLA_ROPE)
    zq = jnp.zeros((MLA_Q_RANK, MLA_HEADS, HP - MLA_NOPE - MLA_ROPE), F32)
    wqa = jnp.concatenate([wq, zq], axis=-1).reshape(MLA_Q_RANK, MLA_HEADS * HP)
    wqb = jnp.concatenate([jnp.zeros_like(wq[..., :MLA_NOPE]), _rot_half_cols(wq[..., MLA_NOPE:]), zq],
                          axis=-1).reshape(MLA_Q_RANK, MLA_HEADS * HP)
    wkv = w_ukv.reshape(MLA_KV_RANK, MLA_HEADS, MLA_NOPE + MLA_V)
    zk = jnp.zeros((MLA_KV_RANK, MLA_HEADS, HP - MLA_NOPE), F32)
    wkk = jnp.concatenate([wkv[..., :MLA_NOPE], zk], axis=-1).reshape(MLA_KV_RANK, MLA_HEADS * HP)
    wvv = jnp.concatenate([wkv[..., MLA_NOPE:], zk], axis=-1).reshape(MLA_KV_RANK, MLA_HEADS * HP)
    return dict(w_in=w_ext.astype(MXU_DT), cq_g=cq_g.reshape(1, -1), ckv_g=ckv_g.reshape(1, -1),
                wqa=wqa.astype(MXU_DT), wqb=wqb.astype(MXU_DT),
                wkk=wkk.astype(MXU_DT), wkv=wvv.astype(MXU_DT))


def _rope_tabs(seq):
    inv = ROPE_BASE ** (-jnp.arange(0, MLA_ROPE, 2, dtype=F32) / MLA_ROPE)
    ang = jnp.arange(seq, dtype=F32)[:, None] * inv[None, :]
    cos, sin = jnp.cos(ang), jnp.sin(ang)
    pad = jnp.zeros((seq, HP - MLA_NOPE - MLA_ROPE), F32)
    tc = jnp.concatenate([jnp.ones((seq, MLA_NOPE), F32), cos, cos, pad], axis=1)
    ts = jnp.concatenate([jnp.zeros((seq, MLA_NOPE), F32), sin, sin, pad], axis=1)
    scale = (MLA_NOPE + MLA_ROPE) ** -0.5
    return dict(cq=tc * scale, sq=ts * scale, ck=tc, sk=ts)


def _nsa_consts(seq):
    nch = seq // CMP_STRIDE
    ns = seq // SEL_BLOCK
    c_start = jnp.arange(nch) * CMP_STRIDE
    s_start = jnp.arange(ns) * SEL_BLOCK
    ovl = jnp.clip(jnp.minimum(c_start[None, :] + CMP_BLOCK, s_start[:, None] + SEL_BLOCK)
                   - jnp.maximum(c_start[None, :], s_start[:, None]), 0).astype(F32) / CMP_BLOCK
    ovl4 = jnp.tile(ovl, (1, NSA_R)).astype(MXU_DT)
    emat = (jnp.arange(seq)[None, :] // SEL_BLOCK == jnp.arange(ns)[:, None]).astype(MXU_DT)
    slopes = 2.0 ** (-8.0 * jnp.arange(1, NSA_HEADS + 1, dtype=F32) / NSA_HEADS)
    lane = jnp.arange(LANE)
    col = jnp.arange(3 * NSA_HEADS * NSA_DH)
    ge = ((lane[:, None] % 3 == col[None, :] // (NSA_HEADS * NSA_DH))
          & (lane[:, None] // 3 == (col[None, :] // NSA_DH) % NSA_HEADS)
          & (lane[:, None] < 3 * NSA_HEADS))
    ge3 = jnp.tile(ge.astype(MXU_DT), (3, 1))
    return dict(ovl4=ovl4, emat=emat, slopes=slopes, ge3=ge3)


def _heads_first(a, batch, seq, lead):
    a = a.reshape(batch, seq, lead, NSA_G, NSA_DH)
    return a.transpose(2, 0, 3, 1, 4)


def _tok_major(a, batch, seq):
    return a.transpose(0, 3, 1, 2, 4).reshape(batch * seq, NSA_HEADS * NSA_DH)


def kernel(x, mem, ln_in_g, ln_in_b, ln_mem_g, ln_mem_b, w_in, mla_cq_g, mla_w_uq, mla_ckv_g, mla_w_ukv, nsa_pe_k, nsa_w1_k, nsa_b1_k, nsa_w2_k, nsa_pe_v, nsa_w1_v, nsa_b1_v, nsa_w2_v, w_out, ln1_g, ln1_b, xa_wq, xa_wkv, xa_wo, ln2_g, ln2_b, ffn_w_up, ffn_conv_w, ffn_conv_b, ffn_w_down, ln3_g, ln3_b):
    batch, seq, d = x.shape
    t = batch * seq
    depth = w_in.shape[0]
    tm = min(512, seq)
    tabs = _rope_tabs(seq)
    consts = _nsa_consts(seq)
    nch = seq // CMP_STRIDE

    xs = _layer_norm(x.reshape(t, d), ln_in_g, ln_in_b, tm)
    memn = _layer_norm(mem.reshape(-1, d), ln_mem_g, ln_mem_b, mem.shape[0] * mem.shape[1])

    for l in range(depth):
        lw = _prep_layer(w_in[l], mla_cq_g[l], mla_w_uq[l], mla_ckv_g[l], mla_w_ukv[l])
        q, k, v, nq, kcvc, kvsw, gates = _inproj(xs, lw, tabs, seq, tm)

        oa = _mla_attn(q, k, v, batch, seq, min(512, seq))
        oa = oa.transpose(1, 0, 2).reshape(t, MLA_HEADS * MLA_V)

        qh = nq.reshape(batch, seq, NSA_G, NSA_R, NSA_DH).transpose(0, 2, 3, 1, 4)
        chunks = _heads_first(kcvc, batch, seq, 2).reshape(2, batch * NSA_G, nch, CMP_STRIDE * NSA_DH)
        pe = jnp.stack([nsa_pe_k[l], nsa_pe_v[l]]).reshape(2, 2, CMP_STRIDE * NSA_DH)
        w1 = jnp.stack([nsa_w1_k[l], nsa_w1_v[l]]).astype(MXU_DT)
        b1 = jnp.stack([nsa_b1_k[l], nsa_b1_v[l]]).reshape(2, 1, CMP_HIDDEN)
        w2 = jnp.stack([nsa_w2_k[l], nsa_w2_v[l]]).astype(MXU_DT)
        cmp = _compress(chunks, pe, w1, b1, w2).reshape(2, batch, NSA_G, nch, NSA_DH)
        oc_t, sel_t = _cmp_attn(consts["slopes"], qh, cmp[0], cmp[1].transpose(0, 1, 3, 2),
                                consts["ovl4"], batch, seq, min(256, seq))
        oc = oc_t.transpose(0, 4, 1, 2, 3).reshape(t, NSA_HEADS * NSA_DH)
        sel = sel_t.transpose(0, 1, 3, 2)
        kv4 = _heads_first(kvsw, batch, seq, 4)
        osel = _sel_attn(consts["slopes"], qh, kv4[0], kv4[1], sel, consts["emat"],
                         batch, seq, min(128, seq), min(512, seq))
        owin = _win_attn(consts["slopes"], qh, kv4[2], kv4[3], batch, seq, min(128, seq))

        xs = _outproj(xs, oa, oc, _tok_major(osel, batch, seq), _tok_major(owin, batch, seq),
                      gates, consts["ge3"], w_out[l].astype(MXU_DT), ln1_g[l], ln1_b[l], tm)

        memkv = _matmul(memn, xa_wkv[l].astype(MXU_DT), 512, MXU_DT)
        xs = _xattn(xs, xa_wq[l].astype(MXU_DT), memkv, xa_wo[l].astype(MXU_DT),
                    ln2_g[l], ln2_b[l], batch, seq, tm)

        xs = _ffn(xs, ffn_w_up[l].astype(MXU_DT), ffn_conv_w[l], ffn_conv_b[l].reshape(1, -1),
                  ffn_w_down[l].astype(MXU_DT), ln3_g[l], ln3_b[l], batch, seq, tm)

    return xs.reshape(batch, seq, d)
```

```python
import functools

import jax
import jax.numpy as jnp
from jax import lax
from jax.experimental import pallas as pl
from jax.experimental.pallas import tpu as pltpu

F32 = jnp.float32
MXU_DT = jnp.bfloat16

D_MODEL = 1024
MLA_HEADS = 8
MLA_NOPE = 64
MLA_ROPE = 32
MLA_V = 64
MLA_Q_RANK = 384
MLA_KV_RANK = 256
ROPE_BASE = 10000.0
NSA_HEADS = 8
NSA_G = 2
NSA_R = 4
NSA_DH = 64
CMP_STRIDE = 16
CMP_BLOCK = 32
CMP_HIDDEN = 128
SEL_BLOCK = 64
N_SEL = 16
WINDOW = 512
FORCED_SCORE = 1e4
XA_HEADS = 4
XA_DH = 256
D_FF = 2816
DEPTH = 2
ALPHA = (2.0 * DEPTH) ** 0.25
LN_EPS = 1e-5
RMS_EPS = 1e-6
NEG = -1e30

LANE = 128
HP = 128
VMEM_LIMIT = 56 * 1024 * 1024

TM = 512
MLA_T = 512
CMP_TQ = 256
NSA_TQ = 128
NSA_TK = 256
POS_SPLIT = 128

_O_CQ, _O_CKV, _O_KR, _O_NQ = 0, 384, 640, 672
_O_KV6, _O_GL, _IN_COLS = 1184, 1952, 1976
_E_CQ, _E_CKV, _E_NQ, _E_KCVC, _E_KVSW, _E_KRA, _E_KRB, _E_GL, _E_END = (
    0, 384, 640, 1664, 1920, 2432, 2560, 2688, 2816)

_NT = (((1,), (1,)), ((), ()))


def _params(n_axes=1):
    return pltpu.CompilerParams(
        dimension_semantics=("arbitrary",) * n_axes,
        vmem_limit_bytes=VMEM_LIMIT)


def _const_spec(shape):
    nd = len(shape)
    return pl.BlockSpec(shape, lambda *_: (0,) * nd)


def _ln_rows(v, g, b):
    mu = jnp.mean(v, -1, keepdims=True)
    d = v - mu
    var = jnp.mean(d * d, -1, keepdims=True)
    return d * lax.rsqrt(var + LN_EPS) * g + b


def _dot(a, b):
    return jnp.dot(a, b, preferred_element_type=F32)


def _dot_nt(a, b):
    return lax.dot_general(a, b, _NT, preferred_element_type=F32)


def _ln_body(x_ref, g_ref, b_ref, o_ref):
    o_ref[...] = _ln_rows(x_ref[...], g_ref[...], b_ref[...])


def _layer_norm(x2d, g, b, tm):
    t, d = x2d.shape
    return pl.pallas_call(
        _ln_body,
        grid=(t // tm,),
        in_specs=[pl.BlockSpec((tm, d), lambda i: (i, 0)),
                  _const_spec((1, d)), _const_spec((1, d))],
        out_specs=pl.BlockSpec((tm, d), lambda i: (i, 0)),
        out_shape=jax.ShapeDtypeStruct((t, d), F32),
        compiler_params=_params(1),
        name="ln_rows",
    )(x2d, g.reshape(1, d), b.reshape(1, d))


def _matmul_body(a_ref, w_ref, o_ref):
    o_ref[...] = _dot(a_ref[...].astype(MXU_DT), w_ref[...]).astype(o_ref.dtype)


def _matmul(a, w, tn, out_dtype):
    m, k = a.shape
    n = w.shape[1]
    return pl.pallas_call(
        _matmul_body,
        grid=(n // tn,),
        in_specs=[_const_spec((m, k)), pl.BlockSpec((k, tn), lambda j: (0, j))],
        out_specs=pl.BlockSpec((m, tn), lambda j: (0, j)),
        out_shape=jax.ShapeDtypeStruct((m, n), out_dtype),
        compiler_params=_params(1),
        name="mem_kv_proj",
    )(a, w)


def _rms_rows(v, g):
    return v * lax.rsqrt(jnp.mean(v * v, -1, keepdims=True) + RMS_EPS) * g


def _inproj_body(x_ref, win_ref, cqg_ref, ckvg_ref, wqa_ref, wqb_ref, wkk_ref, wkv_ref,
                 tcq_ref, tsq_ref, tck_ref, tsk_ref,
                 q_ref, k_ref, v_ref, nq_ref, kcvc_ref, kvsw_ref, gate_ref):
    xb = x_ref[...].astype(MXU_DT)
    h = _dot(xb, win_ref[...])
    cqn = _rms_rows(h[:, _E_CQ:_E_CKV], cqg_ref[...]).astype(MXU_DT)
    qa = _dot(cqn, wqa_ref[...])
    qb = _dot(cqn, wqb_ref[...])
    ckvn = _rms_rows(h[:, _E_CKV:_E_NQ], ckvg_ref[...]).astype(MXU_DT)
    kn = _dot(ckvn, wkk_ref[...])
    v_ref[...] = _dot(ckvn, wkv_ref[...]).astype(v_ref.dtype)
    krr = h[:, _E_KRA:_E_KRB] * tck_ref[...] + h[:, _E_KRB:_E_GL] * tsk_ref[...]
    tcq = tcq_ref[...]
    tsq = tsq_ref[...]
    for hh in range(MLA_HEADS):
        sl = slice(hh * HP, (hh + 1) * HP)
        q_ref[:, sl] = (qa[:, sl] * tcq + qb[:, sl] * tsq).astype(q_ref.dtype)
        k_ref[:, sl] = (kn[:, sl] + krr).astype(k_ref.dtype)
    nq_ref[...] = h[:, _E_NQ:_E_KCVC].astype(nq_ref.dtype)
    kcvc_ref[...] = h[:, _E_KCVC:_E_KVSW]
    kvsw_ref[...] = h[:, _E_KVSW:_E_KRA].astype(kvsw_ref.dtype)
    gate_ref[...] = jax.nn.sigmoid(h[:, _E_GL:_E_END])


def _inproj(x2d, lw, tabs, seq, tm):
    t = x2d.shape[0]
    npos = seq // tm
    row = lambda i: (i, 0)
    pos = lambda i: (i % npos, 0)
    hw = MLA_HEADS * HP
    out_shape = [
        jax.ShapeDtypeStruct((t, hw), MXU_DT),
        jax.ShapeDtypeStruct((t, hw), MXU_DT),
        jax.ShapeDtypeStruct((t, MLA_HEADS * MLA_V), MXU_DT),
        jax.ShapeDtypeStruct((t, NSA_HEADS * HP), MXU_DT),
        jax.ShapeDtypeStruct((t, 256), F32),
        jax.ShapeDtypeStruct((t, 512), MXU_DT),
        jax.ShapeDtypeStruct((t, LANE), F32),
    ]
    return pl.pallas_call(
        _inproj_body,
        grid=(t // tm,),
        in_specs=[pl.BlockSpec((tm, D_MODEL), row),
                  _const_spec((D_MODEL, _E_END)),
                  _const_spec((1, MLA_Q_RANK)), _const_spec((1, MLA_KV_RANK)),
                  _const_spec((MLA_Q_RANK, hw)), _const_spec((MLA_Q_RANK, hw)),
                  _const_spec((MLA_KV_RANK, hw)), _const_spec((MLA_KV_RANK, MLA_HEADS * MLA_V)),
                  pl.BlockSpec((tm, HP), pos), pl.BlockSpec((tm, HP), pos),
                  pl.BlockSpec((tm, HP), pos), pl.BlockSpec((tm, HP), pos)],
        out_specs=[pl.BlockSpec((tm, s.shape[1]), row) for s in out_shape],
        out_shape=out_shape,
        compiler_params=_params(1),
        name="in_proj",
    )(x2d, lw["w_in"], lw["cq_g"], lw["ckv_g"], lw["wqa"], lw["wqb"], lw["wkk"], lw["wkv"],
      tabs["cq"], tabs["sq"], tabs["ck"], tabs["sk"])


_MLA_PAIR = 2


def _mla_attn_body(q_ref, k_ref, vt_ref, o_ref, *, tq):
    qi = pl.program_id(2)

    def step(kt, carry, diag):
        off = pl.multiple_of(kt * tq, tq)
        out = []
        for h in range(_MLA_PAIR):
            m, l, acc = carry[h]
            hs = slice(h * HP, (h + 1) * HP)
            s = _dot_nt(k_ref[pl.ds(off, tq), hs], q_ref[:, hs])
            if diag:
                kpos = lax.broadcasted_iota(jnp.int32, (tq, tq), 0)
                qpos = lax.broadcasted_iota(jnp.int32, (tq, tq), 1)
                s = jnp.where(kpos <= qpos, s, NEG)
            m_new = jnp.maximum(m, jnp.max(s, 0, keepdims=True))
            alpha = jnp.exp(m - m_new)
            p = jnp.exp(s - m_new)
            l = alpha * l + jnp.sum(p, 0, keepdims=True)
            vt = vt_ref[h * MLA_V:(h + 1) * MLA_V, pl.ds(off, tq)]
            acc = alpha * acc + _dot(vt, p.astype(MXU_DT))
            out.append((m_new, l, acc))
        return tuple(out)

    one = (jnp.full((1, tq), NEG, F32), jnp.zeros((1, tq), F32), jnp.zeros((MLA_V, tq), F32))
    carry = lax.fori_loop(0, qi, lambda kt, c: step(kt, c, False), (one,) * _MLA_PAIR)
    carry = step(qi, carry, True)
    for h in range(_MLA_PAIR):
        _, l, acc = carry[h]
        o_ref[h * MLA_V:(h + 1) * MLA_V, :] = acc / l


def _mla_attn(q, k, vt, batch, seq, tq):
    nq = seq // tq
    pw = _MLA_PAIR * HP
    pv = _MLA_PAIR * MLA_V
    return pl.pallas_call(
        functools.partial(_mla_attn_body, tq=tq),
        grid=(batch, MLA_HEADS // _MLA_PAIR, nq),
        in_specs=[pl.BlockSpec((tq, pw), lambda b, h, i: (b * nq + i, h)),
                  pl.BlockSpec((seq, pw), lambda b, h, i: (b, h)),
                  pl.BlockSpec((None, pv, seq), lambda b, h, i: (b, h, 0))],
        out_specs=pl.BlockSpec((None, pv, tq), lambda b, h, i: (b, h, i)),
        out_shape=jax.ShapeDtypeStruct((batch, MLA_HEADS * MLA_V, seq), F32),
        compiler_params=_params(3),
        name="mla_attn",
    )(q, k, vt)


def _compress_body(c_ref, pe_ref, w1_ref, b1_ref, w2_ref, o_ref):
    c = c_ref[...]
    half = CMP_STRIDE * NSA_DH
    lo = (c + pe_ref[0:1, :]).astype(MXU_DT)
    hi = (c + pe_ref[1:2, :]).astype(MXU_DT)
    u = _dot(lo, w1_ref[0:half, :])
    w = _dot(hi, w1_ref[half:2 * half, :])
    w = pltpu.roll(w, w.shape[0] - 1, 0)
    hid = jax.nn.gelu(u + w + b1_ref[...])
    o_ref[...] = _dot(hid.astype(MXU_DT), w2_ref[...]).astype(o_ref.dtype)


def _compress(chunks, pe, w1, b1, w2):
    _, bg, nch, cw = chunks.shape
    return pl.pallas_call(
        _compress_body,
        grid=(2, bg),
        in_specs=[pl.BlockSpec((None, None, nch, cw), lambda s, j: (s, j, 0, 0)),
                  pl.BlockSpec((None, 2, cw), lambda s, j: (s, 0, 0)),
                  pl.BlockSpec((None, 2 * cw, CMP_HIDDEN), lambda s, j: (s, 0, 0)),
                  pl.BlockSpec((None, 1, CMP_HIDDEN), lambda s, j: (s, 0, 0)),
                  pl.BlockSpec((None, CMP_HIDDEN, NSA_DH), lambda s, j: (s, 0, 0))],
        out_specs=pl.BlockSpec((None, None, nch, NSA_DH), lambda s, j: (s, j, 0, 0)),
        out_shape=jax.ShapeDtypeStruct((2, bg, nch, NSA_DH), MXU_DT),
        compiler_params=_params(2),
        name="nsa_compress",
    )(chunks, pe, w1, b1, w2)


def _cmp_attn_body(slope_ref, q_ref, kc_ref, vct_ref, ovl_ref, pool_ref,
                   oc_ref, sel_ref, flag_ref, *, tq, n_sel):
    g = pl.program_id(1)
    q0 = pl.program_id(2) * tq
    nch = kc_ref.shape[0]
    ns = sel_ref.shape[1]
    kc = kc_ref[...]
    vct = vct_ref[...]
    n_i = lax.broadcasted_iota(jnp.int32, (nch, tq), 0)
    t_i = q0 + lax.broadcasted_iota(jnp.int32, (nch, tq), 1)
    dist = t_i - (n_i * CMP_STRIDE + (CMP_BLOCK - 1))
    valid = dist >= 0
    distf = dist.astype(F32)
    ps = []
    for r in range(NSA_R):
        s = _dot_nt(kc, q_ref[:, r * HP:r * HP + NSA_DH])
        s = jnp.where(valid, s - slope_ref[g * NSA_R + r] * distf, NEG)
        m = jnp.max(s, 0, keepdims=True)
        e = jnp.where(valid, jnp.exp(s - m), 0.0)
        d = jnp.sum(e, 0, keepdims=True)
        p = (e / jnp.maximum(d, 1e-30)).astype(MXU_DT)
        oc_ref[r] = _dot(vct, p)
        ps.append(p)
    imp = _dot(ovl_ref[...], jnp.concatenate(ps, axis=0))

    blk = lax.broadcasted_iota(jnp.int32, (ns, tq), 0)
    cur = (q0 + lax.broadcasted_iota(jnp.int32, (ns, tq), 1)) // SEL_BLOCK
    blkf = blk.astype(F32)
    validb = blk <= cur
    score = jnp.where(validb, imp, -1.0)
    score = jnp.where(blk == 0, FORCED_SCORE, score)
    score = jnp.where(blk == cur, FORCED_SCORE, score)
    score = jnp.where(blk == cur - 1, FORCED_SCORE, score)
    sel = jnp.zeros((ns, tq), F32)
    for _ in range(n_sel):
        mx = jnp.max(score, 0, keepdims=True)
        first = jnp.min(jnp.where(score == mx, blkf, float(ns)), 0, keepdims=True)
        hit = blkf == first
        sel = jnp.where(hit, 1.0, sel)
        score = jnp.where(hit, -3e38, score)
    sel = jnp.where(validb, sel, 0.0)
    sel_ref[...] = sel.T.astype(sel_ref.dtype)
    ones = jnp.ones((8, NSA_TQ), MXU_DT)
    selb = sel.astype(MXU_DT)
    for j in range(tq // NSA_TQ):
        cnt = _dot_nt(ones, selb[:, j * NSA_TQ:(j + 1) * NSA_TQ])
        flag_ref[j] = _dot(cnt.astype(MXU_DT), pool_ref[...])


def _cmp_attn(slopes, nq, kc, vct, ovl4, pool, batch, seq, tq):
    nq_t = seq // tq
    nch = kc.shape[-2]
    ns = seq // SEL_BLOCK
    n_sel = min(N_SEL, ns)
    sub = tq // NSA_TQ
    gw = NSA_R * HP
    return pl.pallas_call(
        functools.partial(_cmp_attn_body, tq=tq, n_sel=n_sel),
        grid=(batch, NSA_G, nq_t),
        in_specs=[pl.BlockSpec(memory_space=pltpu.SMEM),
                  pl.BlockSpec((tq, gw), lambda b, g, i: (b * nq_t + i, g)),
                  pl.BlockSpec((None, None, nch, NSA_DH), lambda b, g, i: (b, g, 0, 0)),
                  pl.BlockSpec((None, None, NSA_DH, nch), lambda b, g, i: (b, g, 0, 0)),
                  _const_spec(ovl4.shape), _const_spec(pool.shape)],
        out_specs=[pl.BlockSpec((None, None, NSA_R, NSA_DH, tq), lambda b, g, i: (b, g, 0, 0, i)),
                   pl.BlockSpec((None, None, tq, ns), lambda b, g, i: (b, g, i, 0)),
                   pl.BlockSpec((None, None, sub, 8, LANE), lambda b, g, i: (b, g, i, 0, 0))],
        out_shape=[jax.ShapeDtypeStruct((batch, NSA_G, NSA_R, NSA_DH, seq), F32),
                   jax.ShapeDtypeStruct((batch, NSA_G, seq, ns), MXU_DT),
                   jax.ShapeDtypeStruct((batch, NSA_G, seq // NSA_TQ, 8, LANE), F32)],
        compiler_params=_params(3),
        name="nsa_cmp_attn_topk",
    )(slopes, nq, kc, vct, ovl4, pool)


_BIG = 1e30


def _nsa_attn_body(flag_ref, slope_ref, q_ref, sel_ref, kaug_ref, vst_ref, kwaug_ref, vwt_ref,
                   oct_ref, gt_ref, o_ref, m_scr, l_scr, acc_scr, *, tq, tk, nkt):
    b = pl.program_id(0)
    g = pl.program_id(1)
    qi = pl.program_id(2)
    q0 = qi * tq
    n = NSA_R * tq
    seq = kwaug_ref.shape[0]
    wk = WINDOW + tq

    lane = lax.broadcasted_iota(jnp.int32, (tq, HP), 1)
    selneg = ((sel_ref[...].astype(F32) - 1.0) * _BIG).astype(MXU_DT)
    q0f = q0.astype(F32)
    qc, qa = [], []
    for r in range(NSA_R):
        slope = slope_ref[g * NSA_R + r]
        c = jnp.where(lane == NSA_DH, slope * POS_SPLIT,
                      jnp.where(lane == NSA_DH + 1, slope,
                                jnp.where(lane == NSA_DH + 2, -slope * q0f, 0.0)))
        qr = q_ref[:, r * HP:(r + 1) * HP] + c.astype(MXU_DT)
        qc.append(qr)
        qa.append(jnp.concatenate([selneg, qr], axis=1))
    qc = jnp.concatenate(qc, axis=0)
    qa = jnp.concatenate(qa, axis=0)

    t_lane = q0 + (lax.broadcasted_iota(jnp.int32, (1, n), 1) & (tq - 1))

    m_scr[...] = jnp.full(m_scr.shape, NEG, F32)
    l_scr[...] = jnp.zeros(l_scr.shape, F32)
    acc_scr[...] = jnp.zeros(acc_scr.shape, F32)

    def tile(kt, diag):
        off = pl.multiple_of(kt * tk, tk)
        s = _dot_nt(kaug_ref[pl.ds(off, tk), :], qa)
        if diag:
            kpos = off + lax.broadcasted_iota(jnp.int32, (tk, n), 0)
            s = jnp.where(kpos <= t_lane, s, NEG)
        m_old = m_scr[...]
        m_new = jnp.maximum(m_old, jnp.max(s, 0, keepdims=True))
        alpha = jnp.exp(m_old - m_new)
        p = jnp.exp(s - m_new)
        l_scr[...] = alpha * l_scr[...] + jnp.sum(p, 0, keepdims=True)
        acc_scr[...] = alpha * acc_scr[...] + _dot(vst_ref[:, pl.ds(off, tk)], p.astype(MXU_DT))
        m_scr[...] = m_new

    kd = q0 // tk
    fbase = ((b * NSA_G + g) * pl.num_programs(2) + qi) * nkt

    def body(kt, carry):
        @pl.when(flag_ref[fbase + kt] > 0)
        def _():
            tile(kt, False)
        return carry

    lax.fori_loop(0, kd, body, 0)
    tile(kd, True)
    o_sel = acc_scr[...] / l_scr[...]

    start = pl.multiple_of(jnp.clip(q0 - WINDOW, 0, seq - wk), LANE)
    s = _dot_nt(kwaug_ref[pl.ds(start, wk), :], qc)
    dist = t_lane - (start + lax.broadcasted_iota(jnp.int32, (wk, n), 0))
    s = jnp.where((dist >= 0) & (dist < WINDOW), s, NEG)
    e = jnp.exp(s - jnp.max(s, 0, keepdims=True))
    d = jnp.sum(e, 0, keepdims=True)
    o_win = _dot(vwt_ref[:, pl.ds(start, wk)], e.astype(MXU_DT)) / d

    for r in range(NSA_R):
        ls = slice(r * tq, (r + 1) * tq)
        o_ref[r] = (gt_ref[0, r:r + 1, :] * oct_ref[r] + gt_ref[1, r:r + 1, :] * o_sel[:, ls]
                    + gt_ref[2, r:r + 1, :] * o_win[:, ls])


def _nsa_attn(flags, slopes, nq, sel, kaug, vst, kwaug, vwt, oct_, gt, batch, seq, tq, tk):
    nq_t = seq // tq
    ns = seq // SEL_BLOCK
    nkt = seq // tk
    n = NSA_R * tq
    gw = NSA_R * HP
    bg = lambda b, g, i, f: (b, g, 0, 0)
    grid_spec = pltpu.PrefetchScalarGridSpec(
        num_scalar_prefetch=1,
        grid=(batch, NSA_G, nq_t),
        in_specs=[pl.BlockSpec(memory_space=pltpu.SMEM),
                  pl.BlockSpec((tq, gw), lambda b, g, i, f: (b * nq_t + i, g)),
                  pl.BlockSpec((None, None, tq, ns), lambda b, g, i, f: (b, g, i, 0)),
                  pl.BlockSpec((None, None, seq, ns + HP), bg),
                  pl.BlockSpec((None, None, NSA_DH, seq), bg),
                  pl.BlockSpec((None, None, seq, HP), bg),
                  pl.BlockSpec((None, None, NSA_DH, seq), bg),
                  pl.BlockSpec((None, None, NSA_R, NSA_DH, tq), lambda b, g, i, f: (b, g, 0, 0, i)),
                  pl.BlockSpec((None, None, 3, NSA_R, tq), lambda b, g, i, f: (b, g, 0, 0, i))],
        out_specs=pl.BlockSpec((None, None, NSA_R, NSA_DH, tq), lambda b, g, i, f: (b, g, 0, 0, i)),
        scratch_shapes=[pltpu.VMEM((1, n), F32), pltpu.VMEM((1, n), F32),
                        pltpu.VMEM((NSA_DH, n), F32)])
    return pl.pallas_call(
        functools.partial(_nsa_attn_body, tq=tq, tk=tk, nkt=nkt),
        grid_spec=grid_spec,
        out_shape=jax.ShapeDtypeStruct((batch, NSA_G, NSA_R, NSA_DH, seq), F32),
        compiler_params=_params(3),
        name="nsa_sel_win_attn",
    )(flags, slopes, nq, sel, kaug, vst, kwaug, vwt, oct_, gt)


def _outproj_body(x_ref, oa_ref, ob_ref, wo_ref, g_ref, b_ref, y_ref):
    wa = MLA_HEADS * MLA_V
    mix = _dot(oa_ref[...].astype(MXU_DT), wo_ref[0:wa, :])
    mix = mix + _dot(ob_ref[...].astype(MXU_DT), wo_ref[wa:, :])
    y_ref[...] = _ln_rows(ALPHA * x_ref[...] + mix, g_ref[...], b_ref[...])


def _outproj(x2d, oa, ob, wo, g, b, tm):
    t = x2d.shape[0]
    row = lambda i: (i, 0)
    return pl.pallas_call(
        _outproj_body,
        grid=(t // tm,),
        in_specs=[pl.BlockSpec((tm, D_MODEL), row),
                  pl.BlockSpec((tm, oa.shape[1]), row), pl.BlockSpec((tm, ob.shape[1]), row),
                  _const_spec(wo.shape),
                  _const_spec((1, D_MODEL)), _const_spec((1, D_MODEL))],
        out_specs=pl.BlockSpec((tm, D_MODEL), row),
        out_shape=jax.ShapeDtypeStruct((t, D_MODEL), F32),
        compiler_params=_params(1),
        name="out_proj_ln",
    )(x2d, oa, ob, wo, g.reshape(1, -1), b.reshape(1, -1))


def _xattn_body(x_ref, wq_ref, kv_ref, wo_ref, g_ref, b_ref, y_ref):
    x = x_ref[...]
    q = _dot(x.astype(MXU_DT), wq_ref[...]) * (XA_DH ** -0.5)
    hw = XA_HEADS * XA_DH
    y = jnp.zeros_like(x)
    for h in range(XA_HEADS):
        sl = slice(h * XA_DH, (h + 1) * XA_DH)
        s = _dot_nt(q[:, sl].astype(MXU_DT), kv_ref[:, sl])
        e = jnp.exp(s - jnp.max(s, -1, keepdims=True))
        p = e / jnp.sum(e, -1, keepdims=True)
        o = _dot(p.astype(MXU_DT), kv_ref[:, hw + h * XA_DH:hw + (h + 1) * XA_DH])
        y = y + _dot(o.astype(MXU_DT), wo_ref[sl, :])
    y_ref[...] = _ln_rows(ALPHA * x + y, g_ref[...], b_ref[...])


def _xattn(x2d, wq, memkv, wo, g, b, batch, seq, tm):
    ns = seq // tm
    mem = memkv.shape[0] // batch
    row = lambda bb, i: (bb * ns + i, 0)
    return pl.pallas_call(
        _xattn_body,
        grid=(batch, ns),
        in_specs=[pl.BlockSpec((tm, D_MODEL), row),
                  _const_spec(wq.shape),
                  pl.BlockSpec((mem, memkv.shape[1]), lambda bb, i: (bb, 0)),
                  _const_spec(wo.shape),
                  _const_spec((1, D_MODEL)), _const_spec((1, D_MODEL))],
        out_specs=pl.BlockSpec((tm, D_MODEL), row),
        out_shape=jax.ShapeDtypeStruct((batch * seq, D_MODEL), F32),
        compiler_params=_params(2),
        name="mem_xattn_ln",
    )(x2d, wq, memkv, wo, g.reshape(1, -1), b.reshape(1, -1))


_FF_CHUNK = 256
_HALO = 8


def _ffn_body(x_ref, xp_ref, wup_ref, cw_ref, cb_ref, wdn_ref, g_ref, b_ref, y_ref):
    first = pl.program_id(1) == 0
    x = x_ref[...]
    xp = jnp.where(first, 0.0, xp_ref[...])
    xe = jnp.concatenate([xp, x], axis=0).astype(MXU_DT)

    def conv(c0):
        u = _dot(xe, wup_ref[:, c0:c0 + _FF_CHUNK])
        u1 = pltpu.roll(u, 1, 0)
        u2 = pltpu.roll(u, 2, 0)
        cw = cw_ref[:, c0:c0 + _FF_CHUNK]
        r = u * cw[2:3, :] + u1 * cw[1:2, :] + u2 * cw[0:1, :] + cb_ref[:, c0:c0 + _FF_CHUNK]
        return r[_HALO:, :]

    acc = jnp.zeros_like(x)
    for c in range(D_FF // _FF_CHUNK):
        a = conv(c * _FF_CHUNK)
        gg = conv(D_FF + c * _FF_CHUNK)
        act = (a * jax.nn.silu(gg)).astype(MXU_DT)
        acc = acc + _dot(act, wdn_ref[c * _FF_CHUNK:(c + 1) * _FF_CHUNK, :])
    y_ref[...] = _ln_rows(ALPHA * x + acc, g_ref[...], b_ref[...])


def _ffn(x2d, wup, cw, cb, wdn, g, b, batch, seq, tm):
    ns = seq // tm
    hb = tm // _HALO
    row = lambda bb, i: (bb * ns + i, 0)
    prev = lambda bb, i: (jnp.maximum((bb * ns + i) * hb - 1, 0), 0)
    return pl.pallas_call(
        _ffn_body,
        grid=(batch, ns),
        in_specs=[pl.BlockSpec((tm, D_MODEL), row),
                  pl.BlockSpec((_HALO, D_MODEL), prev),
                  _const_spec(wup.shape), _const_spec(cw.shape), _const_spec(cb.shape),
                  _const_spec(wdn.shape),
                  _const_spec((1, D_MODEL)), _const_spec((1, D_MODEL))],
        out_specs=pl.BlockSpec((tm, D_MODEL), row),
        out_shape=jax.ShapeDtypeStruct((batch * seq, D_MODEL), F32),
        compiler_params=_params(2),
        name="conv_ffn_ln",
    )(x2d, x2d, wup, cw, cb, wdn, g.reshape(1, -1), b.reshape(1, -1))


def _rot_half_cols(w):
    half = MLA_ROPE // 2
    return jnp.concatenate([-w[..., half:], w[..., :half]], axis=-1)


def _prep_layer(w_in, cq_g, w_uq, ckv_g, w_ukv):
    d = w_in.shape[0]
    z = lambda n: jnp.zeros((d, n), F32)
    kr = w_in[:, _O_KR:_O_NQ]
    wnq = w_in[:, _O_NQ:_O_KV6].reshape(d, NSA_HEADS, NSA_DH) * (NSA_DH ** -0.5)
    wnq = jnp.concatenate([wnq, jnp.zeros((d, NSA_HEADS, HP - NSA_DH), F32)], axis=-1)
    w_ext = jnp.concatenate([
        w_in[:, _O_CQ:_O_CKV], w_in[:, _O_CKV:_O_KR],
        wnq.reshape(d, NSA_HEADS * HP),
        w_in[:, _O_KV6:_O_GL],
        z(MLA_NOPE), kr, z(HP - MLA_NOPE - MLA_ROPE),
        z(MLA_NOPE), _rot_half_cols(kr), z(HP - MLA_NOPE - MLA_ROPE),
        w_in[:, _O_GL:_IN_COLS], z(_E_END - _E_GL - (_IN_COLS - _O_GL))], axis=1)
    wq = w_uq.reshape(MLA_Q_RANK, MLA_HEADS, MLA_NOPE + MLA_ROPE)
    zq = jnp.zeros((MLA_Q_RANK, MLA_HEADS, HP - MLA_NOPE - MLA_ROPE), F32)
    wqa = jnp.concatenate([wq, zq], axis=-1).reshape(MLA_Q_RANK, MLA_HEADS * HP)
    wqb = jnp.concatenate([jnp.zeros_like(wq[..., :MLA_NOPE]), _rot_half_cols(wq[..., MLA_NOPE:]), zq],
                          axis=-1).reshape(MLA_Q_RANK, MLA_HEADS * HP)
    wkv = w_ukv.reshape(MLA_KV_RANK, MLA_HEADS, MLA_NOPE + MLA_V)
    zk = jnp.zeros((MLA_KV_RANK, MLA_HEADS, HP - MLA_NOPE), F32)
    wkk = jnp.concatenate([wkv[..., :MLA_NOPE], zk], axis=-1).reshape(MLA_KV_RANK, MLA_HEADS * HP)
    wvv = wkv[..., MLA_NOPE:].reshape(MLA_KV_RANK, MLA_HEADS * MLA_V)
    return dict(w_in=w_ext.astype(MXU_DT), cq_g=cq_g.reshape(1, -1), ckv_g=ckv_g.reshape(1, -1),
                wqa=wqa.astype(MXU_DT), wqb=wqb.astype(MXU_DT),
                wkk=wkk.astype(MXU_DT), wkv=wvv.astype(MXU_DT))


def _rope_tabs(seq):
    inv = ROPE_BASE ** (-jnp.arange(0, MLA_ROPE, 2, dtype=F32) / MLA_ROPE)
    ang = jnp.arange(seq, dtype=F32)[:, None] * inv[None, :]
    cos, sin = jnp.cos(ang), jnp.sin(ang)
    pad = jnp.zeros((seq, HP - MLA_NOPE - MLA_ROPE), F32)
    tc = jnp.concatenate([jnp.ones((seq, MLA_NOPE), F32), cos, cos, pad], axis=1)
    ts = jnp.concatenate([jnp.zeros((seq, MLA_NOPE), F32), sin, sin, pad], axis=1)
    scale = (MLA_NOPE + MLA_ROPE) ** -0.5
    return dict(cq=tc * scale, sq=ts * scale, ck=tc, sk=ts)


def _nsa_consts(seq):
    nch = seq // CMP_STRIDE
    ns = seq // SEL_BLOCK
    c_start = jnp.arange(nch) * CMP_STRIDE
    s_start = jnp.arange(ns) * SEL_BLOCK
    ovl = jnp.clip(jnp.minimum(c_start[None, :] + CMP_BLOCK, s_start[:, None] + SEL_BLOCK)
                   - jnp.maximum(c_start[None, :], s_start[:, None]), 0).astype(F32) / CMP_BLOCK
    ovl4 = jnp.tile(ovl, (1, NSA_R)).astype(MXU_DT)
    pos = jnp.arange(seq)
    onehot = (pos[:, None] // SEL_BLOCK == jnp.arange(ns)[None, :]).astype(MXU_DT)
    posc = jnp.zeros((seq, HP - NSA_DH), F32)
    posc = posc.at[:, 0].set(pos // POS_SPLIT).at[:, 1].set(pos % POS_SPLIT).at[:, 2].set(1.0)
    pool = (jnp.arange(ns)[:, None] // (NSA_TK // SEL_BLOCK) == jnp.arange(LANE)[None, :])
    slopes = 2.0 ** (-8.0 * jnp.arange(1, NSA_HEADS + 1, dtype=F32) / NSA_HEADS)
    return dict(ovl4=ovl4, onehot=onehot, posc=posc.astype(MXU_DT), pool=pool.astype(MXU_DT),
                slopes=slopes)


def _heads_first(a, batch, seq, lead):
    a = a.reshape(batch, seq, lead, NSA_G, NSA_DH)
    return a.transpose(2, 0, 3, 1, 4)


def kernel(x, mem, ln_in_g, ln_in_b, ln_mem_g, ln_mem_b, w_in, mla_cq_g, mla_w_uq, mla_ckv_g, mla_w_ukv, nsa_pe_k, nsa_w1_k, nsa_b1_k, nsa_w2_k, nsa_pe_v, nsa_w1_v, nsa_b1_v, nsa_w2_v, w_out, ln1_g, ln1_b, xa_wq, xa_wkv, xa_wo, ln2_g, ln2_b, ffn_w_up, ffn_conv_w, ffn_conv_b, ffn_w_down, ln3_g, ln3_b):
    batch, seq, d = x.shape
    t = batch * seq
    depth = w_in.shape[0]
    tm = min(TM, seq)
    tabs = _rope_tabs(seq)
    consts = _nsa_consts(seq)
    nch = seq // CMP_STRIDE
    ns = seq // SEL_BLOCK
    nkt = seq // NSA_TK
    bcast = lambda a: jnp.broadcast_to(a, (batch, NSA_G) + a.shape)

    xs = _layer_norm(x.reshape(t, d), ln_in_g, ln_in_b, tm)
    memn = _layer_norm(mem.reshape(-1, d), ln_mem_g, ln_mem_b, mem.shape[0] * mem.shape[1])

    for l in range(depth):
        lw = _prep_layer(w_in[l], mla_cq_g[l], mla_w_uq[l], mla_ckv_g[l], mla_w_ukv[l])
        q, k, v, nq, kcvc, kvsw, gates = _inproj(xs, lw, tabs, seq, tm)

        vt = v.reshape(batch, seq, MLA_HEADS * MLA_V).transpose(0, 2, 1)
        oa = _mla_attn(q, k, vt, batch, seq, min(MLA_T, seq))
        oa = oa.transpose(0, 2, 1).reshape(t, MLA_HEADS * MLA_V)

        chunks = _heads_first(kcvc, batch, seq, 2).reshape(2, batch * NSA_G, nch, CMP_STRIDE * NSA_DH)
        pe = jnp.stack([nsa_pe_k[l], nsa_pe_v[l]]).reshape(2, 2, CMP_STRIDE * NSA_DH)
        w1 = jnp.stack([nsa_w1_k[l], nsa_w1_v[l]]).astype(MXU_DT)
        b1 = jnp.stack([nsa_b1_k[l], nsa_b1_v[l]]).reshape(2, 1, CMP_HIDDEN)
        w2 = jnp.stack([nsa_w2_k[l], nsa_w2_v[l]]).astype(MXU_DT)
        cmp = _compress(chunks, pe, w1, b1, w2).reshape(2, batch, NSA_G, nch, NSA_DH)
        oc_t, sel, cnt = _cmp_attn(consts["slopes"], nq, cmp[0], cmp[1].transpose(0, 1, 3, 2),
                                   consts["ovl4"], consts["pool"], batch, seq, min(CMP_TQ, seq))
        flags = (cnt[:, :, :, 0, :nkt] > 0).astype(jnp.int32).reshape(-1)
        kv4 = _heads_first(kvsw, batch, seq, 4)
        kaug = jnp.concatenate([bcast(consts["onehot"]), kv4[0], bcast(consts["posc"])], axis=-1)
        kwaug = jnp.concatenate([kv4[2], bcast(consts["posc"])], axis=-1)
        gt = gates[:, :3 * NSA_HEADS].reshape(batch, seq, NSA_G, NSA_R, 3).transpose(0, 2, 4, 3, 1)
        ob = _nsa_attn(flags, consts["slopes"], nq, sel, kaug, kv4[1].transpose(0, 1, 3, 2),
                       kwaug, kv4[3].transpose(0, 1, 3, 2), oc_t, gt,
                       batch, seq, min(NSA_TQ, seq), min(NSA_TK, seq))
        ob = ob.transpose(0, 4, 1, 2, 3).reshape(t, NSA_HEADS * NSA_DH)

        xs = _outproj(xs, oa, ob, w_out[l].astype(MXU_DT), ln1_g[l], ln1_b[l], tm)

        memkv = _matmul(memn, xa_wkv[l].astype(MXU_DT), 512, MXU_DT)
        xs = _xattn(xs, xa_wq[l].astype(MXU_DT), memkv, xa_wo[l].astype(MXU_DT),
                    ln2_g[l], ln2_b[l], batch, seq, tm)

        xs = _ffn(xs, ffn_w_up[l].astype(MXU_DT), ffn_conv_w[l], ffn_conv_b[l].reshape(1, -1),
                  ffn_w_down[l].astype(MXU_DT), ln3_g[l], ln3_b[l], batch, seq, tm)

    return xs.reshape(batch, seq, d)
```

```python
import functools

import jax
import jax.numpy as jnp
from jax import lax
from jax.experimental import pallas as pl
from jax.experimental.pallas import tpu as pltpu

F32 = jnp.float32
MXU_DT = jnp.bfloat16

D_MODEL = 1024
MLA_HEADS = 8
MLA_NOPE = 64
MLA_ROPE = 32
MLA_V = 64
MLA_Q_RANK = 384
MLA_KV_RANK = 256
ROPE_BASE = 10000.0
NSA_HEADS = 8
NSA_G = 2
NSA_R = 4
NSA_DH = 64
CMP_STRIDE = 16
CMP_BLOCK = 32
CMP_HIDDEN = 128
SEL_BLOCK = 64
N_SEL = 16
WINDOW = 512
XA_HEADS = 4
XA_DH = 256
D_FF = 2816
DEPTH = 2
ALPHA = (2.0 * DEPTH) ** 0.25
LN_EPS = 1e-5
RMS_EPS = 1e-6
NEG = -1e30

LANE = 128
HP = 128
VMEM_LIMIT = 56 * 1024 * 1024

TM = 512
MLA_T = 512
MLA_HPS = 4
CMP_TQ = 512
CMP_CH = 128
NSA_TQ = 256
NSA_TK = 256
NSA_U = 2
POS_SPLIT = 128
VROWS = 80
N_FORCED = 3
LOG2E = 1.4426950408889634

_O_CQ, _O_CKV, _O_KR, _O_NQ = 0, 384, 640, 672
_O_KV6, _O_GL, _IN_COLS = 1184, 1952, 1976
_E_CQ, _E_CKV, _E_NQ, _E_KCVC, _E_KVSW, _E_KRA, _E_KRB, _E_GL, _E_END = (
    0, 384, 640, 1664, 1920, 2432, 2560, 2688, 2816)

_NT = (((1,), (1,)), ((), ()))


def _params(n_axes=1):
    return pltpu.CompilerParams(
        dimension_semantics=("arbitrary",) * n_axes,
        vmem_limit_bytes=VMEM_LIMIT)


def _const_spec(shape):
    nd = len(shape)
    return pl.BlockSpec(shape, lambda *_: (0,) * nd)


def _ln_rows(v, g, b):
    mu = jnp.mean(v, -1, keepdims=True)
    d = v - mu
    var = jnp.mean(d * d, -1, keepdims=True)
    return d * lax.rsqrt(var + LN_EPS) * g + b


def _dot(a, b):
    return jnp.dot(a, b, preferred_element_type=F32)


def _dot_nt(a, b):
    return lax.dot_general(a, b, _NT, preferred_element_type=F32)


def _ln_body(x_ref, g_ref, b_ref, o_ref):
    o_ref[...] = _ln_rows(x_ref[...], g_ref[...], b_ref[...])


def _layer_norm(x2d, g, b, tm):
    t, d = x2d.shape
    return pl.pallas_call(
        _ln_body,
        grid=(t // tm,),
        in_specs=[pl.BlockSpec((tm, d), lambda i: (i, 0)),
                  _const_spec((1, d)), _const_spec((1, d))],
        out_specs=pl.BlockSpec((tm, d), lambda i: (i, 0)),
        out_shape=jax.ShapeDtypeStruct((t, d), F32),
        compiler_params=_params(1),
        name="ln_rows",
    )(x2d, g.reshape(1, d), b.reshape(1, d))


def _matmul_body(a_ref, w_ref, o_ref):
    o_ref[...] = _dot(a_ref[...].astype(MXU_DT), w_ref[...]).astype(o_ref.dtype)


def _matmul(a, w, tn, out_dtype):
    m, k = a.shape
    n = w.shape[1]
    return pl.pallas_call(
        _matmul_body,
        grid=(n // tn,),
        in_specs=[_const_spec((m, k)), pl.BlockSpec((k, tn), lambda j: (0, j))],
        out_specs=pl.BlockSpec((m, tn), lambda j: (0, j)),
        out_shape=jax.ShapeDtypeStruct((m, n), out_dtype),
        compiler_params=_params(1),
        name="mem_kv_proj",
    )(a, w)


def _rms_rows(v, g):
    return v * lax.rsqrt(jnp.mean(v * v, -1, keepdims=True) + RMS_EPS) * g


def _inproj_body(x_ref, win_ref, cqg_ref, ckvg_ref, wqa_ref, wqb_ref, wkk_ref, wkv_ref,
                 tcq_ref, tsq_ref, tck_ref, tsk_ref,
                 q_ref, k_ref, vt_ref, nq_ref, kcvc_ref, kvsw_ref, gate_ref):
    xb = x_ref[...].astype(MXU_DT)
    h = _dot(xb, win_ref[...])
    cqn = _rms_rows(h[:, _E_CQ:_E_CKV], cqg_ref[...]).astype(MXU_DT)
    qa = _dot(cqn, wqa_ref[...])
    qb = _dot(cqn, wqb_ref[...])
    ckvn = _rms_rows(h[:, _E_CKV:_E_NQ], ckvg_ref[...]).astype(MXU_DT)
    kn = _dot(ckvn, wkk_ref[...])
    vt = _dot(ckvn, wkv_ref[...]).T.astype(vt_ref.dtype)
    tail = jnp.where(lax.broadcasted_iota(jnp.int32, (VROWS - MLA_V, vt.shape[1]), 0) == 0, 1.0, 0.0)
    for hh in range(MLA_HEADS):
        vt_ref[hh * VROWS:hh * VROWS + MLA_V, :] = vt[hh * MLA_V:(hh + 1) * MLA_V, :]
        vt_ref[hh * VROWS + MLA_V:(hh + 1) * VROWS, :] = tail.astype(vt_ref.dtype)
    krr = h[:, _E_KRA:_E_KRB] * tck_ref[...] + h[:, _E_KRB:_E_GL] * tsk_ref[...]
    tcq = tcq_ref[...]
    tsq = tsq_ref[...]
    for hh in range(MLA_HEADS):
        sl = slice(hh * HP, (hh + 1) * HP)
        q_ref[:, sl] = (qa[:, sl] * tcq + qb[:, sl] * tsq).astype(q_ref.dtype)
        k_ref[:, sl] = (kn[:, sl] + krr).astype(k_ref.dtype)
    nq_ref[...] = h[:, _E_NQ:_E_KCVC].astype(nq_ref.dtype)
    kcvc_ref[...] = h[:, _E_KCVC:_E_KVSW]
    kvsw_ref[...] = h[:, _E_KVSW:_E_KRA].astype(kvsw_ref.dtype)
    gate_ref[...] = jax.nn.sigmoid(h[:, _E_GL:_E_END])


def _inproj(x2d, lw, tabs, seq, tm):
    t = x2d.shape[0]
    npos = seq // tm
    row = lambda i: (i, 0)
    pos = lambda i: (i % npos, 0)
    hw = MLA_HEADS * HP
    vrows = MLA_HEADS * VROWS
    out_shape = [
        jax.ShapeDtypeStruct((t, hw), MXU_DT),
        jax.ShapeDtypeStruct((t, hw), MXU_DT),
        jax.ShapeDtypeStruct((t // seq, vrows, seq), MXU_DT),
        jax.ShapeDtypeStruct((t, NSA_HEADS * HP), MXU_DT),
        jax.ShapeDtypeStruct((t, 256), F32),
        jax.ShapeDtypeStruct((t, 512), MXU_DT),
        jax.ShapeDtypeStruct((t, LANE), F32),
    ]
    out_specs = [pl.BlockSpec((tm, s.shape[1]), row) for s in out_shape]
    out_specs[2] = pl.BlockSpec((None, vrows, tm), lambda i: (i // npos, 0, i % npos))
    return pl.pallas_call(
        _inproj_body,
        grid=(t // tm,),
        in_specs=[pl.BlockSpec((tm, D_MODEL), row),
                  _const_spec((D_MODEL, _E_END)),
                  _const_spec((1, MLA_Q_RANK)), _const_spec((1, MLA_KV_RANK)),
                  _const_spec((MLA_Q_RANK, hw)), _const_spec((MLA_Q_RANK, hw)),
                  _const_spec((MLA_KV_RANK, hw)), _const_spec((MLA_KV_RANK, MLA_HEADS * MLA_V)),
                  pl.BlockSpec((tm, HP), pos), pl.BlockSpec((tm, HP), pos),
                  pl.BlockSpec((tm, HP), pos), pl.BlockSpec((tm, HP), pos)],
        out_specs=out_specs,
        out_shape=out_shape,
        compiler_params=_params(1),
        name="in_proj",
    )(x2d, lw["w_in"], lw["cq_g"], lw["ckv_g"], lw["wqa"], lw["wqb"], lw["wkk"], lw["wkv"],
      tabs["cq"], tabs["sq"], tabs["ck"], tabs["sk"])


def _mla_attn_body(q_ref, k_ref, vt_ref, o_ref, *, tq):
    qi = pl.program_id(2)

    def step(kt, carry, diag):
        off = pl.multiple_of(kt * tq, tq)
        ss = []
        for h in range(MLA_HPS):
            hs = slice(h * HP, (h + 1) * HP)
            s = _dot_nt(k_ref[pl.ds(off, tq), hs], q_ref[:, hs])
            if diag:
                kpos = lax.broadcasted_iota(jnp.int32, (tq, tq), 0)
                qpos = lax.broadcasted_iota(jnp.int32, (tq, tq), 1)
                s = jnp.where(kpos <= qpos, s, NEG)
            ss.append(s)
        ms, alphas, ps = [], [], []
        for h in range(MLA_HPS):
            m = carry[h][0]
            m_new = jnp.maximum(m, jnp.max(ss[h], 0, keepdims=True))
            alphas.append(jnp.exp2(m - m_new))
            ps.append(jnp.exp2(ss[h] - m_new).astype(MXU_DT))
            ms.append(m_new)
        out = []
        for h in range(MLA_HPS):
            vt = vt_ref[h * VROWS:(h + 1) * VROWS, pl.ds(off, tq)]
            out.append((ms[h], alphas[h] * carry[h][1] + _dot(vt, ps[h])))
        return tuple(out)

    one = (jnp.full((1, tq), NEG, F32), jnp.zeros((VROWS, tq), F32))
    carry = lax.fori_loop(0, qi, lambda kt, c: step(kt, c, False), (one,) * MLA_HPS)
    carry = step(qi, carry, True)
    for h in range(MLA_HPS):
        acc = carry[h][1]
        o_ref[h * MLA_V:(h + 1) * MLA_V, :] = acc[0:MLA_V, :] / acc[MLA_V:MLA_V + 1, :]


def _mla_attn(q, k, vt, batch, seq, tq):
    nq = seq // tq
    pw = MLA_HPS * HP
    return pl.pallas_call(
        functools.partial(_mla_attn_body, tq=tq),
        grid=(batch, MLA_HEADS // MLA_HPS, nq),
        in_specs=[pl.BlockSpec((tq, pw), lambda b, h, i: (b * nq + i, h)),
                  pl.BlockSpec((seq, pw), lambda b, h, i: (b, h)),
                  pl.BlockSpec((None, MLA_HPS * VROWS, seq), lambda b, h, i: (b, h, 0))],
        out_specs=pl.BlockSpec((None, MLA_HPS * MLA_V, tq), lambda b, h, i: (b, h, i)),
        out_shape=jax.ShapeDtypeStruct((batch, MLA_HEADS * MLA_V, seq), F32),
        compiler_params=_params(3),
        name="mla_attn",
    )(q, k, vt)


def _compress_body(c_ref, pe_ref, w1_ref, b1_ref, w2_ref, o_ref):
    c = c_ref[...]
    half = CMP_STRIDE * NSA_DH
    lo = (c + pe_ref[0:1, :]).astype(MXU_DT)
    hi = (c + pe_ref[1:2, :]).astype(MXU_DT)
    u = _dot(lo, w1_ref[0:half, :])
    w = _dot(hi, w1_ref[half:2 * half, :])
    w = pltpu.roll(w, w.shape[0] - 1, 0)
    hid = jax.nn.gelu(u + w + b1_ref[...])
    o_ref[...] = _dot(hid.astype(MXU_DT), w2_ref[...]).astype(o_ref.dtype)


def _compress(chunks, pe, w1, b1, w2):
    _, bg, nch, cw = chunks.shape
    return pl.pallas_call(
        _compress_body,
        grid=(2, bg),
        in_specs=[pl.BlockSpec((None, None, nch, cw), lambda s, j: (s, j, 0, 0)),
                  pl.BlockSpec((None, 2, cw), lambda s, j: (s, 0, 0)),
                  pl.BlockSpec((None, 2 * cw, CMP_HIDDEN), lambda s, j: (s, 0, 0)),
                  pl.BlockSpec((None, 1, CMP_HIDDEN), lambda s, j: (s, 0, 0)),
                  pl.BlockSpec((None, CMP_HIDDEN, NSA_DH), lambda s, j: (s, 0, 0))],
        out_specs=pl.BlockSpec((None, None, nch, NSA_DH), lambda s, j: (s, j, 0, 0)),
        out_shape=jax.ShapeDtypeStruct((2, bg, nch, NSA_DH), MXU_DT),
        compiler_params=_params(2),
        name="nsa_compress",
    )(chunks, pe, w1, b1, w2)


_BIG = 1e30


def _nsa_queries(q_ref, slope_ref, g, q0, tq, dead):
    lane = lax.broadcasted_iota(jnp.int32, (tq, HP), 1)
    q0f = q0.astype(F32)
    qs = []
    for r in range(NSA_R):
        slope = slope_ref[g * NSA_R + r]
        c = jnp.where(lane == NSA_DH, slope * POS_SPLIT,
                      jnp.where(lane == NSA_DH + 1, slope,
                                jnp.where(lane == NSA_DH + 2, -slope * q0f,
                                          jnp.where(lane == NSA_DH + 3, dead, 0.0))))
        qs.append(q_ref[:, r * HP:(r + 1) * HP] + c.astype(MXU_DT))
    return qs


def _cmp_attn_body(slope_ref, q_ref, kc_ref, vct_ref, ovl_ref, pool_ref,
                   oc_ref, sel_ref, flag_ref, s_scr, e_scr, *, tq, n_sel):
    g = pl.program_id(1)
    q0 = pl.program_id(2) * tq
    ns = sel_ref.shape[1]
    n = NSA_R * tq
    ch = min(CMP_CH, kc_ref.shape[0])
    qc = jnp.concatenate(_nsa_queries(q_ref, slope_ref, g, q0, tq, 0.0), axis=0)
    t_lane = q0 + (lax.broadcasted_iota(jnp.int32, (1, n), 1) & (tq - 1))
    cend0 = lax.broadcasted_iota(jnp.int32, (ch, n), 0) * CMP_STRIDE + (CMP_BLOCK - 1)
    nproc = jnp.maximum(q0 + tq - CMP_BLOCK, 0) // (ch * CMP_STRIDE) + 1

    def scores(c, m):
        off = pl.multiple_of(c * ch, ch)
        s = _dot_nt(kc_ref[pl.ds(off, ch), :], qc)
        s = jnp.where(cend0 <= t_lane - off * CMP_STRIDE, s, NEG)
        s_scr[pl.ds(off, ch), :] = s
        return jnp.maximum(m, jnp.max(s, 0, keepdims=True))

    m = lax.fori_loop(0, nproc, scores, jnp.full((1, n), NEG, F32))

    def weights(c, carry):
        off = pl.multiple_of(c * ch, ch)
        e_scr[pl.ds(off, ch), :] = jnp.exp(s_scr[pl.ds(off, ch), :] - m).astype(MXU_DT)
        return carry

    e_scr[...] = jnp.zeros(e_scr.shape, MXU_DT)
    lax.fori_loop(0, nproc, weights, 0)
    e = e_scr[...]
    acc = _dot(vct_ref[...], e)
    has_key = t_lane >= CMP_BLOCK - 1
    inv = jnp.where(has_key, 1.0 / acc[NSA_DH:NSA_DH + 1, :], 0.0)
    impn = _dot(ovl_ref[...], e) * inv
    imp = impn[:, 0:tq]
    for r in range(NSA_R):
        oc_ref[r] = acc[0:NSA_DH, r * tq:(r + 1) * tq] * inv[:, r * tq:(r + 1) * tq]
        if r:
            imp = imp + impn[:, r * tq:(r + 1) * tq]

    blk = lax.broadcasted_iota(jnp.int32, (ns, tq), 0)
    cur = (q0 + lax.broadcasted_iota(jnp.int32, (ns, tq), 1)) // SEL_BLOCK
    blkf = blk.astype(F32)
    forced = (blk == 0) | (blk == cur) | (blk == cur - 1)
    score = jnp.where((blk <= cur) & jnp.logical_not(forced), imp, -1.0)
    sel = jnp.where(forced, 1.0, 0.0)
    for _ in range(n_sel - N_FORCED):
        mx = jnp.max(score, 0, keepdims=True)
        first = jnp.min(jnp.where(score == mx, blkf, float(ns)), 0, keepdims=True)
        hit = (blkf == first) & (mx >= 0.0)
        sel = jnp.where(hit, 1.0, sel)
        score = jnp.where(hit, -1.0, score)
    sel_ref[...] = sel.T.astype(sel_ref.dtype)
    ones = jnp.ones((8, NSA_TQ), MXU_DT)
    selb = sel.astype(MXU_DT)
    for j in range(tq // NSA_TQ):
        cnt = _dot_nt(ones, selb[:, j * NSA_TQ:(j + 1) * NSA_TQ])
        flag_ref[j] = _dot(cnt.astype(MXU_DT), pool_ref[...])


def _cmp_attn(slopes, nq, kc, vct, ovl, pool, batch, seq, tq):
    nq_t = seq // tq
    nch = kc.shape[-2]
    ns = seq // SEL_BLOCK
    n_sel = min(N_SEL, ns)
    sub = tq // NSA_TQ
    gw = NSA_R * HP
    n = NSA_R * tq
    return pl.pallas_call(
        functools.partial(_cmp_attn_body, tq=tq, n_sel=n_sel),
        grid=(batch, NSA_G, nq_t),
        in_specs=[pl.BlockSpec(memory_space=pltpu.SMEM),
                  pl.BlockSpec((tq, gw), lambda b, g, i: (b * nq_t + i, g)),
                  pl.BlockSpec((None, None, nch, HP), lambda b, g, i: (b, g, 0, 0)),
                  pl.BlockSpec((None, None, VROWS, nch), lambda b, g, i: (b, g, 0, 0)),
                  _const_spec(ovl.shape), _const_spec(pool.shape)],
        out_specs=[pl.BlockSpec((None, None, NSA_R, NSA_DH, tq), lambda b, g, i: (b, g, 0, 0, i)),
                   pl.BlockSpec((None, None, tq, ns), lambda b, g, i: (b, g, i, 0)),
                   pl.BlockSpec((None, None, sub, 8, LANE), lambda b, g, i: (b, g, i, 0, 0))],
        out_shape=[jax.ShapeDtypeStruct((batch, NSA_G, NSA_R, NSA_DH, seq), F32),
                   jax.ShapeDtypeStruct((batch, NSA_G, seq, ns), MXU_DT),
                   jax.ShapeDtypeStruct((batch, NSA_G, seq // NSA_TQ, 8, LANE), F32)],
        scratch_shapes=[pltpu.VMEM((nch, n), F32), pltpu.VMEM((nch, n), MXU_DT)],
        compiler_params=_params(3),
        name="nsa_cmp_attn_topk",
    )(slopes, nq, kc, vct, ovl, pool)


def _nsa_attn_body(flag_ref, slope_ref, q_ref, sel_ref, kaug_ref, vst_ref, kwaug_ref, vwt_ref,
                   oct_ref, gt_ref, o_ref, list_scr, m_scr, acc_scr, ow_scr, *, tq, tk, nkt):
    b = pl.program_id(0)
    g = pl.program_id(1)
    qi = pl.program_id(2)
    q0 = qi * tq
    n = NSA_R * tq
    wk = WINDOW + tq

    selneg = ((sel_ref[...].astype(F32) - 1.0) * _BIG).astype(MXU_DT)
    qs = _nsa_queries(q_ref, slope_ref, g, q0, tq, -_BIG)
    qc = jnp.concatenate(qs, axis=0)
    qa = jnp.concatenate([jnp.concatenate([selneg, qr], axis=1) for qr in qs], axis=0)
    t_loc = lax.broadcasted_iota(jnp.int32, (1, n), 1) & (tq - 1)
    t_lane = q0 + t_loc

    kd = q0 // tk
    fbase = ((b * NSA_G + g) * pl.num_programs(2) + qi) * nkt

    def compact(kt, cnt):
        act = flag_ref[fbase + kt] > 0

        @pl.when(act)
        def _():
            list_scr[cnt] = kt
        return cnt + act.astype(jnp.int32)

    cnt = lax.fori_loop(0, kd, compact, 0)
    for j in range(NSA_U):
        list_scr[cnt + j] = nkt

    m_scr[...] = jnp.full(m_scr.shape, NEG, F32)
    acc_scr[...] = jnp.zeros(acc_scr.shape, F32)

    def tiles(idx):
        ss, offs = [], []
        for i in idx:
            off = pl.multiple_of(i * tk, tk)
            ss.append(_dot_nt(kaug_ref[pl.ds(off, tk), :], qa))
            offs.append(off)
        return ss, offs

    def update(ss, offs):
        m_old = m_scr[...]
        m_new = m_old
        for s in ss:
            m_new = jnp.maximum(m_new, jnp.max(s, 0, keepdims=True))
        acc = jnp.exp(m_old - m_new) * acc_scr[...]
        for s, off in zip(ss, offs):
            acc = acc + _dot(vst_ref[:, pl.ds(off, tk)], jnp.exp(s - m_new).astype(MXU_DT))
        acc_scr[...] = acc
        m_scr[...] = m_new

    def group(gi, carry):
        update(*tiles([list_scr[gi * NSA_U + j] for j in range(NSA_U)]))
        return carry

    nfull = cnt // NSA_U
    lax.fori_loop(0, nfull, group, 0)
    ss, offs = tiles([list_scr[nfull * NSA_U + j] for j in range(NSA_U - 1)] + [kd])
    kpos = offs[-1] + lax.broadcasted_iota(jnp.int32, (tk, n), 0)
    ss[-1] = jnp.where(kpos <= t_lane, ss[-1], NEG)
    update(ss, offs)
    acc = acc_scr[...]
    o_sel = acc[0:NSA_DH, :] / acc[NSA_DH:NSA_DH + 1, :]

    k_loc = lax.broadcasted_iota(jnp.int32, (tq, n), 0)

    def window(start, banded):
        s = _dot_nt(kwaug_ref[pl.ds(start, wk), :], qc)
        if banded:
            parts = [jnp.where(k_loc > t_loc, s[0:tq], NEG), s[tq:WINDOW],
                     jnp.where(k_loc <= t_loc, s[WINDOW:wk], NEG)]
        else:
            dist = t_lane - (start + lax.broadcasted_iota(jnp.int32, (wk, n), 0))
            parts = [jnp.where((dist >= 0) & (dist < WINDOW), s, NEG)]
        m = jnp.max(parts[0], 0, keepdims=True)
        for p in parts[1:]:
            m = jnp.maximum(m, jnp.max(p, 0, keepdims=True))
        e = jnp.concatenate([jnp.exp(p - m).astype(MXU_DT) for p in parts], axis=0)
        ow = _dot(vwt_ref[:, pl.ds(start, wk)], e)
        ow_scr[...] = ow[0:NSA_DH, :] / ow[NSA_DH:NSA_DH + 1, :]

    @pl.when(q0 >= WINDOW)
    def _():
        window(pl.multiple_of(q0 - WINDOW, LANE), True)

    @pl.when(q0 < WINDOW)
    def _():
        window(0, False)

    o_win = ow_scr[...]

    for r in range(NSA_R):
        ls = slice(r * tq, (r + 1) * tq)
        o_ref[r] = (gt_ref[0, r:r + 1, :] * oct_ref[r] + gt_ref[1, r:r + 1, :] * o_sel[:, ls]
                    + gt_ref[2, r:r + 1, :] * o_win[:, ls])


def _nsa_attn(flags, slopes, nq, sel, kaug, vst, kwaug, vwt, oct_, gt, batch, seq, tq, tk):
    nq_t = seq // tq
    ns = seq // SEL_BLOCK
    nkt = seq // tk
    n = NSA_R * tq
    gw = NSA_R * HP
    bg = lambda b, g, i, f: (b, g, 0, 0)
    grid_spec = pltpu.PrefetchScalarGridSpec(
        num_scalar_prefetch=1,
        grid=(batch, NSA_G, nq_t),
        in_specs=[pl.BlockSpec(memory_space=pltpu.SMEM),
                  pl.BlockSpec((tq, gw), lambda b, g, i, f: (b * nq_t + i, g)),
                  pl.BlockSpec((None, None, tq, ns), lambda b, g, i, f: (b, g, i, 0)),
                  pl.BlockSpec((None, None, seq + tk, ns + HP), bg),
                  pl.BlockSpec((None, None, VROWS, seq + tk), bg),
                  pl.BlockSpec((None, None, seq, HP), bg),
                  pl.BlockSpec((None, None, VROWS, seq), bg),
                  pl.BlockSpec((None, None, NSA_R, NSA_DH, tq), lambda b, g, i, f: (b, g, 0, 0, i)),
                  pl.BlockSpec((None, None, 3, NSA_R, tq), lambda b, g, i, f: (b, g, 0, 0, i))],
        out_specs=pl.BlockSpec((None, None, NSA_R, NSA_DH, tq), lambda b, g, i, f: (b, g, 0, 0, i)),
        scratch_shapes=[pltpu.SMEM((nkt + NSA_U,), jnp.int32),
                        pltpu.VMEM((1, n), F32), pltpu.VMEM((VROWS, n), F32),
                        pltpu.VMEM((NSA_DH, n), F32)])
    return pl.pallas_call(
        functools.partial(_nsa_attn_body, tq=tq, tk=tk, nkt=nkt),
        grid_spec=grid_spec,
        out_shape=jax.ShapeDtypeStruct((batch, NSA_G, NSA_R, NSA_DH, seq), F32),
        compiler_params=_params(3),
        name="nsa_sel_win_attn",
    )(flags, slopes, nq, sel, kaug, vst, kwaug, vwt, oct_, gt)


_TN = (((0,), (0,)), ((), ()))


def _outproj_body(x_ref, oa_ref, ob_ref, wo_ref, g_ref, b_ref, y_ref):
    wa = oa_ref.shape[0]
    mix = lax.dot_general(oa_ref[...].astype(MXU_DT), wo_ref[0:wa, :], _TN,
                          preferred_element_type=F32)
    mix = mix + lax.dot_general(ob_ref[...].astype(MXU_DT), wo_ref[wa:, :], _TN,
                                preferred_element_type=F32)
    y_ref[...] = _ln_rows(ALPHA * x_ref[...] + mix, g_ref[...], b_ref[...])


def _outproj(x2d, oat, obt, wo, g, b, tm):
    batch, _, seq = oat.shape
    ns = seq // tm
    row = lambda bb, i: (bb * ns + i, 0)
    col = lambda bb, i: (bb, 0, i)
    return pl.pallas_call(
        _outproj_body,
        grid=(batch, ns),
        in_specs=[pl.BlockSpec((tm, D_MODEL), row),
                  pl.BlockSpec((None, oat.shape[1], tm), col),
                  pl.BlockSpec((None, obt.shape[1], tm), col),
                  _const_spec(wo.shape),
                  _const_spec((1, D_MODEL)), _const_spec((1, D_MODEL))],
        out_specs=pl.BlockSpec((tm, D_MODEL), row),
        out_shape=jax.ShapeDtypeStruct((batch * seq, D_MODEL), F32),
        compiler_params=_params(2),
        name="out_proj_ln",
    )(x2d, oat, obt, wo, g.reshape(1, -1), b.reshape(1, -1))


def _xattn_body(x_ref, wq_ref, kv_ref, wo_ref, g_ref, b_ref, y_ref):
    x = x_ref[...]
    q = _dot(x.astype(MXU_DT), wq_ref[...]) * (XA_DH ** -0.5)
    hw = XA_HEADS * XA_DH
    y = jnp.zeros_like(x)
    for h in range(XA_HEADS):
        sl = slice(h * XA_DH, (h + 1) * XA_DH)
        s = _dot_nt(q[:, sl].astype(MXU_DT), kv_ref[:, sl])
        e = jnp.exp(s - jnp.max(s, -1, keepdims=True))
        p = e / jnp.sum(e, -1, keepdims=True)
        o = _dot(p.astype(MXU_DT), kv_ref[:, hw + h * XA_DH:hw + (h + 1) * XA_DH])
        y = y + _dot(o.astype(MXU_DT), wo_ref[sl, :])
    y_ref[...] = _ln_rows(ALPHA * x + y, g_ref[...], b_ref[...])


def _xattn(x2d, wq, memkv, wo, g, b, batch, seq, tm):
    ns = seq // tm
    mem = memkv.shape[0] // batch
    row = lambda bb, i: (bb * ns + i, 0)
    return pl.pallas_call(
        _xattn_body,
        grid=(batch, ns),
        in_specs=[pl.BlockSpec((tm, D_MODEL), row),
                  _const_spec(wq.shape),
                  pl.BlockSpec((mem, memkv.shape[1]), lambda bb, i: (bb, 0)),
                  _const_spec(wo.shape),
                  _const_spec((1, D_MODEL)), _const_spec((1, D_MODEL))],
        out_specs=pl.BlockSpec((tm, D_MODEL), row),
        out_shape=jax.ShapeDtypeStruct((batch * seq, D_MODEL), F32),
        compiler_params=_params(2),
        name="mem_xattn_ln",
    )(x2d, wq, memkv, wo, g.reshape(1, -1), b.reshape(1, -1))


_FF_CHUNK = 256
_HALO = 8


def _ffn_body(x_ref, xp_ref, wup_ref, cw_ref, cb_ref, wdn_ref, g_ref, b_ref, y_ref):
    first = pl.program_id(1) == 0
    x = x_ref[...]
    xp = jnp.where(first, 0.0, xp_ref[...])
    xe = jnp.concatenate([xp, x], axis=0).astype(MXU_DT)

    def conv(c0):
        u = _dot(xe, wup_ref[:, c0:c0 + _FF_CHUNK])
        u1 = pltpu.roll(u, 1, 0)
        u2 = pltpu.roll(u, 2, 0)
        cw = cw_ref[:, c0:c0 + _FF_CHUNK]
        r = u * cw[2:3, :] + u1 * cw[1:2, :] + u2 * cw[0:1, :] + cb_ref[:, c0:c0 + _FF_CHUNK]
        return r[_HALO:, :]

    acc = jnp.zeros_like(x)
    for c in range(D_FF // _FF_CHUNK):
        a = conv(c * _FF_CHUNK)
        gg = conv(D_FF + c * _FF_CHUNK)
        act = (a * jax.nn.silu(gg)).astype(MXU_DT)
        acc = acc + _dot(act, wdn_ref[c * _FF_CHUNK:(c + 1) * _FF_CHUNK, :])
    y_ref[...] = _ln_rows(ALPHA * x + acc, g_ref[...], b_ref[...])


def _ffn(x2d, wup, cw, cb, wdn, g, b, batch, seq, tm):
    ns = seq // tm
    hb = tm // _HALO
    row = lambda bb, i: (bb * ns + i, 0)
    prev = lambda bb, i: (jnp.maximum((bb * ns + i) * hb - 1, 0), 0)
    return pl.pallas_call(
        _ffn_body,
        grid=(batch, ns),
        in_specs=[pl.BlockSpec((tm, D_MODEL), row),
                  pl.BlockSpec((_HALO, D_MODEL), prev),
                  _const_spec(wup.shape), _const_spec(cw.shape), _const_spec(cb.shape),
                  _const_spec(wdn.shape),
                  _const_spec((1, D_MODEL)), _const_spec((1, D_MODEL))],
        out_specs=pl.BlockSpec((tm, D_MODEL), row),
        out_shape=jax.ShapeDtypeStruct((batch * seq, D_MODEL), F32),
        compiler_params=_params(2),
        name="conv_ffn_ln",
    )(x2d, x2d, wup, cw, cb, wdn, g.reshape(1, -1), b.reshape(1, -1))


def _rot_half_cols(w):
    half = MLA_ROPE // 2
    return jnp.concatenate([-w[..., half:], w[..., :half]], axis=-1)


def _prep_layer(w_in, cq_g, w_uq, ckv_g, w_ukv):
    d = w_in.shape[0]
    z = lambda n: jnp.zeros((d, n), F32)
    kr = w_in[:, _O_KR:_O_NQ]
    wnq = w_in[:, _O_NQ:_O_KV6].reshape(d, NSA_HEADS, NSA_DH) * (NSA_DH ** -0.5)
    wnq = jnp.concatenate([wnq, jnp.zeros((d, NSA_HEADS, HP - NSA_DH), F32)], axis=-1)
    w_ext = jnp.concatenate([
        w_in[:, _O_CQ:_O_CKV], w_in[:, _O_CKV:_O_KR],
        wnq.reshape(d, NSA_HEADS * HP),
        w_in[:, _O_KV6:_O_GL],
        z(MLA_NOPE), kr, z(HP - MLA_NOPE - MLA_ROPE),
        z(MLA_NOPE), _rot_half_cols(kr), z(HP - MLA_NOPE - MLA_ROPE),
        w_in[:, _O_GL:_IN_COLS], z(_E_END - _E_GL - (_IN_COLS - _O_GL))], axis=1)
    wq = w_uq.reshape(MLA_Q_RANK, MLA_HEADS, MLA_NOPE + MLA_ROPE)
    zq = jnp.zeros((MLA_Q_RANK, MLA_HEADS, HP - MLA_NOPE - MLA_ROPE), F32)
    wqa = jnp.concatenate([wq, zq], axis=-1).reshape(MLA_Q_RANK, MLA_HEADS * HP)
    wqb = jnp.concatenate([jnp.zeros_like(wq[..., :MLA_NOPE]), _rot_half_cols(wq[..., MLA_NOPE:]), zq],
                          axis=-1).reshape(MLA_Q_RANK, MLA_HEADS * HP)
    wkv = w_ukv.reshape(MLA_KV_RANK, MLA_HEADS, MLA_NOPE + MLA_V)
    zk = jnp.zeros((MLA_KV_RANK, MLA_HEADS, HP - MLA_NOPE), F32)
    wkk = jnp.concatenate([wkv[..., :MLA_NOPE], zk], axis=-1).reshape(MLA_KV_RANK, MLA_HEADS * HP)
    wvv = wkv[..., MLA_NOPE:].reshape(MLA_KV_RANK, MLA_HEADS * MLA_V)
    return dict(w_in=w_ext.astype(MXU_DT), cq_g=cq_g.reshape(1, -1), ckv_g=ckv_g.reshape(1, -1),
                wqa=wqa.astype(MXU_DT), wqb=wqb.astype(MXU_DT),
                wkk=wkk.astype(MXU_DT), wkv=wvv.astype(MXU_DT))


def _rope_tabs(seq):
    inv = ROPE_BASE ** (-jnp.arange(0, MLA_ROPE, 2, dtype=F32) / MLA_ROPE)
    ang = jnp.arange(seq, dtype=F32)[:, None] * inv[None, :]
    cos, sin = jnp.cos(ang), jnp.sin(ang)
    pad = jnp.zeros((seq, HP - MLA_NOPE - MLA_ROPE), F32)
    tc = jnp.concatenate([jnp.ones((seq, MLA_NOPE), F32), cos, cos, pad], axis=1)
    ts = jnp.concatenate([jnp.zeros((seq, MLA_NOPE), F32), sin, sin, pad], axis=1)
    scale = (MLA_NOPE + MLA_ROPE) ** -0.5
    return dict(cq=tc * (scale * LOG2E), sq=ts * (scale * LOG2E), ck=tc, sk=ts)


def _pos_cols(pos, dead):
    c = jnp.zeros((pos.shape[0], HP - NSA_DH), F32)
    c = c.at[:, 0].set(pos // POS_SPLIT).at[:, 1].set(pos % POS_SPLIT)
    return c.at[:, 2].set(1.0 - dead).at[:, 3].set(dead).astype(MXU_DT)


def _ones_row(n):
    return jnp.zeros((VROWS - NSA_DH, n), MXU_DT).at[0].set(1.0)


def _nsa_consts(seq):
    nch = seq // CMP_STRIDE
    ns = seq // SEL_BLOCK
    c_start = jnp.arange(nch) * CMP_STRIDE
    s_start = jnp.arange(ns) * SEL_BLOCK
    ovl = jnp.clip(jnp.minimum(c_start[None, :] + CMP_BLOCK, s_start[:, None] + SEL_BLOCK)
                   - jnp.maximum(c_start[None, :], s_start[:, None]), 0).astype(F32) / CMP_BLOCK
    pos = jnp.arange(seq)
    onehot = (pos[:, None] // SEL_BLOCK == jnp.arange(ns)[None, :]).astype(MXU_DT)
    pool = (jnp.arange(ns)[:, None] // (NSA_TK // SEL_BLOCK) == jnp.arange(LANE)[None, :])
    slopes = 2.0 ** (-8.0 * jnp.arange(1, NSA_HEADS + 1, dtype=F32) / NSA_HEADS)
    return dict(ovl=ovl.astype(MXU_DT),
                onehot=onehot, posc=_pos_cols(pos, 0.0), cendc=_pos_cols(c_start + CMP_BLOCK - 1, 0.0),
                pool=pool.astype(MXU_DT), slopes=slopes)


def _heads_first(a, batch, seq, lead):
    a = a.reshape(batch, seq, lead, NSA_G, NSA_DH)
    return a.transpose(2, 0, 3, 1, 4)


def kernel(x, mem, ln_in_g, ln_in_b, ln_mem_g, ln_mem_b, w_in, mla_cq_g, mla_w_uq, mla_ckv_g, mla_w_ukv, nsa_pe_k, nsa_w1_k, nsa_b1_k, nsa_w2_k, nsa_pe_v, nsa_w1_v, nsa_b1_v, nsa_w2_v, w_out, ln1_g, ln1_b, xa_wq, xa_wkv, xa_wo, ln2_g, ln2_b, ffn_w_up, ffn_conv_w, ffn_conv_b, ffn_w_down, ln3_g, ln3_b):
    batch, seq, d = x.shape
    t = batch * seq
    depth = w_in.shape[0]
    tm = min(TM, seq)
    nsa_tq, nsa_tk = min(NSA_TQ, seq), min(NSA_TK, seq)
    tabs = _rope_tabs(seq)
    consts = _nsa_consts(seq)
    nch = seq // CMP_STRIDE
    nkt = seq // nsa_tk
    bg = (batch, NSA_G)
    bcast = lambda a: jnp.broadcast_to(a, bg + a.shape)
    kdummy = jnp.concatenate([jnp.zeros((nsa_tk, seq // SEL_BLOCK + NSA_DH), MXU_DT),
                              _pos_cols(jnp.zeros((nsa_tk,), jnp.int32), 1.0)], axis=1)
    vdummy = jnp.zeros(bg + (VROWS, nsa_tk), MXU_DT)

    xs = _layer_norm(x.reshape(t, d), ln_in_g, ln_in_b, tm)
    memn = _layer_norm(mem.reshape(-1, d), ln_mem_g, ln_mem_b, mem.shape[0] * mem.shape[1])

    for l in range(depth):
        lw = _prep_layer(w_in[l], mla_cq_g[l], mla_w_uq[l], mla_ckv_g[l], mla_w_ukv[l])
        q, k, vt, nq, kcvc, kvsw, gates = _inproj(xs, lw, tabs, seq, tm)

        oa = _mla_attn(q, k, vt, batch, seq, min(MLA_T, seq))

        chunks = _heads_first(kcvc, batch, seq, 2).reshape(2, batch * NSA_G, nch, CMP_STRIDE * NSA_DH)
        pe = jnp.stack([nsa_pe_k[l], nsa_pe_v[l]]).reshape(2, 2, CMP_STRIDE * NSA_DH)
        w1 = jnp.stack([nsa_w1_k[l], nsa_w1_v[l]]).astype(MXU_DT)
        b1 = jnp.stack([nsa_b1_k[l], nsa_b1_v[l]]).reshape(2, 1, CMP_HIDDEN)
        w2 = jnp.stack([nsa_w2_k[l], nsa_w2_v[l]]).astype(MXU_DT)
        cmp = _compress(chunks, pe, w1, b1, w2).reshape((2,) + bg + (nch, NSA_DH))
        kcaug = jnp.concatenate([cmp[0], bcast(consts["cendc"])], axis=-1)
        vct = jnp.concatenate([cmp[1].transpose(0, 1, 3, 2), bcast(_ones_row(nch))], axis=2)
        oc_t, sel, cnt = _cmp_attn(consts["slopes"], nq, kcaug, vct, consts["ovl"], consts["pool"],
                                   batch, seq, min(CMP_TQ, seq))
        flags = (cnt[:, :, :, 0, :nkt] > 0).astype(jnp.int32).reshape(-1)
        kv4 = _heads_first(kvsw, batch, seq, 4)
        kaug = jnp.concatenate([bcast(consts["onehot"]), kv4[0], bcast(consts["posc"])], axis=-1)
        kaug = jnp.concatenate([kaug, bcast(kdummy)], axis=2)
        vst = jnp.concatenate([kv4[1].transpose(0, 1, 3, 2), bcast(_ones_row(seq))], axis=2)
        vst = jnp.concatenate([vst, vdummy], axis=3)
        kwaug = jnp.concatenate([kv4[2], bcast(consts["posc"])], axis=-1)
        vwt = jnp.concatenate([kv4[3].transpose(0, 1, 3, 2), bcast(_ones_row(seq))], axis=2)
        gt = gates[:, :3 * NSA_HEADS].reshape(batch, seq, NSA_G, NSA_R, 3).transpose(0, 2, 4, 3, 1)
        ob = _nsa_attn(flags, consts["slopes"], nq, sel, kaug, vst, kwaug, vwt, oc_t, gt,
                       batch, seq, nsa_tq, nsa_tk)
        ob = ob.reshape(batch, NSA_HEADS * NSA_DH, seq)

        xs = _outproj(xs, oa, ob, w_out[l].astype(MXU_DT), ln1_g[l], ln1_b[l], tm)

        memkv = _matmul(memn, xa_wkv[l].astype(MXU_DT), 512, MXU_DT)
        xs = _xattn(xs, xa_wq[l].astype(MXU_DT), memkv, xa_wo[l].astype(MXU_DT),
                    ln2_g[l], ln2_b[l], batch, seq, tm)

        xs = _ffn(xs, ffn_w_up[l].astype(MXU_DT), ffn_conv_w[l], ffn_conv_b[l].reshape(1, -1),
                  ffn_w_down[l].astype(MXU_DT), ln3_g[l], ln3_b[l], batch, seq, tm)

    return xs.reshape(batch, seq, d)
```

```python
import functools

import jax
import jax.numpy as jnp
from jax import lax
from jax.experimental import pallas as pl
from jax.experimental.pallas import tpu as pltpu

F32 = jnp.float32
MXU_DT = jnp.bfloat16

D_MODEL = 1024
MLA_HEADS = 8
MLA_NOPE = 64
MLA_ROPE = 32
MLA_V = 64
MLA_Q_RANK = 384
MLA_KV_RANK = 256
ROPE_BASE = 10000.0
NSA_HEADS = 8
NSA_G = 2
NSA_R = 4
NSA_DH = 64
CMP_STRIDE = 16
CMP_BLOCK = 32
CMP_HIDDEN = 128
SEL_BLOCK = 64
N_SEL = 16
WINDOW = 512
XA_HEADS = 4
XA_DH = 256
D_FF = 2816
DEPTH = 2
ALPHA = (2.0 * DEPTH) ** 0.25
LN_EPS = 1e-5
RMS_EPS = 1e-6
NEG = -1e30

LANE = 128
HP = 128
VMEM_LIMIT = 56 * 1024 * 1024

TM = 512
MLA_T = 512
MLA_HPS = 4
CMP_TQ = 512
CMP_CH = 128
NSA_TQ = 256
NSA_TK = 256
NSA_U = 2
POS_SPLIT = 128
VROWS = 80
N_FORCED = 3
LOG2E = 1.4426950408889634

_O_CQ, _O_CKV, _O_KR, _O_NQ = 0, 384, 640, 672
_O_KV6, _O_GL, _IN_COLS = 1184, 1952, 1976
_E_CQ, _E_CKV, _E_NQ, _E_KCVC, _E_KS, _E_VS, _E_KW, _E_VW, _E_KRA, _E_KRB, _E_GL, _E_END = (
    0, 384, 640, 1664, 1920, 2176, 2304, 2560, 2688, 2816, 2944, 3072)

_NT = (((1,), (1,)), ((), ()))


def _params(n_axes=1):
    return pltpu.CompilerParams(
        dimension_semantics=("arbitrary",) * n_axes,
        vmem_limit_bytes=VMEM_LIMIT)


def _const_spec(shape):
    nd = len(shape)
    return pl.BlockSpec(shape, lambda *_: (0,) * nd)


def _ln_rows(v, g, b):
    mu = jnp.mean(v, -1, keepdims=True)
    d = v - mu
    var = jnp.mean(d * d, -1, keepdims=True)
    return d * lax.rsqrt(var + LN_EPS) * g + b


def _dot(a, b):
    return jnp.dot(a, b, preferred_element_type=F32)


def _dot_nt(a, b):
    return lax.dot_general(a, b, _NT, preferred_element_type=F32)


def _ln_body(x_ref, g_ref, b_ref, o_ref):
    o_ref[...] = _ln_rows(x_ref[...], g_ref[...], b_ref[...])


def _layer_norm(x2d, g, b, tm):
    t, d = x2d.shape
    return pl.pallas_call(
        _ln_body,
        grid=(t // tm,),
        in_specs=[pl.BlockSpec((tm, d), lambda i: (i, 0)),
                  _const_spec((1, d)), _const_spec((1, d))],
        out_specs=pl.BlockSpec((tm, d), lambda i: (i, 0)),
        out_shape=jax.ShapeDtypeStruct((t, d), F32),
        compiler_params=_params(1),
        name="ln_rows",
    )(x2d, g.reshape(1, d), b.reshape(1, d))


def _matmul_body(a_ref, w_ref, o_ref):
    o_ref[...] = _dot(a_ref[...].astype(MXU_DT), w_ref[...]).astype(o_ref.dtype)


def _matmul(a, w, tn, out_dtype):
    m, k = a.shape
    n = w.shape[1]
    return pl.pallas_call(
        _matmul_body,
        grid=(n // tn,),
        in_specs=[_const_spec((m, k)), pl.BlockSpec((k, tn), lambda j: (0, j))],
        out_specs=pl.BlockSpec((m, tn), lambda j: (0, j)),
        out_shape=jax.ShapeDtypeStruct((m, n), out_dtype),
        compiler_params=_params(1),
        name="mem_kv_proj",
    )(a, w)


def _rms_rows(v, g):
    return v * lax.rsqrt(jnp.mean(v * v, -1, keepdims=True) + RMS_EPS) * g


def _inproj_body(x_ref, win_ref, cqg_ref, ckvg_ref, wqa_ref, wqb_ref, wkk_ref, wkv_ref,
                 tcq_ref, tsq_ref, tck_ref, tsk_ref, onehot_ref, posc_ref,
                 q_ref, k_ref, vt_ref, nq_ref, kcvc_ref, kaug_ref, vst_ref, kwaug_ref, vwt_ref,
                 gate_ref):
    xb = x_ref[...].astype(MXU_DT)
    h = _dot(xb, win_ref[...])
    cqn = _rms_rows(h[:, _E_CQ:_E_CKV], cqg_ref[...]).astype(MXU_DT)
    qa = _dot(cqn, wqa_ref[...])
    qb = _dot(cqn, wqb_ref[...])
    ckvn = _rms_rows(h[:, _E_CKV:_E_NQ], ckvg_ref[...]).astype(MXU_DT)
    kn = _dot(ckvn, wkk_ref[...])
    tail = jnp.where(lax.broadcasted_iota(jnp.int32, (VROWS - MLA_V, h.shape[0]), 0) == 0,
                     1.0, 0.0).astype(MXU_DT)
    vt = _dot(ckvn, wkv_ref[...]).T.astype(vt_ref.dtype)
    for hh in range(MLA_HEADS):
        vt_ref[hh * VROWS:hh * VROWS + MLA_V, :] = vt[hh * MLA_V:(hh + 1) * MLA_V, :]
        vt_ref[hh * VROWS + MLA_V:(hh + 1) * VROWS, :] = tail
    ns = onehot_ref.shape[1]
    posc = posc_ref[...]
    vst = h[:, _E_VS:_E_KW].T.astype(MXU_DT)
    vwt = h[:, _E_VW:_E_KRA].T.astype(MXU_DT)
    for g in range(NSA_G):
        kaug_ref[g, :, 0:ns] = onehot_ref[...]
        kaug_ref[g, :, ns:ns + HP] = (h[:, _E_KS + g * HP:_E_KS + (g + 1) * HP] + posc).astype(MXU_DT)
        kwaug_ref[g] = (h[:, _E_KW + g * HP:_E_KW + (g + 1) * HP] + posc).astype(MXU_DT)
        for t_ref, tv in ((vst_ref, vst), (vwt_ref, vwt)):
            t_ref[g, 0:NSA_DH, :] = tv[g * NSA_DH:(g + 1) * NSA_DH, :]
            t_ref[g, NSA_DH:VROWS, :] = tail
    krr = h[:, _E_KRA:_E_KRB] * tck_ref[...] + h[:, _E_KRB:_E_GL] * tsk_ref[...]
    tcq = tcq_ref[...]
    tsq = tsq_ref[...]
    for hh in range(MLA_HEADS):
        sl = slice(hh * HP, (hh + 1) * HP)
        q_ref[:, sl] = (qa[:, sl] * tcq + qb[:, sl] * tsq).astype(q_ref.dtype)
        k_ref[:, sl] = (kn[:, sl] + krr).astype(k_ref.dtype)
    nq_ref[...] = h[:, _E_NQ:_E_KCVC].astype(nq_ref.dtype)
    kcvc_ref[...] = h[:, _E_KCVC:_E_KS]
    gate_ref[...] = jax.nn.sigmoid(h[:, _E_GL:_E_END])


def _inproj(x2d, lw, tabs, consts, seq, tm):
    t = x2d.shape[0]
    batch = t // seq
    npos = seq // tm
    ns = seq // SEL_BLOCK
    row = lambda i: (i, 0)
    pos = lambda i: (i % npos, 0)
    tok = lambda i: (i // npos, 0, i % npos, 0)
    ttr = lambda i: (i // npos, 0, 0, i % npos)
    hw = MLA_HEADS * HP
    vrows = MLA_HEADS * VROWS
    sds = jax.ShapeDtypeStruct
    outs = [
        (sds((t, hw), MXU_DT), pl.BlockSpec((tm, hw), row)),
        (sds((t, hw), MXU_DT), pl.BlockSpec((tm, hw), row)),
        (sds((batch, vrows, seq), MXU_DT),
         pl.BlockSpec((None, vrows, tm), lambda i: (i // npos, 0, i % npos))),
        (sds((t, NSA_HEADS * HP), MXU_DT), pl.BlockSpec((tm, NSA_HEADS * HP), row)),
        (sds((t, 256), F32), pl.BlockSpec((tm, 256), row)),
        (sds((batch, NSA_G, seq, ns + HP), MXU_DT), pl.BlockSpec((None, NSA_G, tm, ns + HP), tok)),
        (sds((batch, NSA_G, VROWS, seq), MXU_DT), pl.BlockSpec((None, NSA_G, VROWS, tm), ttr)),
        (sds((batch, NSA_G, seq, HP), MXU_DT), pl.BlockSpec((None, NSA_G, tm, HP), tok)),
        (sds((batch, NSA_G, VROWS, seq), MXU_DT), pl.BlockSpec((None, NSA_G, VROWS, tm), ttr)),
        (sds((t, LANE), F32), pl.BlockSpec((tm, LANE), row)),
    ]
    out_shape = [o[0] for o in outs]
    out_specs = [o[1] for o in outs]
    return pl.pallas_call(
        _inproj_body,
        grid=(t // tm,),
        in_specs=[pl.BlockSpec((tm, D_MODEL), row),
                  _const_spec((D_MODEL, _E_END)),
                  _const_spec((1, MLA_Q_RANK)), _const_spec((1, MLA_KV_RANK)),
                  _const_spec((MLA_Q_RANK, hw)), _const_spec((MLA_Q_RANK, hw)),
                  _const_spec((MLA_KV_RANK, hw)), _const_spec((MLA_KV_RANK, MLA_HEADS * MLA_V)),
                  pl.BlockSpec((tm, HP), pos), pl.BlockSpec((tm, HP), pos),
                  pl.BlockSpec((tm, HP), pos), pl.BlockSpec((tm, HP), pos),
                  pl.BlockSpec((tm, ns), pos), pl.BlockSpec((tm, HP), pos)],
        out_specs=out_specs,
        out_shape=out_shape,
        compiler_params=_params(1),
        name="in_proj",
    )(x2d, lw["w_in"], lw["cq_g"], lw["ckv_g"], lw["wqa"], lw["wqb"], lw["wkk"], lw["wkv"],
      tabs["cq"], tabs["sq"], tabs["ck"], tabs["sk"], consts["onehot"], consts["posc"])


def _mla_attn_body(q_ref, k_ref, vt_ref, o_ref, *, tq):
    qi = pl.program_id(2)

    def step(kt, carry, diag):
        off = pl.multiple_of(kt * tq, tq)
        ss = []
        for h in range(MLA_HPS):
            hs = slice(h * HP, (h + 1) * HP)
            s = _dot_nt(k_ref[pl.ds(off, tq), hs], q_ref[:, hs])
            if diag:
                kpos = lax.broadcasted_iota(jnp.int32, (tq, tq), 0)
                qpos = lax.broadcasted_iota(jnp.int32, (tq, tq), 1)
                s = jnp.where(kpos <= qpos, s, NEG)
            ss.append(s)
        ms, alphas, ps = [], [], []
        for h in range(MLA_HPS):
            m = carry[h][0]
            m_new = jnp.maximum(m, jnp.max(ss[h], 0, keepdims=True))
            alphas.append(jnp.exp2(m - m_new))
            ps.append(jnp.exp2(ss[h] - m_new).astype(MXU_DT))
            ms.append(m_new)
        out = []
        for h in range(MLA_HPS):
            vt = vt_ref[h * VROWS:(h + 1) * VROWS, pl.ds(off, tq)]
            out.append((ms[h], alphas[h] * carry[h][1] + _dot(vt, ps[h])))
        return tuple(out)

    one = (jnp.full((1, tq), NEG, F32), jnp.zeros((VROWS, tq), F32))
    carry = lax.fori_loop(0, qi, lambda kt, c: step(kt, c, False), (one,) * MLA_HPS)
    carry = step(qi, carry, True)
    for h in range(MLA_HPS):
        acc = carry[h][1]
        o_ref[h * MLA_V:(h + 1) * MLA_V, :] = acc[0:MLA_V, :] / acc[MLA_V:MLA_V + 1, :]


def _mla_attn(q, k, vt, batch, seq, tq):
    nq = seq // tq
    pw = MLA_HPS * HP
    return pl.pallas_call(
        functools.partial(_mla_attn_body, tq=tq),
        grid=(batch, MLA_HEADS // MLA_HPS, nq),
        in_specs=[pl.BlockSpec((tq, pw), lambda b, h, i: (b * nq + i, h)),
                  pl.BlockSpec((seq, pw), lambda b, h, i: (b, h)),
                  pl.BlockSpec((None, MLA_HPS * VROWS, seq), lambda b, h, i: (b, h, 0))],
        out_specs=pl.BlockSpec((None, MLA_HPS * MLA_V, tq), lambda b, h, i: (b, h, i)),
        out_shape=jax.ShapeDtypeStruct((batch, MLA_HEADS * MLA_V, seq), F32),
        compiler_params=_params(3),
        name="mla_attn",
    )(q, k, vt)


def _compress_body(c_ref, pe_ref, w1_ref, b1_ref, w2_ref, o_ref):
    c = c_ref[...]
    half = CMP_STRIDE * NSA_DH
    lo = (c + pe_ref[0:1, :]).astype(MXU_DT)
    hi = (c + pe_ref[1:2, :]).astype(MXU_DT)
    u = _dot(lo, w1_ref[0:half, :])
    w = _dot(hi, w1_ref[half:2 * half, :])
    w = pltpu.roll(w, w.shape[0] - 1, 0)
    hid = jax.nn.gelu(u + w + b1_ref[...])
    o_ref[...] = _dot(hid.astype(MXU_DT), w2_ref[...]).astype(o_ref.dtype)


def _compress(chunks, pe, w1, b1, w2):
    _, bg, nch, cw = chunks.shape
    return pl.pallas_call(
        _compress_body,
        grid=(2, bg),
        in_specs=[pl.BlockSpec((None, None, nch, cw), lambda s, j: (s, j, 0, 0)),
                  pl.BlockSpec((None, 2, cw), lambda s, j: (s, 0, 0)),
                  pl.BlockSpec((None, 2 * cw, CMP_HIDDEN), lambda s, j: (s, 0, 0)),
                  pl.BlockSpec((None, 1, CMP_HIDDEN), lambda s, j: (s, 0, 0)),
                  pl.BlockSpec((None, CMP_HIDDEN, NSA_DH), lambda s, j: (s, 0, 0))],
        out_specs=pl.BlockSpec((None, None, nch, NSA_DH), lambda s, j: (s, j, 0, 0)),
        out_shape=jax.ShapeDtypeStruct((2, bg, nch, NSA_DH), MXU_DT),
        compiler_params=_params(2),
        name="nsa_compress",
    )(chunks, pe, w1, b1, w2)


_BIG = 1e30


def _nsa_queries(q_ref, slope_ref, g, q0, tq):
    lane = lax.broadcasted_iota(jnp.int32, (tq, HP), 1)
    q0f = q0.astype(F32)
    qs = []
    for r in range(NSA_R):
        slope = slope_ref[g * NSA_R + r]
        c = jnp.where(lane == NSA_DH, slope * POS_SPLIT,
                      jnp.where(lane == NSA_DH + 1, slope,
                                jnp.where(lane == NSA_DH + 2, -slope * q0f, 0.0)))
        qs.append(q_ref[:, r * HP:(r + 1) * HP] + c.astype(MXU_DT))
    return qs


def _cmp_attn_body(slope_ref, q_ref, kc_ref, vct_ref, ovl_ref, pool_ref,
                   oc_ref, sel_ref, flag_ref, s_scr, e_scr, *, tq, n_sel):
    g = pl.program_id(1)
    q0 = pl.program_id(2) * tq
    ns = sel_ref.shape[1]
    n = NSA_R * tq
    ch = min(CMP_CH, kc_ref.shape[0])
    qc = jnp.concatenate(_nsa_queries(q_ref, slope_ref, g, q0, tq), axis=0)
    t_lane = q0 + (lax.broadcasted_iota(jnp.int32, (1, n), 1) & (tq - 1))
    cend0 = lax.broadcasted_iota(jnp.int32, (ch, n), 0) * CMP_STRIDE + (CMP_BLOCK - 1)
    nproc = jnp.maximum(q0 + tq - CMP_BLOCK, 0) // (ch * CMP_STRIDE) + 1

    def scores(c, m):
        off = pl.multiple_of(c * ch, ch)
        s = _dot_nt(kc_ref[pl.ds(off, ch), :], qc)
        s = jnp.where(cend0 <= t_lane - off * CMP_STRIDE, s, NEG)
        s_scr[pl.ds(off, ch), :] = s
        return jnp.maximum(m, jnp.max(s, 0, keepdims=True))

    m = lax.fori_loop(0, nproc, scores, jnp.full((1, n), NEG, F32))

    def weights(c, carry):
        off = pl.multiple_of(c * ch, ch)
        e_scr[pl.ds(off, ch), :] = jnp.exp(s_scr[pl.ds(off, ch), :] - m).astype(MXU_DT)
        return carry

    e_scr[...] = jnp.zeros(e_scr.shape, MXU_DT)
    lax.fori_loop(0, nproc, weights, 0)
    e = e_scr[...]
    acc = _dot(vct_ref[...], e)
    has_key = t_lane >= CMP_BLOCK - 1
    inv = jnp.where(has_key, 1.0 / acc[NSA_DH:NSA_DH + 1, :], 0.0)
    impn = _dot(ovl_ref[...], e) * inv
    imp = impn[:, 0:tq]
    for r in range(NSA_R):
        oc_ref[r] = acc[0:NSA_DH, r * tq:(r + 1) * tq] * inv[:, r * tq:(r + 1) * tq]
        if r:
            imp = imp + impn[:, r * tq:(r + 1) * tq]

    blk = lax.broadcasted_iota(jnp.int32, (ns, tq), 0)
    cur = (q0 + lax.broadcasted_iota(jnp.int32, (ns, tq), 1)) // SEL_BLOCK
    blkf = blk.astype(F32)
    forced = (blk == 0) | (blk == cur) | (blk == cur - 1)
    score = jnp.where((blk <= cur) & jnp.logical_not(forced), imp, -1.0)
    sel = jnp.where(forced, 1.0, 0.0)
    for _ in range(n_sel - N_FORCED):
        mx = jnp.max(score, 0, keepdims=True)
        first = jnp.min(jnp.where(score == mx, blkf, float(ns)), 0, keepdims=True)
        hit = (blkf == first) & (mx >= 0.0)
        sel = jnp.where(hit, 1.0, sel)
        score = jnp.where(hit, -1.0, score)
    sel_ref[...] = sel.T.astype(sel_ref.dtype)
    ones = jnp.ones((8, NSA_TQ), MXU_DT)
    selb = sel.astype(MXU_DT)
    for j in range(tq // NSA_TQ):
        cnt = _dot_nt(ones, selb[:, j * NSA_TQ:(j + 1) * NSA_TQ])
        flag_ref[j] = _dot(cnt.astype(MXU_DT), pool_ref[...])


def _cmp_attn(slopes, nq, kc, vct, ovl, pool, batch, seq, tq):
    nq_t = seq // tq
    nch = kc.shape[-2]
    ns = seq // SEL_BLOCK
    n_sel = min(N_SEL, ns)
    sub = tq // NSA_TQ
    gw = NSA_R * HP
    n = NSA_R * tq
    return pl.pallas_call(
        functools.partial(_cmp_attn_body, tq=tq, n_sel=n_sel),
        grid=(batch, NSA_G, nq_t),
        in_specs=[pl.BlockSpec(memory_space=pltpu.SMEM),
                  pl.BlockSpec((tq, gw), lambda b, g, i: (b * nq_t + i, g)),
                  pl.BlockSpec((None, None, nch, HP), lambda b, g, i: (b, g, 0, 0)),
                  pl.BlockSpec((None, None, VROWS, nch), lambda b, g, i: (b, g, 0, 0)),
                  _const_spec(ovl.shape), _const_spec(pool.shape)],
        out_specs=[pl.BlockSpec((None, None, NSA_R, NSA_DH, tq), lambda b, g, i: (b, g, 0, 0, i)),
                   pl.BlockSpec((None, None, tq, ns), lambda b, g, i: (b, g, i, 0)),
                   pl.BlockSpec((None, None, sub, 8, LANE), lambda b, g, i: (b, g, i, 0, 0))],
        out_shape=[jax.ShapeDtypeStruct((batch, NSA_G, NSA_R, NSA_DH, seq), F32),
                   jax.ShapeDtypeStruct((batch, NSA_G, seq, ns), MXU_DT),
                   jax.ShapeDtypeStruct((batch, NSA_G, seq // NSA_TQ, 8, LANE), F32)],
        scratch_shapes=[pltpu.VMEM((nch, n), F32), pltpu.VMEM((nch, n), MXU_DT)],
        compiler_params=_params(3),
        name="nsa_cmp_attn_topk",
    )(slopes, nq, kc, vct, ovl, pool)


def _nsa_attn_body(flag_ref, slope_ref, q_ref, sel_ref, kaug_ref, vst_ref, kwaug_ref, vwt_ref,
                   oct_ref, gt_ref, o_ref, list_scr, ow_scr, *, tq, tk, nkt):
    b = pl.program_id(0)
    g = pl.program_id(1)
    qi = pl.program_id(2)
    q0 = qi * tq
    n = NSA_R * tq
    wk = WINDOW + tq

    selneg = ((sel_ref[...].astype(F32) - 1.0) * _BIG).astype(MXU_DT)
    qs = _nsa_queries(q_ref, slope_ref, g, q0, tq)
    qc = jnp.concatenate(qs, axis=0)
    qa = jnp.concatenate([jnp.concatenate([selneg, qr], axis=1) for qr in qs], axis=0)
    t_loc = lax.broadcasted_iota(jnp.int32, (1, n), 1) & (tq - 1)
    t_lane = q0 + t_loc

    kd = q0 // tk
    fbase = ((b * NSA_G + g) * pl.num_programs(2) + qi) * nkt

    def compact(kt, cnt):
        act = flag_ref[fbase + kt] > 0

        @pl.when(act)
        def _():
            list_scr[cnt] = kt
        return cnt + act.astype(jnp.int32)

    cnt = lax.fori_loop(0, kd, compact, 0)
    for j in range(NSA_U - 1):
        list_scr[cnt + j] = kd

    def tiles(idx):
        ss, offs = [], []
        for i in idx:
            off = pl.multiple_of(i * tk, tk)
            ss.append(_dot_nt(kaug_ref[pl.ds(off, tk), :], qa))
            offs.append(off)
        return ss, offs

    def update(ss, offs, carry):
        m_old, acc = carry
        m_new = m_old
        for s in ss:
            m_new = jnp.maximum(m_new, jnp.max(s, 0, keepdims=True))
        acc = jnp.exp(m_old - m_new) * acc
        for s, off in zip(ss, offs):
            acc = acc + _dot(vst_ref[:, pl.ds(off, tk)], jnp.exp(s - m_new).astype(MXU_DT))
        return m_new, acc

    def group(gi, carry):
        return update(*tiles([list_scr[gi * NSA_U + j] for j in range(NSA_U)]), carry)

    nfull = cnt // NSA_U
    carry = lax.fori_loop(0, nfull, group,
                          (jnp.full((1, n), NEG, F32), jnp.zeros((VROWS, n), F32)))
    ss, offs = tiles([list_scr[nfull * NSA_U + j] for j in range(NSA_U - 1)] + [kd])
    for j in range(NSA_U - 1):
        ss[j] = jnp.where(nfull * NSA_U + j < cnt, ss[j], NEG)
    kpos = offs[-1] + lax.broadcasted_iota(jnp.int32, (tk, n), 0)
    ss[-1] = jnp.where(kpos <= t_lane, ss[-1], NEG)
    _, acc = update(ss, offs, carry)
    o_sel = acc[0:NSA_DH, :] / acc[NSA_DH:NSA_DH + 1, :]

    k_loc = lax.broadcasted_iota(jnp.int32, (tq, n), 0)

    def window(start, banded):
        s = _dot_nt(kwaug_ref[pl.ds(start, wk), :], qc)
        if banded:
            parts = [jnp.where(k_loc > t_loc, s[0:tq], NEG), s[tq:WINDOW],
                     jnp.where(k_loc <= t_loc, s[WINDOW:wk], NEG)]
        else:
            dist = t_lane - (start + lax.broadcasted_iota(jnp.int32, (wk, n), 0))
            parts = [jnp.where((dist >= 0) & (dist < WINDOW), s, NEG)]
        m = jnp.max(parts[0], 0, keepdims=True)
        for p in parts[1:]:
            m = jnp.maximum(m, jnp.max(p, 0, keepdims=True))
        e = jnp.concatenate([jnp.exp(p - m).astype(MXU_DT) for p in parts], axis=0)
        ow = _dot(vwt_ref[:, pl.ds(start, wk)], e)
        ow_scr[...] = ow[0:NSA_DH, :] / ow[NSA_DH:NSA_DH + 1, :]

    @pl.when(q0 >= WINDOW)
    def _():
        window(pl.multiple_of(q0 - WINDOW, LANE), True)

    @pl.when(q0 < WINDOW)
    def _():
        window(0, False)

    o_win = ow_scr[...]

    for r in range(NSA_R):
        ls = slice(r * tq, (r + 1) * tq)
        o_ref[r] = (gt_ref[0, r:r + 1, :] * oct_ref[r] + gt_ref[1, r:r + 1, :] * o_sel[:, ls]
                    + gt_ref[2, r:r + 1, :] * o_win[:, ls])


def _nsa_attn(flags, slopes, nq, sel, kaug, vst, kwaug, vwt, oct_, gt, batch, seq, tq, tk):
    nq_t = seq // tq
    ns = seq // SEL_BLOCK
    nkt = seq // tk
    n = NSA_R * tq
    gw = NSA_R * HP
    bg = lambda b, g, i, f: (b, g, 0, 0)
    grid_spec = pltpu.PrefetchScalarGridSpec(
        num_scalar_prefetch=1,
        grid=(batch, NSA_G, nq_t),
        in_specs=[pl.BlockSpec(memory_space=pltpu.SMEM),
                  pl.BlockSpec((tq, gw), lambda b, g, i, f: (b * nq_t + i, g)),
                  pl.BlockSpec((None, None, tq, ns), lambda b, g, i, f: (b, g, i, 0)),
                  pl.BlockSpec((None, None, seq, ns + HP), bg),
                  pl.BlockSpec((None, None, VROWS, seq), bg),
                  pl.BlockSpec((None, None, seq, HP), bg),
                  pl.BlockSpec((None, None, VROWS, seq), bg),
                  pl.BlockSpec((None, None, NSA_R, NSA_DH, tq), lambda b, g, i, f: (b, g, 0, 0, i)),
                  pl.BlockSpec((None, None, 3, NSA_R, tq), lambda b, g, i, f: (b, g, 0, 0, i))],
        out_specs=pl.BlockSpec((None, None, NSA_R, NSA_DH, tq), lambda b, g, i, f: (b, g, 0, 0, i)),
        scratch_shapes=[pltpu.SMEM((nkt + NSA_U,), jnp.int32),
                        pltpu.VMEM((NSA_DH, n), F32)])
    return pl.pallas_call(
        functools.partial(_nsa_attn_body, tq=tq, tk=tk, nkt=nkt),
        grid_spec=grid_spec,
        out_shape=jax.ShapeDtypeStruct((batch, NSA_G, NSA_R, NSA_DH, seq), F32),
        compiler_params=_params(3),
        name="nsa_sel_win_attn",
    )(flags, slopes, nq, sel, kaug, vst, kwaug, vwt, oct_, gt)


_TN = (((0,), (0,)), ((), ()))


def _outproj_body(x_ref, oa_ref, ob_ref, wo_ref, g_ref, b_ref, y_ref):
    wa = oa_ref.shape[0]
    mix = lax.dot_general(oa_ref[...].astype(MXU_DT), wo_ref[0:wa, :], _TN,
                          preferred_element_type=F32)
    mix = mix + lax.dot_general(ob_ref[...].astype(MXU_DT), wo_ref[wa:, :], _TN,
                                preferred_element_type=F32)
    y_ref[...] = _ln_rows(ALPHA * x_ref[...] + mix, g_ref[...], b_ref[...])


def _outproj(x2d, oat, obt, wo, g, b, tm):
    batch, _, seq = oat.shape
    ns = seq // tm
    row = lambda bb, i: (bb * ns + i, 0)
    col = lambda bb, i: (bb, 0, i)
    return pl.pallas_call(
        _outproj_body,
        grid=(batch, ns),
        in_specs=[pl.BlockSpec((tm, D_MODEL), row),
                  pl.BlockSpec((None, oat.shape[1], tm), col),
                  pl.BlockSpec((None, obt.shape[1], tm), col),
                  _const_spec(wo.shape),
                  _const_spec((1, D_MODEL)), _const_spec((1, D_MODEL))],
        out_specs=pl.BlockSpec((tm, D_MODEL), row),
        out_shape=jax.ShapeDtypeStruct((batch * seq, D_MODEL), F32),
        compiler_params=_params(2),
        name="out_proj_ln",
    )(x2d, oat, obt, wo, g.reshape(1, -1), b.reshape(1, -1))


def _xattn_body(x_ref, wq_ref, kv_ref, wo_ref, g_ref, b_ref, y_ref):
    x = x_ref[...]
    q = _dot(x.astype(MXU_DT), wq_ref[...]) * (XA_DH ** -0.5)
    hw = XA_HEADS * XA_DH
    y = jnp.zeros_like(x)
    for h in range(XA_HEADS):
        sl = slice(h * XA_DH, (h + 1) * XA_DH)
        s = _dot_nt(q[:, sl].astype(MXU_DT), kv_ref[:, sl])
        e = jnp.exp(s - jnp.max(s, -1, keepdims=True))
        p = e / jnp.sum(e, -1, keepdims=True)
        o = _dot(p.astype(MXU_DT), kv_ref[:, hw + h * XA_DH:hw + (h + 1) * XA_DH])
        y = y + _dot(o.astype(MXU_DT), wo_ref[sl, :])
    y_ref[...] = _ln_rows(ALPHA * x + y, g_ref[...], b_ref[...])


def _xattn(x2d, wq, memkv, wo, g, b, batch, seq, tm):
    ns = seq // tm
    mem = memkv.shape[0] // batch
    row = lambda bb, i: (bb * ns + i, 0)
    return pl.pallas_call(
        _xattn_body,
        grid=(batch, ns),
        in_specs=[pl.BlockSpec((tm, D_MODEL), row),
                  _const_spec(wq.shape),
                  pl.BlockSpec((mem, memkv.shape[1]), lambda bb, i: (bb, 0)),
                  _const_spec(wo.shape),
                  _const_spec((1, D_MODEL)), _const_spec((1, D_MODEL))],
        out_specs=pl.BlockSpec((tm, D_MODEL), row),
        out_shape=jax.ShapeDtypeStruct((batch * seq, D_MODEL), F32),
        compiler_params=_params(2),
        name="mem_xattn_ln",
    )(x2d, wq, memkv, wo, g.reshape(1, -1), b.reshape(1, -1))


_FF_CHUNK = 256
_HALO = 8


def _ffn_body(x_ref, xp_ref, wup_ref, cw_ref, cb_ref, wdn_ref, g_ref, b_ref, y_ref):
    first = pl.program_id(1) == 0
    x = x_ref[...]
    xp = jnp.where(first, 0.0, xp_ref[...])
    xe = jnp.concatenate([xp, x], axis=0).astype(MXU_DT)

    def conv(c0):
        u = _dot(xe, wup_ref[:, c0:c0 + _FF_CHUNK])
        u1 = pltpu.roll(u, 1, 0)
        u2 = pltpu.roll(u, 2, 0)
        cw = cw_ref[:, c0:c0 + _FF_CHUNK]
        r = u * cw[2:3, :] + u1 * cw[1:2, :] + u2 * cw[0:1, :] + cb_ref[:, c0:c0 + _FF_CHUNK]
        return r[_HALO:, :]

    acc = jnp.zeros_like(x)
    for c in range(D_FF // _FF_CHUNK):
        a = conv(c * _FF_CHUNK)
        gg = conv(D_FF + c * _FF_CHUNK)
        act = (a * jax.nn.silu(gg)).astype(MXU_DT)
        acc = acc + _dot(act, wdn_ref[c * _FF_CHUNK:(c + 1) * _FF_CHUNK, :])
    y_ref[...] = _ln_rows(ALPHA * x + acc, g_ref[...], b_ref[...])


def _ffn(x2d, wup, cw, cb, wdn, g, b, batch, seq, tm):
    ns = seq // tm
    hb = tm // _HALO
    row = lambda bb, i: (bb * ns + i, 0)
    prev = lambda bb, i: (jnp.maximum((bb * ns + i) * hb - 1, 0), 0)
    return pl.pallas_call(
        _ffn_body,
        grid=(batch, ns),
        in_specs=[pl.BlockSpec((tm, D_MODEL), row),
                  pl.BlockSpec((_HALO, D_MODEL), prev),
                  _const_spec(wup.shape), _const_spec(cw.shape), _const_spec(cb.shape),
                  _const_spec(wdn.shape),
                  _const_spec((1, D_MODEL)), _const_spec((1, D_MODEL))],
        out_specs=pl.BlockSpec((tm, D_MODEL), row),
        out_shape=jax.ShapeDtypeStruct((batch * seq, D_MODEL), F32),
        compiler_params=_params(2),
        name="conv_ffn_ln",
    )(x2d, x2d, wup, cw, cb, wdn, g.reshape(1, -1), b.reshape(1, -1))


def _rot_half_cols(w):
    half = MLA_ROPE // 2
    return jnp.concatenate([-w[..., half:], w[..., :half]], axis=-1)


def _prep_layer(w_in, cq_g, w_uq, ckv_g, w_ukv):
    d = w_in.shape[0]
    z = lambda n: jnp.zeros((d, n), F32)
    kr = w_in[:, _O_KR:_O_NQ]
    wnq = w_in[:, _O_NQ:_O_KV6].reshape(d, NSA_HEADS, NSA_DH) * (NSA_DH ** -0.5)
    wnq = jnp.concatenate([wnq, jnp.zeros((d, NSA_HEADS, HP - NSA_DH), F32)], axis=-1)
    kv6 = w_in[:, _O_KV6:_O_GL].reshape(d, 6, NSA_G, NSA_DH)
    wide = lambda w: jnp.concatenate([w, jnp.zeros_like(w)], axis=-1).reshape(d, NSA_G * HP)
    w_ext = jnp.concatenate([
        w_in[:, _O_CQ:_O_CKV], w_in[:, _O_CKV:_O_KR],
        wnq.reshape(d, NSA_HEADS * HP),
        kv6[:, 0:2].reshape(d, 2 * NSA_G * NSA_DH),
        wide(kv6[:, 2]), kv6[:, 3].reshape(d, NSA_G * NSA_DH),
        wide(kv6[:, 4]), kv6[:, 5].reshape(d, NSA_G * NSA_DH),
        z(MLA_NOPE), kr, z(HP - MLA_NOPE - MLA_ROPE),
        z(MLA_NOPE), _rot_half_cols(kr), z(HP - MLA_NOPE - MLA_ROPE),
        w_in[:, _O_GL:_IN_COLS], z(_E_END - _E_GL - (_IN_COLS - _O_GL))], axis=1)
    wq = w_uq.reshape(MLA_Q_RANK, MLA_HEADS, MLA_NOPE + MLA_ROPE)
    zq = jnp.zeros((MLA_Q_RANK, MLA_HEADS, HP - MLA_NOPE - MLA_ROPE), F32)
    wqa = jnp.concatenate([wq, zq], axis=-1).reshape(MLA_Q_RANK, MLA_HEADS * HP)
    wqb = jnp.concatenate([jnp.zeros_like(wq[..., :MLA_NOPE]), _rot_half_cols(wq[..., MLA_NOPE:]), zq],
                          axis=-1).reshape(MLA_Q_RANK, MLA_HEADS * HP)
    wkv = w_ukv.reshape(MLA_KV_RANK, MLA_HEADS, MLA_NOPE + MLA_V)
    zk = jnp.zeros((MLA_KV_RANK, MLA_HEADS, HP - MLA_NOPE), F32)
    wkk = jnp.concatenate([wkv[..., :MLA_NOPE], zk], axis=-1).reshape(MLA_KV_RANK, MLA_HEADS * HP)
    wvv = wkv[..., MLA_NOPE:].reshape(MLA_KV_RANK, MLA_HEADS * MLA_V)
    return dict(w_in=w_ext.astype(MXU_DT), cq_g=cq_g.reshape(1, -1), ckv_g=ckv_g.reshape(1, -1),
                wqa=wqa.astype(MXU_DT), wqb=wqb.astype(MXU_DT),
                wkk=wkk.astype(MXU_DT), wkv=wvv.astype(MXU_DT))


def _rope_tabs(seq):
    inv = ROPE_BASE ** (-jnp.arange(0, MLA_ROPE, 2, dtype=F32) / MLA_ROPE)
    ang = jnp.arange(seq, dtype=F32)[:, None] * inv[None, :]
    cos, sin = jnp.cos(ang), jnp.sin(ang)
    pad = jnp.zeros((seq, HP - MLA_NOPE - MLA_ROPE), F32)
    tc = jnp.concatenate([jnp.ones((seq, MLA_NOPE), F32), cos, cos, pad], axis=1)
    ts = jnp.concatenate([jnp.zeros((seq, MLA_NOPE), F32), sin, sin, pad], axis=1)
    scale = (MLA_NOPE + MLA_ROPE) ** -0.5
    return dict(cq=tc * (scale * LOG2E), sq=ts * (scale * LOG2E), ck=tc, sk=ts)


def _pos_cols(pos):
    c = jnp.zeros((pos.shape[0], HP - NSA_DH), F32)
    return c.at[:, 0].set(pos // POS_SPLIT).at[:, 1].set(pos % POS_SPLIT).at[:, 2].set(1.0)


def _ones_row(n):
    return jnp.zeros((VROWS - NSA_DH, n), MXU_DT).at[0].set(1.0)


def _nsa_consts(seq):
    nch = seq // CMP_STRIDE
    ns = seq // SEL_BLOCK
    c_start = jnp.arange(nch) * CMP_STRIDE
    s_start = jnp.arange(ns) * SEL_BLOCK
    ovl = jnp.clip(jnp.minimum(c_start[None, :] + CMP_BLOCK, s_start[:, None] + SEL_BLOCK)
                   - jnp.maximum(c_start[None, :], s_start[:, None]), 0).astype(F32) / CMP_BLOCK
    pos = jnp.arange(seq)
    onehot = (pos[:, None] // SEL_BLOCK == jnp.arange(ns)[None, :]).astype(MXU_DT)
    pool = (jnp.arange(ns)[:, None] // (NSA_TK // SEL_BLOCK) == jnp.arange(LANE)[None, :])
    slopes = 2.0 ** (-8.0 * jnp.arange(1, NSA_HEADS + 1, dtype=F32) / NSA_HEADS)
    return dict(ovl=ovl.astype(MXU_DT),
                onehot=onehot,
                posc=jnp.concatenate([jnp.zeros((seq, NSA_DH), F32), _pos_cols(pos)], axis=1),
                cendc=_pos_cols(c_start + CMP_BLOCK - 1).astype(MXU_DT),
                pool=pool.astype(MXU_DT), slopes=slopes)


def _heads_first(a, batch, seq, lead):
    a = a.reshape(batch, seq, lead, NSA_G, NSA_DH)
    return a.transpose(2, 0, 3, 1, 4)


def kernel(x, mem, ln_in_g, ln_in_b, ln_mem_g, ln_mem_b, w_in, mla_cq_g, mla_w_uq, mla_ckv_g, mla_w_ukv, nsa_pe_k, nsa_w1_k, nsa_b1_k, nsa_w2_k, nsa_pe_v, nsa_w1_v, nsa_b1_v, nsa_w2_v, w_out, ln1_g, ln1_b, xa_wq, xa_wkv, xa_wo, ln2_g, ln2_b, ffn_w_up, ffn_conv_w, ffn_conv_b, ffn_w_down, ln3_g, ln3_b):
    batch, seq, d = x.shape
    t = batch * seq
    depth = w_in.shape[0]
    tm = min(TM, seq)
    nsa_tq, nsa_tk = min(NSA_TQ, seq), min(NSA_TK, seq)
    tabs = _rope_tabs(seq)
    consts = _nsa_consts(seq)
    nch = seq // CMP_STRIDE
    nkt = seq // nsa_tk
    bg = (batch, NSA_G)
    bcast = lambda a: jnp.broadcast_to(a, bg + a.shape)

    xs = _layer_norm(x.reshape(t, d), ln_in_g, ln_in_b, tm)
    memn = _layer_norm(mem.reshape(-1, d), ln_mem_g, ln_mem_b, mem.shape[0] * mem.shape[1])

    for l in range(depth):
        lw = _prep_layer(w_in[l], mla_cq_g[l], mla_w_uq[l], mla_ckv_g[l], mla_w_ukv[l])
        q, k, vt, nq, kcvc, kaug, vst, kwaug, vwt, gates = _inproj(xs, lw, tabs, consts, seq, tm)

        oa = _mla_attn(q, k, vt, batch, seq, min(MLA_T, seq))

        chunks = _heads_first(kcvc, batch, seq, 2).reshape(2, batch * NSA_G, nch, CMP_STRIDE * NSA_DH)
        pe = jnp.stack([nsa_pe_k[l], nsa_pe_v[l]]).reshape(2, 2, CMP_STRIDE * NSA_DH)
        w1 = jnp.stack([nsa_w1_k[l], nsa_w1_v[l]]).astype(MXU_DT)
        b1 = jnp.stack([nsa_b1_k[l], nsa_b1_v[l]]).reshape(2, 1, CMP_HIDDEN)
        w2 = jnp.stack([nsa_w2_k[l], nsa_w2_v[l]]).astype(MXU_DT)
        cmp = _compress(chunks, pe, w1, b1, w2).reshape((2,) + bg + (nch, NSA_DH))
        kcaug = jnp.concatenate([cmp[0], bcast(consts["cendc"])], axis=-1)
        vct = jnp.concatenate([cmp[1].transpose(0, 1, 3, 2), bcast(_ones_row(nch))], axis=2)
        oc_t, sel, cnt = _cmp_attn(consts["slopes"], nq, kcaug, vct, consts["ovl"], consts["pool"],
                                   batch, seq, min(CMP_TQ, seq))
        flags = (cnt[:, :, :, 0, :nkt] > 0).astype(jnp.int32).reshape(-1)
        gt = gates[:, :3 * NSA_HEADS].reshape(batch, seq, NSA_G, NSA_R, 3).transpose(0, 2, 4, 3, 1)
        ob = _nsa_attn(flags, consts["slopes"], nq, sel, kaug, vst, kwaug, vwt, oc_t, gt,
                       batch, seq, nsa_tq, nsa_tk)
        ob = ob.reshape(batch, NSA_HEADS * NSA_DH, seq)

        xs = _outproj(xs, oa, ob, w_out[l].astype(MXU_DT), ln1_g[l], ln1_b[l], tm)

        memkv = _matmul(memn, xa_wkv[l].astype(MXU_DT), 512, MXU_DT)
        xs = _xattn(xs, xa_wq[l].astype(MXU_DT), memkv, xa_wo[l].astype(MXU_DT),
                    ln2_g[l], ln2_b[l], batch, seq, tm)

        xs = _ffn(xs, ffn_w_up[l].astype(MXU_DT), ffn_conv_w[l], ffn_conv_b[l].reshape(1, -1),
                  ffn_w_down[l].astype(MXU_DT), ln3_g[l], ln3_b[l], batch, seq, tm)

    return xs.reshape(batch, seq, d)
```

```python
import functools

import jax
import jax.numpy as jnp
from jax import lax
from jax.experimental import pallas as pl
from jax.experimental.pallas import tpu as pltpu

F32 = jnp.float32
MXU_DT = jnp.bfloat16

D_MODEL = 1024
MLA_HEADS = 8
MLA_NOPE = 64
MLA_ROPE = 32
MLA_V = 64
MLA_Q_RANK = 384
MLA_KV_RANK = 256
ROPE_BASE = 10000.0
NSA_HEADS = 8
NSA_G = 2
NSA_R = 4
NSA_DH = 64
CMP_STRIDE = 16
CMP_BLOCK = 32
CMP_HIDDEN = 128
SEL_BLOCK = 64
N_SEL = 16
WINDOW = 512
XA_HEADS = 4
XA_DH = 256
D_FF = 2816
DEPTH = 2
ALPHA = (2.0 * DEPTH) ** 0.25
LN_EPS = 1e-5
RMS_EPS = 1e-6
NEG = -1e30

LANE = 128
HP = 128
VMEM_LIMIT = 56 * 1024 * 1024

TM = 512
MLA_T = 512
MLA_HPS = 4
CMP_TQ = 512
CMP_CH = 128
NSA_TQ = 256
NSA_TK = 256
NSA_U = 2
POS_SPLIT = 128
VROWS = 80
N_FORCED = 3
LOG2E = 1.4426950408889634

_O_CQ, _O_CKV, _O_KR, _O_NQ = 0, 384, 640, 672
_O_KV6, _O_GL, _IN_COLS = 1184, 1952, 1976
_E_CQ, _E_CKV, _E_NQ, _E_KCVC, _E_KS, _E_VS, _E_KW, _E_VW, _E_KRA, _E_KRB, _E_GL, _E_END = (
    0, 384, 640, 1664, 1920, 2176, 2304, 2560, 2688, 2816, 2944, 3072)

_NT = (((1,), (1,)), ((), ()))


def _params(n_axes=1):
    return pltpu.CompilerParams(
        dimension_semantics=("arbitrary",) * n_axes,
        vmem_limit_bytes=VMEM_LIMIT)


def _const_spec(shape):
    nd = len(shape)
    return pl.BlockSpec(shape, lambda *_: (0,) * nd)


def _ln_rows(v, g, b):
    mu = jnp.mean(v, -1, keepdims=True)
    d = v - mu
    var = jnp.mean(d * d, -1, keepdims=True)
    return d * lax.rsqrt(var + LN_EPS) * g + b


def _dot(a, b):
    return jnp.dot(a, b, preferred_element_type=F32)


def _dot_nt(a, b):
    return lax.dot_general(a, b, _NT, preferred_element_type=F32)


def _ln_body(x_ref, g_ref, b_ref, o_ref):
    o_ref[...] = _ln_rows(x_ref[...], g_ref[...], b_ref[...])


def _layer_norm(x2d, g, b, tm):
    t, d = x2d.shape
    return pl.pallas_call(
        _ln_body,
        grid=(t // tm,),
        in_specs=[pl.BlockSpec((tm, d), lambda i: (i, 0)),
                  _const_spec((1, d)), _const_spec((1, d))],
        out_specs=pl.BlockSpec((tm, d), lambda i: (i, 0)),
        out_shape=jax.ShapeDtypeStruct((t, d), F32),
        compiler_params=_params(1),
        name="ln_rows",
    )(x2d, g.reshape(1, d), b.reshape(1, d))


def _matmul_body(a_ref, w_ref, o_ref):
    o_ref[...] = _dot(a_ref[...].astype(MXU_DT), w_ref[...]).astype(o_ref.dtype)


def _matmul(a, w, tn, out_dtype):
    m, k = a.shape
    n = w.shape[1]
    return pl.pallas_call(
        _matmul_body,
        grid=(n // tn,),
        in_specs=[_const_spec((m, k)), pl.BlockSpec((k, tn), lambda j: (0, j))],
        out_specs=pl.BlockSpec((m, tn), lambda j: (0, j)),
        out_shape=jax.ShapeDtypeStruct((m, n), out_dtype),
        compiler_params=_params(1),
        name="mem_kv_proj",
    )(a, w)


def _rms_rows(v, g):
    return v * lax.rsqrt(jnp.mean(v * v, -1, keepdims=True) + RMS_EPS) * g


def _inproj_body(x_ref, win_ref, cqg_ref, ckvg_ref, wqa_ref, wqb_ref, wkk_ref, wkv_ref,
                 tcq_ref, tsq_ref, tck_ref, tsk_ref, onehot_ref, posc_ref,
                 q_ref, k_ref, vt_ref, nq_ref, kcvc_ref, kaug_ref, vst_ref, kwaug_ref, vwt_ref,
                 gate_ref):
    xb = x_ref[...].astype(MXU_DT)
    h = _dot(xb, win_ref[...])
    cqn = _rms_rows(h[:, _E_CQ:_E_CKV], cqg_ref[...]).astype(MXU_DT)
    qa = _dot(cqn, wqa_ref[...])
    qb = _dot(cqn, wqb_ref[...])
    ckvn = _rms_rows(h[:, _E_CKV:_E_NQ], ckvg_ref[...]).astype(MXU_DT)
    kn = _dot(ckvn, wkk_ref[...])
    tail = jnp.where(lax.broadcasted_iota(jnp.int32, (VROWS - MLA_V, h.shape[0]), 0) == 0,
                     1.0, 0.0).astype(MXU_DT)
    vt = _dot(ckvn, wkv_ref[...]).T.astype(vt_ref.dtype)
    for hh in range(MLA_HEADS):
        vt_ref[hh * VROWS:hh * VROWS + MLA_V, :] = vt[hh * MLA_V:(hh + 1) * MLA_V, :]
        vt_ref[hh * VROWS + MLA_V:(hh + 1) * VROWS, :] = tail
    ns = onehot_ref.shape[1]
    posc = posc_ref[...]
    vst = h[:, _E_VS:_E_KW].T.astype(MXU_DT)
    vwt = h[:, _E_VW:_E_KRA].T.astype(MXU_DT)
    for g in range(NSA_G):
        kaug_ref[g, :, 0:ns] = onehot_ref[...]
        kaug_ref[g, :, ns:ns + HP] = (h[:, _E_KS + g * HP:_E_KS + (g + 1) * HP] + posc).astype(MXU_DT)
        kwaug_ref[g] = (h[:, _E_KW + g * HP:_E_KW + (g + 1) * HP] + posc).astype(MXU_DT)
        for t_ref, tv in ((vst_ref, vst), (vwt_ref, vwt)):
            t_ref[g, 0:NSA_DH, :] = tv[g * NSA_DH:(g + 1) * NSA_DH, :]
            t_ref[g, NSA_DH:VROWS, :] = tail
    krr = h[:, _E_KRA:_E_KRB] * tck_ref[...] + h[:, _E_KRB:_E_GL] * tsk_ref[...]
    tcq = tcq_ref[...]
    tsq = tsq_ref[...]
    for hh in range(MLA_HEADS):
        sl = slice(hh * HP, (hh + 1) * HP)
        q_ref[:, sl] = (qa[:, sl] * tcq + qb[:, sl] * tsq).astype(q_ref.dtype)
        k_ref[:, sl] = (kn[:, sl] + krr).astype(k_ref.dtype)
    nq_ref[...] = h[:, _E_NQ:_E_KCVC].astype(nq_ref.dtype)
    kcvc_ref[...] = h[:, _E_KCVC:_E_KS]
    gate_ref[...] = jax.nn.sigmoid(h[:, _E_GL:_E_END])


def _inproj(x2d, lw, tabs, consts, seq, tm):
    t = x2d.shape[0]
    batch = t // seq
    npos = seq // tm
    ns = seq // SEL_BLOCK
    row = lambda i: (i, 0)
    pos = lambda i: (i % npos, 0)
    tok = lambda i: (i // npos, 0, i % npos, 0)
    ttr = lambda i: (i // npos, 0, 0, i % npos)
    hw = MLA_HEADS * HP
    vrows = MLA_HEADS * VROWS
    sds = jax.ShapeDtypeStruct
    outs = [
        (sds((t, hw), MXU_DT), pl.BlockSpec((tm, hw), row)),
        (sds((t, hw), MXU_DT), pl.BlockSpec((tm, hw), row)),
        (sds((batch, vrows, seq), MXU_DT),
         pl.BlockSpec((None, vrows, tm), lambda i: (i // npos, 0, i % npos))),
        (sds((t, NSA_HEADS * HP), MXU_DT), pl.BlockSpec((tm, NSA_HEADS * HP), row)),
        (sds((t, 256), F32), pl.BlockSpec((tm, 256), row)),
        (sds((batch, NSA_G, seq, ns + HP), MXU_DT), pl.BlockSpec((None, NSA_G, tm, ns + HP), tok)),
        (sds((batch, NSA_G, VROWS, seq), MXU_DT), pl.BlockSpec((None, NSA_G, VROWS, tm), ttr)),
        (sds((batch, NSA_G, seq, HP), MXU_DT), pl.BlockSpec((None, NSA_G, tm, HP), tok)),
        (sds((batch, NSA_G, VROWS, seq), MXU_DT), pl.BlockSpec((None, NSA_G, VROWS, tm), ttr)),
        (sds((t, LANE), F32), pl.BlockSpec((tm, LANE), row)),
    ]
    out_shape = [o[0] for o in outs]
    out_specs = [o[1] for o in outs]
    return pl.pallas_call(
        _inproj_body,
        grid=(t // tm,),
        in_specs=[pl.BlockSpec((tm, D_MODEL), row),
                  _const_spec((D_MODEL, _E_END)),
                  _const_spec((1, MLA_Q_RANK)), _const_spec((1, MLA_KV_RANK)),
                  _const_spec((MLA_Q_RANK, hw)), _const_spec((MLA_Q_RANK, hw)),
                  _const_spec((MLA_KV_RANK, hw)), _const_spec((MLA_KV_RANK, MLA_HEADS * MLA_V)),
                  pl.BlockSpec((tm, HP), pos), pl.BlockSpec((tm, HP), pos),
                  pl.BlockSpec((tm, HP), pos), pl.BlockSpec((tm, HP), pos),
                  pl.BlockSpec((tm, ns), pos), pl.BlockSpec((tm, HP), pos)],
        out_specs=out_specs,
        out_shape=out_shape,
        compiler_params=_params(1),
        name="in_proj",
    )(x2d, lw["w_in"], lw["cq_g"], lw["ckv_g"], lw["wqa"], lw["wqb"], lw["wkk"], lw["wkv"],
      tabs["cq"], tabs["sq"], tabs["ck"], tabs["sk"], consts["onehot"], consts["posc"])


def _mla_attn_body(q_ref, k_ref, vt_ref, o_ref, *, tq):
    qi = pl.program_id(2)

    def step(kt, carry, diag):
        off = pl.multiple_of(kt * tq, tq)
        ss = []
        for h in range(MLA_HPS):
            hs = slice(h * HP, (h + 1) * HP)
            s = _dot_nt(k_ref[pl.ds(off, tq), hs], q_ref[:, hs])
            if diag:
                kpos = lax.broadcasted_iota(jnp.int32, (tq, tq), 0)
                qpos = lax.broadcasted_iota(jnp.int32, (tq, tq), 1)
                s = jnp.where(kpos <= qpos, s, NEG)
            ss.append(s)
        ms, alphas, ps = [], [], []
        for h in range(MLA_HPS):
            m = carry[h][0]
            m_new = jnp.maximum(m, jnp.max(ss[h], 0, keepdims=True))
            alphas.append(jnp.exp2(m - m_new))
            ps.append(jnp.exp2(ss[h] - m_new).astype(MXU_DT))
            ms.append(m_new)
        out = []
        for h in range(MLA_HPS):
            vt = vt_ref[h * VROWS:(h + 1) * VROWS, pl.ds(off, tq)]
            out.append((ms[h], alphas[h] * carry[h][1] + _dot(vt, ps[h])))
        return tuple(out)

    one = (jnp.full((1, tq), NEG, F32), jnp.zeros((VROWS, tq), F32))
    carry = lax.fori_loop(0, qi, lambda kt, c: step(kt, c, False), (one,) * MLA_HPS)
    carry = step(qi, carry, True)
    for h in range(MLA_HPS):
        acc = carry[h][1]
        o_ref[h * MLA_V:(h + 1) * MLA_V, :] = acc[0:MLA_V, :] / acc[MLA_V:MLA_V + 1, :]


def _mla_attn(q, k, vt, batch, seq, tq):
    nq = seq // tq
    pw = MLA_HPS * HP
    return pl.pallas_call(
        functools.partial(_mla_attn_body, tq=tq),
        grid=(batch, MLA_HEADS // MLA_HPS, nq),
        in_specs=[pl.BlockSpec((tq, pw), lambda b, h, i: (b * nq + i, h)),
                  pl.BlockSpec((seq, pw), lambda b, h, i: (b, h)),
                  pl.BlockSpec((None, MLA_HPS * VROWS, seq), lambda b, h, i: (b, h, 0))],
        out_specs=pl.BlockSpec((None, MLA_HPS * MLA_V, tq), lambda b, h, i: (b, h, i)),
        out_shape=jax.ShapeDtypeStruct((batch, MLA_HEADS * MLA_V, seq), F32),
        compiler_params=_params(3),
        name="mla_attn",
    )(q, k, vt)


def _compress_body(c_ref, pe_ref, w1_ref, b1_ref, w2_ref, o_ref):
    c = c_ref[...]
    half = CMP_STRIDE * NSA_DH
    lo = (c + pe_ref[0:1, :]).astype(MXU_DT)
    hi = (c + pe_ref[1:2, :]).astype(MXU_DT)
    u = _dot(lo, w1_ref[0:half, :])
    w = _dot(hi, w1_ref[half:2 * half, :])
    w = pltpu.roll(w, w.shape[0] - 1, 0)
    hid = jax.nn.gelu(u + w + b1_ref[...])
    o_ref[...] = _dot(hid.astype(MXU_DT), w2_ref[...]).astype(o_ref.dtype)


def _compress(chunks, pe, w1, b1, w2):
    _, bg, nch, cw = chunks.shape
    return pl.pallas_call(
        _compress_body,
        grid=(2, bg),
        in_specs=[pl.BlockSpec((None, None, nch, cw), lambda s, j: (s, j, 0, 0)),
                  pl.BlockSpec((None, 2, cw), lambda s, j: (s, 0, 0)),
                  pl.BlockSpec((None, 2 * cw, CMP_HIDDEN), lambda s, j: (s, 0, 0)),
                  pl.BlockSpec((None, 1, CMP_HIDDEN), lambda s, j: (s, 0, 0)),
                  pl.BlockSpec((None, CMP_HIDDEN, NSA_DH), lambda s, j: (s, 0, 0))],
        out_specs=pl.BlockSpec((None, None, nch, NSA_DH), lambda s, j: (s, j, 0, 0)),
        out_shape=jax.ShapeDtypeStruct((2, bg, nch, NSA_DH), MXU_DT),
        compiler_params=_params(2),
        name="nsa_compress",
    )(chunks, pe, w1, b1, w2)


_BIG = 1e30


def _nsa_queries(q_ref, slope_ref, g, q0, tq):
    lane = lax.broadcasted_iota(jnp.int32, (tq, HP), 1)
    q0f = q0.astype(F32)
    qs = []
    for r in range(NSA_R):
        slope = slope_ref[g * NSA_R + r]
        c = jnp.where(lane == NSA_DH, slope * POS_SPLIT,
                      jnp.where(lane == NSA_DH + 1, slope,
                                jnp.where(lane == NSA_DH + 2, -slope * q0f, 0.0)))
        qs.append(q_ref[:, r * HP:(r + 1) * HP] + c.astype(MXU_DT))
    return qs


def _cmp_attn_body(slope_ref, q_ref, kc_ref, vct_ref, ovl_ref, pool_ref,
                   oc_ref, sel_ref, flag_ref, s_scr, e_scr, *, tq, n_sel):
    g = pl.program_id(1)
    q0 = pl.program_id(2) * tq
    ns = sel_ref.shape[1]
    n = NSA_R * tq
    ch = min(CMP_CH, kc_ref.shape[0])
    qc = jnp.concatenate(_nsa_queries(q_ref, slope_ref, g, q0, tq), axis=0)
    t_lane = q0 + (lax.broadcasted_iota(jnp.int32, (1, n), 1) & (tq - 1))
    cend0 = lax.broadcasted_iota(jnp.int32, (ch, n), 0) * CMP_STRIDE + (CMP_BLOCK - 1)
    nproc = jnp.maximum(q0 + tq - CMP_BLOCK, 0) // (ch * CMP_STRIDE) + 1

    def scores(c, m):
        off = pl.multiple_of(c * ch, ch)
        s = _dot_nt(kc_ref[pl.ds(off, ch), :], qc)
        s = jnp.where(cend0 <= t_lane - off * CMP_STRIDE, s, NEG)
        s_scr[pl.ds(off, ch), :] = s
        return jnp.maximum(m, jnp.max(s, 0, keepdims=True))

    m = lax.fori_loop(0, nproc, scores, jnp.full((1, n), NEG, F32))

    def weights(c, carry):
        off = pl.multiple_of(c * ch, ch)
        e_scr[pl.ds(off, ch), :] = jnp.exp(s_scr[pl.ds(off, ch), :] - m).astype(MXU_DT)
        return carry

    e_scr[...] = jnp.zeros(e_scr.shape, MXU_DT)
    lax.fori_loop(0, nproc, weights, 0)
    e = e_scr[...]
    acc = _dot(vct_ref[...], e)
    has_key = t_lane >= CMP_BLOCK - 1
    inv = jnp.where(has_key, 1.0 / acc[NSA_DH:NSA_DH + 1, :], 0.0)
    impn = _dot(ovl_ref[...], e) * inv
    imp = impn[:, 0:tq]
    for r in range(NSA_R):
        oc_ref[r] = acc[0:NSA_DH, r * tq:(r + 1) * tq] * inv[:, r * tq:(r + 1) * tq]
        if r:
            imp = imp + impn[:, r * tq:(r + 1) * tq]

    blk = lax.broadcasted_iota(jnp.int32, (ns, tq), 0)
    cur = (q0 + lax.broadcasted_iota(jnp.int32, (ns, tq), 1)) // SEL_BLOCK
    blkf = blk.astype(F32)
    forced = (blk == 0) | (blk == cur) | (blk == cur - 1)
    score0 = jnp.where((blk <= cur) & jnp.logical_not(forced), imp, -1.0)
    score = score0
    for _ in range(n_sel - N_FORCED):
        mx = jnp.max(score, 0, keepdims=True)
        first = jnp.min(jnp.where(score == mx, blkf, float(ns)), 0, keepdims=True)
        score = jnp.where(blkf == first, -1.0, score)
    sel = jnp.where(forced, 1.0, jnp.where(score != score0, 1.0, 0.0))
    sel_ref[...] = sel.T.astype(sel_ref.dtype)
    ones = jnp.ones((8, NSA_TQ), MXU_DT)
    selb = sel.astype(MXU_DT)
    for j in range(tq // NSA_TQ):
        cnt = _dot_nt(ones, selb[:, j * NSA_TQ:(j + 1) * NSA_TQ])
        flag_ref[j] = _dot(cnt.astype(MXU_DT), pool_ref[...])


def _cmp_attn(slopes, nq, kc, vct, ovl, pool, batch, seq, tq):
    nq_t = seq // tq
    nch = kc.shape[-2]
    ns = seq // SEL_BLOCK
    n_sel = min(N_SEL, ns)
    sub = tq // NSA_TQ
    gw = NSA_R * HP
    n = NSA_R * tq
    return pl.pallas_call(
        functools.partial(_cmp_attn_body, tq=tq, n_sel=n_sel),
        grid=(batch, NSA_G, nq_t),
        in_specs=[pl.BlockSpec(memory_space=pltpu.SMEM),
                  pl.BlockSpec((tq, gw), lambda b, g, i: (b * nq_t + i, g)),
                  pl.BlockSpec((None, None, nch, HP), lambda b, g, i: (b, g, 0, 0)),
                  pl.BlockSpec((None, None, VROWS, nch), lambda b, g, i: (b, g, 0, 0)),
                  _const_spec(ovl.shape), _const_spec(pool.shape)],
        out_specs=[pl.BlockSpec((None, None, NSA_R, NSA_DH, tq), lambda b, g, i: (b, g, 0, 0, i)),
                   pl.BlockSpec((None, None, tq, ns), lambda b, g, i: (b, g, i, 0)),
                   pl.BlockSpec((None, None, sub, 8, LANE), lambda b, g, i: (b, g, i, 0, 0))],
        out_shape=[jax.ShapeDtypeStruct((batch, NSA_G, NSA_R, NSA_DH, seq), F32),
                   jax.ShapeDtypeStruct((batch, NSA_G, seq, ns), MXU_DT),
                   jax.ShapeDtypeStruct((batch, NSA_G, seq // NSA_TQ, 8, LANE), F32)],
        scratch_shapes=[pltpu.VMEM((nch, n), F32), pltpu.VMEM((nch, n), MXU_DT)],
        compiler_params=_params(3),
        name="nsa_cmp_attn_topk",
    )(slopes, nq, kc, vct, ovl, pool)


def _nsa_attn_body(flag_ref, slope_ref, q_ref, sel_ref, kaug_ref, vst_ref, kwaug_ref, vwt_ref,
                   oct_ref, gt_ref, o_ref, list_scr, ow_scr, *, tq, tk, nkt):
    b = pl.program_id(0)
    g = pl.program_id(1)
    qi = pl.program_id(2)
    q0 = qi * tq
    n = NSA_R * tq
    wk = WINDOW + tq

    selneg = ((sel_ref[...].astype(F32) - 1.0) * _BIG).astype(MXU_DT)
    qs = _nsa_queries(q_ref, slope_ref, g, q0, tq)
    qc = jnp.concatenate(qs, axis=0)
    qa = jnp.concatenate([jnp.concatenate([selneg, qr], axis=1) for qr in qs], axis=0)
    t_loc = lax.broadcasted_iota(jnp.int32, (1, n), 1) & (tq - 1)
    t_lane = q0 + t_loc

    kd = q0 // tk
    fbase = ((b * NSA_G + g) * pl.num_programs(2) + qi) * nkt

    def compact(kt, cnt):
        act = flag_ref[fbase + kt] > 0

        @pl.when(act)
        def _():
            list_scr[cnt] = kt
        return cnt + act.astype(jnp.int32)

    cnt = lax.fori_loop(0, kd, compact, 0)
    for j in range(NSA_U - 1):
        list_scr[cnt + j] = kd

    def tiles(idx):
        ss, offs = [], []
        for i in idx:
            off = pl.multiple_of(i * tk, tk)
            ss.append(_dot_nt(kaug_ref[pl.ds(off, tk), :], qa))
            offs.append(off)
        return ss, offs

    def update(ss, offs, carry):
        m_old, acc = carry
        m_new = m_old
        for s in ss:
            m_new = jnp.maximum(m_new, jnp.max(s, 0, keepdims=True))
        acc = jnp.exp(m_old - m_new) * acc
        for s, off in zip(ss, offs):
            acc = acc + _dot(vst_ref[:, pl.ds(off, tk)], jnp.exp(s - m_new).astype(MXU_DT))
        return m_new, acc

    def group(gi, carry):
        return update(*tiles([list_scr[gi * NSA_U + j] for j in range(NSA_U)]), carry)

    nfull = cnt // NSA_U
    carry = lax.fori_loop(0, nfull, group,
                          (jnp.full((1, n), NEG, F32), jnp.zeros((VROWS, n), F32)))
    ss, offs = tiles([list_scr[nfull * NSA_U + j] for j in range(NSA_U - 1)] + [kd])
    for j in range(NSA_U - 1):
        ss[j] = jnp.where(nfull * NSA_U + j < cnt, ss[j], NEG)
    kpos = offs[-1] + lax.broadcasted_iota(jnp.int32, (tk, n), 0)
    ss[-1] = jnp.where(kpos <= t_lane, ss[-1], NEG)
    _, acc = update(ss, offs, carry)
    o_sel = acc[0:NSA_DH, :] / acc[NSA_DH:NSA_DH + 1, :]

    k_loc = lax.broadcasted_iota(jnp.int32, (tq, n), 0)

    def window(start, banded):
        s = _dot_nt(kwaug_ref[pl.ds(start, wk), :], qc)
        if banded:
            parts = [jnp.where(k_loc > t_loc, s[0:tq], NEG), s[tq:WINDOW],
                     jnp.where(k_loc <= t_loc, s[WINDOW:wk], NEG)]
        else:
            dist = t_lane - (start + lax.broadcasted_iota(jnp.int32, (wk, n), 0))
            parts = [jnp.where((dist >= 0) & (dist < WINDOW), s, NEG)]
        m = jnp.max(parts[0], 0, keepdims=True)
        for p in parts[1:]:
            m = jnp.maximum(m, jnp.max(p, 0, keepdims=True))
        e = jnp.concatenate([jnp.exp(p - m).astype(MXU_DT) for p in parts], axis=0)
        ow = _dot(vwt_ref[:, pl.ds(start, wk)], e)
        ow_scr[...] = ow[0:NSA_DH, :] / ow[NSA_DH:NSA_DH + 1, :]

    @pl.when(q0 >= WINDOW)
    def _():
        window(pl.multiple_of(q0 - WINDOW, LANE), True)

    @pl.when(q0 < WINDOW)
    def _():
        window(0, False)

    o_win = ow_scr[...]

    for r in range(NSA_R):
        ls = slice(r * tq, (r + 1) * tq)
        o_ref[r] = (gt_ref[0, r:r + 1, :] * oct_ref[r] + gt_ref[1, r:r + 1, :] * o_sel[:, ls]
                    + gt_ref[2, r:r + 1, :] * o_win[:, ls])


def _nsa_attn(flags, slopes, nq, sel, kaug, vst, kwaug, vwt, oct_, gt, batch, seq, tq, tk):
    nq_t = seq // tq
    ns = seq // SEL_BLOCK
    nkt = seq // tk
    n = NSA_R * tq
    gw = NSA_R * HP
    bg = lambda b, g, i, f: (b, g, 0, 0)
    grid_spec = pltpu.PrefetchScalarGridSpec(
        num_scalar_prefetch=1,
        grid=(batch, NSA_G, nq_t),
        in_specs=[pl.BlockSpec(memory_space=pltpu.SMEM),
                  pl.BlockSpec((tq, gw), lambda b, g, i, f: (b * nq_t + i, g)),
                  pl.BlockSpec((None, None, tq, ns), lambda b, g, i, f: (b, g, i, 0)),
                  pl.BlockSpec((None, None, seq, ns + HP), bg),
                  pl.BlockSpec((None, None, VROWS, seq), bg),
                  pl.BlockSpec((None, None, seq, HP), bg),
                  pl.BlockSpec((None, None, VROWS, seq), bg),
                  pl.BlockSpec((None, None, NSA_R, NSA_DH, tq), lambda b, g, i, f: (b, g, 0, 0, i)),
                  pl.BlockSpec((None, None, 3, NSA_R, tq), lambda b, g, i, f: (b, g, 0, 0, i))],
        out_specs=pl.BlockSpec((None, None, NSA_R, NSA_DH, tq), lambda b, g, i, f: (b, g, 0, 0, i)),
        scratch_shapes=[pltpu.SMEM((nkt + NSA_U,), jnp.int32),
                        pltpu.VMEM((NSA_DH, n), F32)])
    return pl.pallas_call(
        functools.partial(_nsa_attn_body, tq=tq, tk=tk, nkt=nkt),
        grid_spec=grid_spec,
        out_shape=jax.ShapeDtypeStruct((batch, NSA_G, NSA_R, NSA_DH, seq), F32),
        compiler_params=_params(3),
        name="nsa_sel_win_attn",
    )(flags, slopes, nq, sel, kaug, vst, kwaug, vwt, oct_, gt)


_TN = (((0,), (0,)), ((), ()))


def _outproj_body(x_ref, oa_ref, ob_ref, wo_ref, g_ref, b_ref, y_ref):
    wa = oa_ref.shape[0]
    mix = lax.dot_general(oa_ref[...].astype(MXU_DT), wo_ref[0:wa, :], _TN,
                          preferred_element_type=F32)
    mix = mix + lax.dot_general(ob_ref[...].astype(MXU_DT), wo_ref[wa:, :], _TN,
                                preferred_element_type=F32)
    y_ref[...] = _ln_rows(ALPHA * x_ref[...] + mix, g_ref[...], b_ref[...])


def _outproj(x2d, oat, obt, wo, g, b, tm):
    batch, _, seq = oat.shape
    ns = seq // tm
    row = lambda bb, i: (bb * ns + i, 0)
    col = lambda bb, i: (bb, 0, i)
    return pl.pallas_call(
        _outproj_body,
        grid=(batch, ns),
        in_specs=[pl.BlockSpec((tm, D_MODEL), row),
                  pl.BlockSpec((None, oat.shape[1], tm), col),
                  pl.BlockSpec((None, obt.shape[1], tm), col),
                  _const_spec(wo.shape),
                  _const_spec((1, D_MODEL)), _const_spec((1, D_MODEL))],
        out_specs=pl.BlockSpec((tm, D_MODEL), row),
        out_shape=jax.ShapeDtypeStruct((batch * seq, D_MODEL), F32),
        compiler_params=_params(2),
        name="out_proj_ln",
    )(x2d, oat, obt, wo, g.reshape(1, -1), b.reshape(1, -1))


def _xattn_body(x_ref, wq_ref, kv_ref, wo_ref, g_ref, b_ref, y_ref):
    x = x_ref[...]
    q = _dot(x.astype(MXU_DT), wq_ref[...]) * (XA_DH ** -0.5)
    hw = XA_HEADS * XA_DH
    y = jnp.zeros_like(x)
    for h in range(XA_HEADS):
        sl = slice(h * XA_DH, (h + 1) * XA_DH)
        s = _dot_nt(q[:, sl].astype(MXU_DT), kv_ref[:, sl])
        e = jnp.exp(s - jnp.max(s, -1, keepdims=True))
        p = e / jnp.sum(e, -1, keepdims=True)
        o = _dot(p.astype(MXU_DT), kv_ref[:, hw + h * XA_DH:hw + (h + 1) * XA_DH])
        y = y + _dot(o.astype(MXU_DT), wo_ref[sl, :])
    y_ref[...] = _ln_rows(ALPHA * x + y, g_ref[...], b_ref[...])


def _xattn(x2d, wq, memkv, wo, g, b, batch, seq, tm):
    ns = seq // tm
    mem = memkv.shape[0] // batch
    row = lambda bb, i: (bb * ns + i, 0)
    return pl.pallas_call(
        _xattn_body,
        grid=(batch, ns),
        in_specs=[pl.BlockSpec((tm, D_MODEL), row),
                  _const_spec(wq.shape),
                  pl.BlockSpec((mem, memkv.shape[1]), lambda bb, i: (bb, 0)),
                  _const_spec(wo.shape),
                  _const_spec((1, D_MODEL)), _const_spec((1, D_MODEL))],
        out_specs=pl.BlockSpec((tm, D_MODEL), row),
        out_shape=jax.ShapeDtypeStruct((batch * seq, D_MODEL), F32),
        compiler_params=_params(2),
        name="mem_xattn_ln",
    )(x2d, wq, memkv, wo, g.reshape(1, -1), b.reshape(1, -1))


_FF_CHUNK = D_FF
_HALO = 8


def _ffn_body(x_ref, xp_ref, wup_ref, cw_ref, cb_ref, wdn_ref, g_ref, b_ref, y_ref):
    first = pl.program_id(1) == 0
    x = x_ref[...]
    xp = jnp.where(first, 0.0, xp_ref[...])
    xe = jnp.concatenate([xp, x], axis=0).astype(MXU_DT)

    def conv(c0):
        u = _dot(xe, wup_ref[:, c0:c0 + _FF_CHUNK])
        u1 = pltpu.roll(u, 1, 0)
        u2 = pltpu.roll(u, 2, 0)
        cw = cw_ref[:, c0:c0 + _FF_CHUNK]
        r = u * cw[2:3, :] + u1 * cw[1:2, :] + u2 * cw[0:1, :] + cb_ref[:, c0:c0 + _FF_CHUNK]
        return r[_HALO:, :]

    acc = jnp.zeros_like(x)
    for c in range(D_FF // _FF_CHUNK):
        a = conv(c * _FF_CHUNK)
        gg = conv(D_FF + c * _FF_CHUNK)
        act = (a * jax.nn.silu(gg)).astype(MXU_DT)
        acc = acc + _dot(act, wdn_ref[c * _FF_CHUNK:(c + 1) * _FF_CHUNK, :])
    y_ref[...] = _ln_rows(ALPHA * x + acc, g_ref[...], b_ref[...])


def _ffn(x2d, wup, cw, cb, wdn, g, b, batch, seq, tm):
    ns = seq // tm
    hb = tm // _HALO
    row = lambda bb, i: (bb * ns + i, 0)
    prev = lambda bb, i: (jnp.maximum((bb * ns + i) * hb - 1, 0), 0)
    return pl.pallas_call(
        _ffn_body,
        grid=(batch, ns),
        in_specs=[pl.BlockSpec((tm, D_MODEL), row),
                  pl.BlockSpec((_HALO, D_MODEL), prev),
                  _const_spec(wup.shape), _const_spec(cw.shape), _const_spec(cb.shape),
                  _const_spec(wdn.shape),
                  _const_spec((1, D_MODEL)), _const_spec((1, D_MODEL))],
        out_specs=pl.BlockSpec((tm, D_MODEL), row),
        out_shape=jax.ShapeDtypeStruct((batch * seq, D_MODEL), F32),
        compiler_params=_params(2),
        name="conv_ffn_ln",
    )(x2d, x2d, wup, cw, cb, wdn, g.reshape(1, -1), b.reshape(1, -1))


def _rot_half_cols(w):
    half = MLA_ROPE // 2
    return jnp.concatenate([-w[..., half:], w[..., :half]], axis=-1)


def _prep_layer(w_in, cq_g, w_uq, ckv_g, w_ukv):
    d = w_in.shape[0]
    z = lambda n: jnp.zeros((d, n), F32)
    kr = w_in[:, _O_KR:_O_NQ]
    wnq = w_in[:, _O_NQ:_O_KV6].reshape(d, NSA_HEADS, NSA_DH) * (NSA_DH ** -0.5)
    wnq = jnp.concatenate([wnq, jnp.zeros((d, NSA_HEADS, HP - NSA_DH), F32)], axis=-1)
    kv6 = w_in[:, _O_KV6:_O_GL].reshape(d, 6, NSA_G, NSA_DH)
    wide = lambda w: jnp.concatenate([w, jnp.zeros_like(w)], axis=-1).reshape(d, NSA_G * HP)
    w_ext = jnp.concatenate([
        w_in[:, _O_CQ:_O_CKV], w_in[:, _O_CKV:_O_KR],
        wnq.reshape(d, NSA_HEADS * HP),
        kv6[:, 0:2].reshape(d, 2 * NSA_G * NSA_DH),
        wide(kv6[:, 2]), kv6[:, 3].reshape(d, NSA_G * NSA_DH),
        wide(kv6[:, 4]), kv6[:, 5].reshape(d, NSA_G * NSA_DH),
        z(MLA_NOPE), kr, z(HP - MLA_NOPE - MLA_ROPE),
        z(MLA_NOPE), _rot_half_cols(kr), z(HP - MLA_NOPE - MLA_ROPE),
        w_in[:, _O_GL:_IN_COLS], z(_E_END - _E_GL - (_IN_COLS - _O_GL))], axis=1)
    wq = w_uq.reshape(MLA_Q_RANK, MLA_HEADS, MLA_NOPE + MLA_ROPE)
    zq = jnp.zeros((MLA_Q_RANK, MLA_HEADS, HP - MLA_NOPE - MLA_ROPE), F32)
    wqa = jnp.concatenate([wq, zq], axis=-1).reshape(MLA_Q_RANK, MLA_HEADS * HP)
    wqb = jnp.concatenate([jnp.zeros_like(wq[..., :MLA_NOPE]), _rot_half_cols(wq[..., MLA_NOPE:]), zq],
                          axis=-1).reshape(MLA_Q_RANK, MLA_HEADS * HP)
    wkv = w_ukv.reshape(MLA_KV_RANK, MLA_HEADS, MLA_NOPE + MLA_V)
    zk = jnp.zeros((MLA_KV_RANK, MLA_HEADS, HP - MLA_NOPE), F32)
    wkk = jnp.concatenate([wkv[..., :MLA_NOPE], zk], axis=-1).reshape(MLA_KV_RANK, MLA_HEADS * HP)
    wvv = wkv[..., MLA_NOPE:].reshape(MLA_KV_RANK, MLA_HEADS * MLA_V)
    return dict(w_in=w_ext.astype(MXU_DT), cq_g=cq_g.reshape(1, -1), ckv_g=ckv_g.reshape(1, -1),
                wqa=wqa.astype(MXU_DT), wqb=wqb.astype(MXU_DT),
                wkk=wkk.astype(MXU_DT), wkv=wvv.astype(MXU_DT))


def _rope_tabs(seq):
    inv = ROPE_BASE ** (-jnp.arange(0, MLA_ROPE, 2, dtype=F32) / MLA_ROPE)
    ang = jnp.arange(seq, dtype=F32)[:, None] * inv[None, :]
    cos, sin = jnp.cos(ang), jnp.sin(ang)
    pad = jnp.zeros((seq, HP - MLA_NOPE - MLA_ROPE), F32)
    tc = jnp.concatenate([jnp.ones((seq, MLA_NOPE), F32), cos, cos, pad], axis=1)
    ts = jnp.concatenate([jnp.zeros((seq, MLA_NOPE), F32), sin, sin, pad], axis=1)
    scale = (MLA_NOPE + MLA_ROPE) ** -0.5
    return dict(cq=tc * (scale * LOG2E), sq=ts * (scale * LOG2E), ck=tc, sk=ts)


def _pos_cols(pos, width, first):
    lane = jnp.arange(width)[None, :]
    p = pos[:, None]
    return jnp.where(lane == first, p // POS_SPLIT,
                     jnp.where(lane == first + 1, p % POS_SPLIT,
                               jnp.where(lane == first + 2, 1, 0))).astype(F32)


def _ones_row(n):
    return (jnp.arange(VROWS - NSA_DH)[:, None] == jnp.zeros((1, n), jnp.int32)).astype(MXU_DT)


def _nsa_consts(seq):
    nch = seq // CMP_STRIDE
    ns = seq // SEL_BLOCK
    c_start = jnp.arange(nch) * CMP_STRIDE
    s_start = jnp.arange(ns) * SEL_BLOCK
    ovl = jnp.clip(jnp.minimum(c_start[None, :] + CMP_BLOCK, s_start[:, None] + SEL_BLOCK)
                   - jnp.maximum(c_start[None, :], s_start[:, None]), 0).astype(F32) / CMP_BLOCK
    pos = jnp.arange(seq)
    onehot = (pos[:, None] // SEL_BLOCK == jnp.arange(ns)[None, :]).astype(MXU_DT)
    pool = (jnp.arange(ns)[:, None] // (NSA_TK // SEL_BLOCK) == jnp.arange(LANE)[None, :])
    slopes = 2.0 ** (-8.0 * jnp.arange(1, NSA_HEADS + 1, dtype=F32) / NSA_HEADS)
    return dict(ovl=ovl.astype(MXU_DT),
                onehot=onehot,
                posc=_pos_cols(pos, HP, NSA_DH),
                cendc=_pos_cols(c_start + CMP_BLOCK - 1, HP - NSA_DH, 0).astype(MXU_DT),
                pool=pool.astype(MXU_DT), slopes=slopes)


def _heads_first(a, batch, seq, lead):
    a = a.reshape(batch, seq, lead, NSA_G, NSA_DH)
    return a.transpose(2, 0, 3, 1, 4)


def kernel(x, mem, ln_in_g, ln_in_b, ln_mem_g, ln_mem_b, w_in, mla_cq_g, mla_w_uq, mla_ckv_g, mla_w_ukv, nsa_pe_k, nsa_w1_k, nsa_b1_k, nsa_w2_k, nsa_pe_v, nsa_w1_v, nsa_b1_v, nsa_w2_v, w_out, ln1_g, ln1_b, xa_wq, xa_wkv, xa_wo, ln2_g, ln2_b, ffn_w_up, ffn_conv_w, ffn_conv_b, ffn_w_down, ln3_g, ln3_b):
    batch, seq, d = x.shape
    t = batch * seq
    depth = w_in.shape[0]
    tm = min(TM, seq)
    nsa_tq, nsa_tk = min(NSA_TQ, seq), min(NSA_TK, seq)
    tabs = _rope_tabs(seq)
    consts = _nsa_consts(seq)
    nch = seq // CMP_STRIDE
    nkt = seq // nsa_tk
    bg = (batch, NSA_G)
    bcast = lambda a: jnp.broadcast_to(a, bg + a.shape)

    xs = _layer_norm(x.reshape(t, d), ln_in_g, ln_in_b, tm)
    memn = _layer_norm(mem.reshape(-1, d), ln_mem_g, ln_mem_b, mem.shape[0] * mem.shape[1])

    for l in range(depth):
        lw = _prep_layer(w_in[l], mla_cq_g[l], mla_w_uq[l], mla_ckv_g[l], mla_w_ukv[l])
        q, k, vt, nq, kcvc, kaug, vst, kwaug, vwt, gates = _inproj(xs, lw, tabs, consts, seq, tm)

        oa = _mla_attn(q, k, vt, batch, seq, min(MLA_T, seq))

        chunks = _heads_first(kcvc, batch, seq, 2).reshape(2, batch * NSA_G, nch, CMP_STRIDE * NSA_DH)
        pe = jnp.stack([nsa_pe_k[l], nsa_pe_v[l]]).reshape(2, 2, CMP_STRIDE * NSA_DH)
        w1 = jnp.stack([nsa_w1_k[l], nsa_w1_v[l]]).astype(MXU_DT)
        b1 = jnp.stack([nsa_b1_k[l], nsa_b1_v[l]]).reshape(2, 1, CMP_HIDDEN)
        w2 = jnp.stack([nsa_w2_k[l], nsa_w2_v[l]]).astype(MXU_DT)
        cmp = _compress(chunks, pe, w1, b1, w2).reshape((2,) + bg + (nch, NSA_DH))
        kcaug = jnp.concatenate([cmp[0], bcast(consts["cendc"])], axis=-1)
        vct = jnp.concatenate([cmp[1].transpose(0, 1, 3, 2), bcast(_ones_row(nch))], axis=2)
        oc_t, sel, cnt = _cmp_attn(consts["slopes"], nq, kcaug, vct, consts["ovl"], consts["pool"],
                                   batch, seq, min(CMP_TQ, seq))
        flags = (cnt[:, :, :, 0, :nkt] > 0).astype(jnp.int32).reshape(-1)
        gt = gates[:, :3 * NSA_HEADS].reshape(batch, seq, NSA_G, NSA_R, 3).transpose(0, 2, 4, 3, 1)
        ob = _nsa_attn(flags, consts["slopes"], nq, sel, kaug, vst, kwaug, vwt, oc_t, gt,
                       batch, seq, nsa_tq, nsa_tk)
        ob = ob.reshape(batch, NSA_HEADS * NSA_DH, seq)

        xs = _outproj(xs, oa, ob, w_out[l].astype(MXU_DT), ln1_g[l], ln1_b[l], tm)

        memkv = _matmul(memn, xa_wkv[l].astype(MXU_DT), 512, MXU_DT)
        xs = _xattn(xs, xa_wq[l].astype(MXU_DT), memkv, xa_wo[l].astype(MXU_DT),
                    ln2_g[l], ln2_b[l], batch, seq, tm)

        xs = _ffn(xs, ffn_w_up[l].astype(MXU_DT), ffn_conv_w[l], ffn_conv_b[l].reshape(1, -1),
                  ffn_w_down[l].astype(MXU_DT), ln3_g[l], ln3_b[l], batch, seq, tm)

    return xs.reshape(batch, seq, d)
```

```python
import functools

import jax
import jax.numpy as jnp
from jax import lax
from jax.experimental import pallas as pl
from jax.experimental.pallas import tpu as pltpu

F32 = jnp.float32
MXU_DT = jnp.bfloat16

D_MODEL = 1024
MLA_HEADS = 8
MLA_NOPE = 64
MLA_ROPE = 32
MLA_V = 64
MLA_Q_RANK = 384
MLA_KV_RANK = 256
ROPE_BASE = 10000.0
NSA_HEADS = 8
NSA_G = 2
NSA_R = 4
NSA_DH = 64
CMP_STRIDE = 16
CMP_BLOCK = 32
CMP_HIDDEN = 128
SEL_BLOCK = 64
N_SEL = 16
WINDOW = 512
XA_HEADS = 4
XA_DH = 256
D_FF = 2816
DEPTH = 2
ALPHA = (2.0 * DEPTH) ** 0.25
LN_EPS = 1e-5
RMS_EPS = 1e-6
NEG = -1e30

LANE = 128
HP = 128
VMEM_LIMIT = 56 * 1024 * 1024

TM = 512
MLA_T = 512
MLA_HPS = 4
CMP_TQ = 512
CMP_CH = 128
NSA_TQ = 256
NSA_TK = 256
NSA_U = 2
POS_SPLIT = 128
VROWS = 80
N_FORCED = 3
LOG2E = 1.4426950408889634

_O_CQ, _O_CKV, _O_KR, _O_NQ = 0, 384, 640, 672
_O_KV6, _O_GL, _IN_COLS = 1184, 1952, 1976
_E_CQ, _E_CKV, _E_NQ, _E_KCVC, _E_KS, _E_VS, _E_KW, _E_VW, _E_KRA, _E_KRB, _E_GL, _E_END = (
    0, 384, 640, 1664, 1920, 2176, 2304, 2560, 2688, 2816, 2944, 3072)

_NT = (((1,), (1,)), ((), ()))


def _params(n_axes=1):
    return pltpu.CompilerParams(
        dimension_semantics=("arbitrary",) * n_axes,
        vmem_limit_bytes=VMEM_LIMIT)


def _const_spec(shape):
    nd = len(shape)
    return pl.BlockSpec(shape, lambda *_: (0,) * nd)


def _ln_rows(v, g, b):
    mu = jnp.mean(v, -1, keepdims=True)
    d = v - mu
    var = jnp.mean(d * d, -1, keepdims=True)
    return d * lax.rsqrt(var + LN_EPS) * g + b


def _dot(a, b):
    return jnp.dot(a, b, preferred_element_type=F32)


def _dot_nt(a, b):
    return lax.dot_general(a, b, _NT, preferred_element_type=F32)


def _ln_body(x_ref, g_ref, b_ref, o_ref):
    o_ref[...] = _ln_rows(x_ref[...], g_ref[...], b_ref[...])


def _layer_norm(x2d, g, b, tm):
    t, d = x2d.shape
    return pl.pallas_call(
        _ln_body,
        grid=(t // tm,),
        in_specs=[pl.BlockSpec((tm, d), lambda i: (i, 0)),
                  _const_spec((1, d)), _const_spec((1, d))],
        out_specs=pl.BlockSpec((tm, d), lambda i: (i, 0)),
        out_shape=jax.ShapeDtypeStruct((t, d), F32),
        compiler_params=_params(1),
        name="ln_rows",
    )(x2d, g.reshape(1, d), b.reshape(1, d))


def _matmul_body(a_ref, w_ref, o_ref):
    o_ref[...] = _dot(a_ref[...].astype(MXU_DT), w_ref[...]).astype(o_ref.dtype)


def _matmul(a, w, tn, out_dtype):
    m, k = a.shape
    n = w.shape[1]
    return pl.pallas_call(
        _matmul_body,
        grid=(n // tn,),
        in_specs=[_const_spec((m, k)), pl.BlockSpec((k, tn), lambda j: (0, j))],
        out_specs=pl.BlockSpec((m, tn), lambda j: (0, j)),
        out_shape=jax.ShapeDtypeStruct((m, n), out_dtype),
        compiler_params=_params(1),
        name="mem_kv_proj",
    )(a, w)


def _rms_rows(v, g):
    return v * lax.rsqrt(jnp.mean(v * v, -1, keepdims=True) + RMS_EPS) * g


def _inproj_body(x_ref, win_ref, cqg_ref, ckvg_ref, wqa_ref, wqb_ref, wkk_ref, wkv_ref,
                 tcq_ref, tsq_ref, tck_ref, tsk_ref, onehot_ref, posc_ref,
                 q_ref, k_ref, vt_ref, nq_ref, kcvc_ref, kaug_ref, vst_ref, kwaug_ref, vwt_ref,
                 gate_ref):
    xb = x_ref[...].astype(MXU_DT)
    h = _dot(xb, win_ref[...])
    cqn = _rms_rows(h[:, _E_CQ:_E_CKV], cqg_ref[...]).astype(MXU_DT)
    qa = _dot(cqn, wqa_ref[...])
    qb = _dot(cqn, wqb_ref[...])
    ckvn = _rms_rows(h[:, _E_CKV:_E_NQ], ckvg_ref[...]).astype(MXU_DT)
    kn = _dot(ckvn, wkk_ref[...])
    tail = jnp.where(lax.broadcasted_iota(jnp.int32, (VROWS - MLA_V, h.shape[0]), 0) == 0,
                     1.0, 0.0).astype(MXU_DT)
    vt = _dot(ckvn, wkv_ref[...]).T.astype(vt_ref.dtype)
    for hh in range(MLA_HEADS):
        vt_ref[hh * VROWS:hh * VROWS + MLA_V, :] = vt[hh * MLA_V:(hh + 1) * MLA_V, :]
        vt_ref[hh * VROWS + MLA_V:(hh + 1) * VROWS, :] = tail
    ns = onehot_ref.shape[1]
    posc = posc_ref[...]
    vst = h[:, _E_VS:_E_KW].T.astype(MXU_DT)
    vwt = h[:, _E_VW:_E_KRA].T.astype(MXU_DT)
    for g in range(NSA_G):
        kaug_ref[g, :, 0:ns] = onehot_ref[...]
        kaug_ref[g, :, ns:ns + HP] = (h[:, _E_KS + g * HP:_E_KS + (g + 1) * HP] + posc).astype(MXU_DT)
        kwaug_ref[g] = (h[:, _E_KW + g * HP:_E_KW + (g + 1) * HP] + posc).astype(MXU_DT)
        for t_ref, tv in ((vst_ref, vst), (vwt_ref, vwt)):
            t_ref[g, 0:NSA_DH, :] = tv[g * NSA_DH:(g + 1) * NSA_DH, :]
            t_ref[g, NSA_DH:VROWS, :] = tail
    krr = h[:, _E_KRA:_E_KRB] * tck_ref[...] + h[:, _E_KRB:_E_GL] * tsk_ref[...]
    tcq = tcq_ref[...]
    tsq = tsq_ref[...]
    for hh in range(MLA_HEADS):
        sl = slice(hh * HP, (hh + 1) * HP)
        q_ref[:, sl] = (qa[:, sl] * tcq + qb[:, sl] * tsq).astype(q_ref.dtype)
        k_ref[:, sl] = (kn[:, sl] + krr).astype(k_ref.dtype)
    nq_ref[...] = h[:, _E_NQ:_E_KCVC].astype(nq_ref.dtype)
    kcvc_ref[...] = h[:, _E_KCVC:_E_KS]
    gate_ref[...] = jax.nn.sigmoid(h[:, _E_GL:_E_END])


def _inproj(x2d, lw, tabs, consts, seq, tm):
    t = x2d.shape[0]
    batch = t // seq
    npos = seq // tm
    ns = seq // SEL_BLOCK
    row = lambda i: (i, 0)
    pos = lambda i: (i % npos, 0)
    tok = lambda i: (i // npos, 0, i % npos, 0)
    ttr = lambda i: (i // npos, 0, 0, i % npos)
    hw = MLA_HEADS * HP
    vrows = MLA_HEADS * VROWS
    sds = jax.ShapeDtypeStruct
    outs = [
        (sds((t, hw), MXU_DT), pl.BlockSpec((tm, hw), row)),
        (sds((t, hw), MXU_DT), pl.BlockSpec((tm, hw), row)),
        (sds((batch, vrows, seq), MXU_DT),
         pl.BlockSpec((None, vrows, tm), lambda i: (i // npos, 0, i % npos))),
        (sds((t, NSA_HEADS * HP), MXU_DT), pl.BlockSpec((tm, NSA_HEADS * HP), row)),
        (sds((t, 256), F32), pl.BlockSpec((tm, 256), row)),
        (sds((batch, NSA_G, seq, ns + HP), MXU_DT), pl.BlockSpec((None, NSA_G, tm, ns + HP), tok)),
        (sds((batch, NSA_G, VROWS, seq), MXU_DT), pl.BlockSpec((None, NSA_G, VROWS, tm), ttr)),
        (sds((batch, NSA_G, seq, HP), MXU_DT), pl.BlockSpec((None, NSA_G, tm, HP), tok)),
        (sds((batch, NSA_G, VROWS, seq), MXU_DT), pl.BlockSpec((None, NSA_G, VROWS, tm), ttr)),
        (sds((t, LANE), F32), pl.BlockSpec((tm, LANE), row)),
    ]
    out_shape = [o[0] for o in outs]
    out_specs = [o[1] for o in outs]
    return pl.pallas_call(
        _inproj_body,
        grid=(t // tm,),
        in_specs=[pl.BlockSpec((tm, D_MODEL), row),
                  _const_spec((D_MODEL, _E_END)),
                  _const_spec((1, MLA_Q_RANK)), _const_spec((1, MLA_KV_RANK)),
                  _const_spec((MLA_Q_RANK, hw)), _const_spec((MLA_Q_RANK, hw)),
                  _const_spec((MLA_KV_RANK, hw)), _const_spec((MLA_KV_RANK, MLA_HEADS * MLA_V)),
                  pl.BlockSpec((tm, HP), pos), pl.BlockSpec((tm, HP), pos),
                  pl.BlockSpec((tm, HP), pos), pl.BlockSpec((tm, HP), pos),
                  pl.BlockSpec((tm, ns), pos), pl.BlockSpec((tm, HP), pos)],
        out_specs=out_specs,
        out_shape=out_shape,
        compiler_params=_params(1),
        name="in_proj",
    )(x2d, lw["w_in"], lw["cq_g"], lw["ckv_g"], lw["wqa"], lw["wqb"], lw["wkk"], lw["wkv"],
      tabs["cq"], tabs["sq"], tabs["ck"], tabs["sk"], consts["onehot"], consts["posc"])


def _mla_attn_body(q_ref, k_ref, vt_ref, o_ref, *, tq):
    qi = pl.program_id(2)

    def step(kt, carry, diag):
        off = pl.multiple_of(kt * tq, tq)
        ss = []
        for h in range(MLA_HPS):
            hs = slice(h * HP, (h + 1) * HP)
            s = _dot_nt(k_ref[pl.ds(off, tq), hs], q_ref[:, hs])
            if diag:
                kpos = lax.broadcasted_iota(jnp.int32, (tq, tq), 0)
                qpos = lax.broadcasted_iota(jnp.int32, (tq, tq), 1)
                s = jnp.where(kpos <= qpos, s, NEG)
            ss.append(s)
        ms, alphas, ps = [], [], []
        for h in range(MLA_HPS):
            m = carry[h][0]
            m_new = jnp.maximum(m, jnp.max(ss[h], 0, keepdims=True))
            alphas.append(jnp.exp2(m - m_new))
            ps.append(jnp.exp2(ss[h] - m_new).astype(MXU_DT))
            ms.append(m_new)
        out = []
        for h in range(MLA_HPS):
            vt = vt_ref[h * VROWS:(h + 1) * VROWS, pl.ds(off, tq)]
            out.append((ms[h], alphas[h] * carry[h][1] + _dot(vt, ps[h])))
        return tuple(out)

    one = (jnp.full((1, tq), NEG, F32), jnp.zeros((VROWS, tq), F32))
    carry = lax.fori_loop(0, qi, lambda kt, c: step(kt, c, False), (one,) * MLA_HPS)
    carry = step(qi, carry, True)
    for h in range(MLA_HPS):
        acc = carry[h][1]
        o_ref[h * MLA_V:(h + 1) * MLA_V, :] = acc[0:MLA_V, :] / acc[MLA_V:MLA_V + 1, :]


def _mla_attn(q, k, vt, batch, seq, tq):
    nq = seq // tq
    pw = MLA_HPS * HP
    return pl.pallas_call(
        functools.partial(_mla_attn_body, tq=tq),
        grid=(batch, MLA_HEADS // MLA_HPS, nq),
        in_specs=[pl.BlockSpec((tq, pw), lambda b, h, i: (b * nq + i, h)),
                  pl.BlockSpec((seq, pw), lambda b, h, i: (b, h)),
                  pl.BlockSpec((None, MLA_HPS * VROWS, seq), lambda b, h, i: (b, h, 0))],
        out_specs=pl.BlockSpec((None, MLA_HPS * MLA_V, tq), lambda b, h, i: (b, h, i)),
        out_shape=jax.ShapeDtypeStruct((batch, MLA_HEADS * MLA_V, seq), F32),
        compiler_params=_params(3),
        name="mla_attn",
    )(q, k, vt)


def _compress_body(c_ref, pe_ref, w1_ref, b1_ref, w2_ref, o_ref):
    nch = c_ref.shape[0] // CMP_STRIDE
    u = jnp.zeros((nch, NSA_G * CMP_HIDDEN), F32)
    w = jnp.zeros((nch, NSA_G * CMP_HIDDEN), F32)
    for p in range(CMP_STRIDE):
        x = c_ref[pl.ds(p, nch, stride=CMP_STRIDE), :]
        u = u + _dot((x + pe_ref[p:p + 1, :]).astype(MXU_DT), w1_ref[p])
        w = w + _dot((x + pe_ref[CMP_STRIDE + p:CMP_STRIDE + p + 1, :]).astype(MXU_DT),
                     w1_ref[CMP_STRIDE + p])
    w = pltpu.roll(w, nch - 1, 0)
    hid = jax.nn.gelu(u + w + b1_ref[...])
    o_ref[...] = _dot(hid.astype(MXU_DT), w2_ref[...]).astype(o_ref.dtype)


def _compress(kcvc, pe, w1, b1, w2, batch, seq):
    nch = seq // CMP_STRIDE
    gw = NSA_G * NSA_DH
    gh = NSA_G * CMP_HIDDEN
    return pl.pallas_call(
        _compress_body,
        grid=(2, batch),
        in_specs=[pl.BlockSpec((seq, gw), lambda s, j: (j, s)),
                  pl.BlockSpec((None, CMP_BLOCK, gw), lambda s, j: (s, 0, 0)),
                  pl.BlockSpec((None, CMP_BLOCK, gw, gh), lambda s, j: (s, 0, 0, 0)),
                  pl.BlockSpec((None, 1, gh), lambda s, j: (s, 0, 0)),
                  pl.BlockSpec((None, gh, gw), lambda s, j: (s, 0, 0))],
        out_specs=pl.BlockSpec((None, None, nch, gw), lambda s, j: (s, j, 0, 0)),
        out_shape=jax.ShapeDtypeStruct((2, batch, nch, gw), MXU_DT),
        compiler_params=_params(2),
        name="nsa_compress",
    )(kcvc, pe, w1, b1, w2)


def _head_blockdiag(w):
    eye = jnp.eye(NSA_G, dtype=w.dtype)
    out = jnp.einsum("gh,...ab->...gahb", eye, w)
    return out.reshape(w.shape[:-2] + (NSA_G * w.shape[-2], NSA_G * w.shape[-1]))


_BIG = 1e30


def _nsa_queries(q_ref, slope_ref, g, q0, tq):
    lane = lax.broadcasted_iota(jnp.int32, (tq, HP), 1)
    q0f = q0.astype(F32)
    qs = []
    for r in range(NSA_R):
        slope = slope_ref[g * NSA_R + r]
        c = jnp.where(lane == NSA_DH, slope * POS_SPLIT,
                      jnp.where(lane == NSA_DH + 1, slope,
                                jnp.where(lane == NSA_DH + 2, -slope * q0f, 0.0)))
        qs.append(q_ref[:, r * HP:(r + 1) * HP] + c.astype(MXU_DT))
    return qs


def _cmp_attn_body(slope_ref, q_ref, kc_ref, vct_ref, ovl_ref, pool_ref,
                   oc_ref, sel_ref, flag_ref, s_scr, e_scr, *, tq, n_sel):
    g = pl.program_id(1)
    q0 = pl.program_id(2) * tq
    ns = sel_ref.shape[1]
    n = NSA_R * tq
    ch = min(CMP_CH, kc_ref.shape[0])
    qc = jnp.concatenate(_nsa_queries(q_ref, slope_ref, g, q0, tq), axis=0)
    t_lane = q0 + (lax.broadcasted_iota(jnp.int32, (1, n), 1) & (tq - 1))
    cend0 = lax.broadcasted_iota(jnp.int32, (ch, n), 0) * CMP_STRIDE + (CMP_BLOCK - 1)
    nproc = jnp.maximum(q0 + tq - CMP_BLOCK, 0) // (ch * CMP_STRIDE) + 1

    def scores(c, m):
        off = pl.multiple_of(c * ch, ch)
        s = _dot_nt(kc_ref[pl.ds(off, ch), :], qc)
        s = jnp.where(cend0 <= t_lane - off * CMP_STRIDE, s, NEG)
        s_scr[pl.ds(off, ch), :] = s
        return jnp.maximum(m, jnp.max(s, 0, keepdims=True))

    m = lax.fori_loop(0, nproc, scores, jnp.full((1, n), NEG, F32))

    def weights(c, carry):
        off = pl.multiple_of(c * ch, ch)
        e_scr[pl.ds(off, ch), :] = jnp.exp(s_scr[pl.ds(off, ch), :] - m).astype(MXU_DT)
        return carry

    e_scr[...] = jnp.zeros(e_scr.shape, MXU_DT)
    lax.fori_loop(0, nproc, weights, 0)
    e = e_scr[...]
    acc = _dot(vct_ref[...], e)
    has_key = t_lane >= CMP_BLOCK - 1
    inv = jnp.where(has_key, 1.0 / acc[NSA_DH:NSA_DH + 1, :], 0.0)
    impn = _dot(ovl_ref[...], e) * inv
    imp = impn[:, 0:tq]
    for r in range(NSA_R):
        oc_ref[r] = acc[0:NSA_DH, r * tq:(r + 1) * tq] * inv[:, r * tq:(r + 1) * tq]
        if r:
            imp = imp + impn[:, r * tq:(r + 1) * tq]

    blk = lax.broadcasted_iota(jnp.int32, (ns, tq), 0)
    cur = (q0 + lax.broadcasted_iota(jnp.int32, (ns, tq), 1)) // SEL_BLOCK
    blkf = blk.astype(F32)
    forced = (blk == 0) | (blk == cur) | (blk == cur - 1)
    score0 = jnp.where((blk <= cur) & jnp.logical_not(forced), imp, -1.0)
    score = score0
    for _ in range(n_sel - N_FORCED):
        mx = jnp.max(score, 0, keepdims=True)
        first = jnp.min(jnp.where(score == mx, blkf, float(ns)), 0, keepdims=True)
        score = jnp.where(blkf == first, -1.0, score)
    sel = jnp.where(forced, 1.0, jnp.where(score != score0, 1.0, 0.0))
    sel_ref[...] = sel.T.astype(sel_ref.dtype)
    ones = jnp.ones((8, NSA_TQ), MXU_DT)
    selb = sel.astype(MXU_DT)
    for j in range(tq // NSA_TQ):
        cnt = _dot_nt(ones, selb[:, j * NSA_TQ:(j + 1) * NSA_TQ])
        flag_ref[j] = _dot(cnt.astype(MXU_DT), pool_ref[...])


def _cmp_attn(slopes, nq, kc, vct, ovl, pool, batch, seq, tq):
    nq_t = seq // tq
    nch = kc.shape[-2]
    ns = seq // SEL_BLOCK
    n_sel = min(N_SEL, ns)
    sub = tq // NSA_TQ
    gw = NSA_R * HP
    n = NSA_R * tq
    return pl.pallas_call(
        functools.partial(_cmp_attn_body, tq=tq, n_sel=n_sel),
        grid=(batch, NSA_G, nq_t),
        in_specs=[pl.BlockSpec(memory_space=pltpu.SMEM),
                  pl.BlockSpec((tq, gw), lambda b, g, i: (b * nq_t + i, g)),
                  pl.BlockSpec((None, None, nch, HP), lambda b, g, i: (b, g, 0, 0)),
                  pl.BlockSpec((None, None, VROWS, nch), lambda b, g, i: (b, g, 0, 0)),
                  _const_spec(ovl.shape), _const_spec(pool.shape)],
        out_specs=[pl.BlockSpec((None, None, NSA_R, NSA_DH, tq), lambda b, g, i: (b, g, 0, 0, i)),
                   pl.BlockSpec((None, None, tq, ns), lambda b, g, i: (b, g, i, 0)),
                   pl.BlockSpec((None, None, sub, 8, LANE), lambda b, g, i: (b, g, i, 0, 0))],
        out_shape=[jax.ShapeDtypeStruct((batch, NSA_G, NSA_R, NSA_DH, seq), F32),
                   jax.ShapeDtypeStruct((batch, NSA_G, seq, ns), MXU_DT),
                   jax.ShapeDtypeStruct((batch, NSA_G, seq // NSA_TQ, 8, LANE), F32)],
        scratch_shapes=[pltpu.VMEM((nch, n), F32), pltpu.VMEM((nch, n), MXU_DT)],
        compiler_params=_params(3),
        name="nsa_cmp_attn_topk",
    )(slopes, nq, kc, vct, ovl, pool)


def _nsa_attn_body(flag_ref, slope_ref, q_ref, sel_ref, kaug_ref, vst_ref, kwaug_ref, vwt_ref,
                   oct_ref, gt_ref, o_ref, list_scr, os_scr, ow_scr, *, tq, tk, nkt):
    b = pl.program_id(0)
    g = pl.program_id(1)
    qi = pl.program_id(2)
    q0 = qi * tq
    n = NSA_R * tq
    wk = WINDOW + tq

    selneg = ((sel_ref[...].astype(F32) - 1.0) * _BIG).astype(MXU_DT)
    qs = _nsa_queries(q_ref, slope_ref, g, q0, tq)
    qc = jnp.concatenate(qs, axis=0)
    qa = jnp.concatenate([jnp.concatenate([selneg, qr], axis=1) for qr in qs], axis=0)
    t_loc = lax.broadcasted_iota(jnp.int32, (1, n), 1) & (tq - 1)
    t_lane = q0 + t_loc

    kd = q0 // tk
    fbase = ((b * NSA_G + g) * pl.num_programs(2) + qi) * nkt

    def compact(kt, cnt):
        act = flag_ref[fbase + kt] > 0

        @pl.when(act)
        def _():
            list_scr[cnt] = kt
        return cnt + act.astype(jnp.int32)

    cnt = lax.fori_loop(0, kd, compact, 0)
    for j in range(NSA_U - 1):
        list_scr[cnt + j] = kd

    def tiles(idx):
        ss, offs = [], []
        for i in idx:
            off = pl.multiple_of(i * tk, tk)
            ss.append(_dot_nt(kaug_ref[pl.ds(off, tk), :], qa))
            offs.append(off)
        return ss, offs

    def update(ss, offs, carry):
        m_old, acc = carry
        m_new = m_old
        for s in ss:
            m_new = jnp.maximum(m_new, jnp.max(s, 0, keepdims=True))
        acc = jnp.exp(m_old - m_new) * acc
        for s, off in zip(ss, offs):
            acc = acc + _dot(vst_ref[:, pl.ds(off, tk)], jnp.exp(s - m_new).astype(MXU_DT))
        return m_new, acc

    def group(gi, carry):
        return update(*tiles([list_scr[gi * NSA_U + j] for j in range(NSA_U)]), carry)

    nfull = cnt // NSA_U
    carry = lax.fori_loop(0, nfull, group,
                          (jnp.full((1, n), NEG, F32), jnp.zeros((VROWS, n), F32)))
    k_loc = lax.broadcasted_iota(jnp.int32, (tq, n), 0)

    def tail(start, banded):
        ss, offs = tiles([list_scr[nfull * NSA_U + j] for j in range(NSA_U - 1)] + [kd])
        sw = _dot_nt(kwaug_ref[pl.ds(start, wk), :], qc)
        for j in range(NSA_U - 1):
            ss[j] = jnp.where(nfull * NSA_U + j < cnt, ss[j], NEG)
        kpos = offs[-1] + lax.broadcasted_iota(jnp.int32, (tk, n), 0)
        ss[-1] = jnp.where(kpos <= t_lane, ss[-1], NEG)
        if banded:
            parts = [jnp.where(k_loc > t_loc, sw[0:tq], NEG), sw[tq:WINDOW],
                     jnp.where(k_loc <= t_loc, sw[WINDOW:wk], NEG)]
        else:
            dist = t_lane - (start + lax.broadcasted_iota(jnp.int32, (wk, n), 0))
            parts = [jnp.where((dist >= 0) & (dist < WINDOW), sw, NEG)]
        m_old, acc = carry
        m_new = m_old
        for s in ss:
            m_new = jnp.maximum(m_new, jnp.max(s, 0, keepdims=True))
        mw = jnp.max(parts[0], 0, keepdims=True)
        for p in parts[1:]:
            mw = jnp.maximum(mw, jnp.max(p, 0, keepdims=True))
        ps = [jnp.exp(s - m_new).astype(MXU_DT) for s in ss]
        e = jnp.concatenate([jnp.exp(p - mw).astype(MXU_DT) for p in parts], axis=0)
        acc = jnp.exp(m_old - m_new) * acc
        for p, off in zip(ps, offs):
            acc = acc + _dot(vst_ref[:, pl.ds(off, tk)], p)
        ow = _dot(vwt_ref[:, pl.ds(start, wk)], e)
        os_scr[...] = acc[0:NSA_DH, :] / acc[NSA_DH:NSA_DH + 1, :]
        ow_scr[...] = ow[0:NSA_DH, :] / ow[NSA_DH:NSA_DH + 1, :]

    @pl.when(q0 >= WINDOW)
    def _():
        tail(pl.multiple_of(q0 - WINDOW, LANE), True)

    @pl.when(q0 < WINDOW)
    def _():
        tail(0, False)

    o_sel = os_scr[...]
    o_win = ow_scr[...]

    for r in range(NSA_R):
        ls = slice(r * tq, (r + 1) * tq)
        o_ref[r] = (gt_ref[0, r:r + 1, :] * oct_ref[r] + gt_ref[1, r:r + 1, :] * o_sel[:, ls]
                    + gt_ref[2, r:r + 1, :] * o_win[:, ls])


def _nsa_attn(flags, slopes, nq, sel, kaug, vst, kwaug, vwt, oct_, gt, batch, seq, tq, tk):
    nq_t = seq // tq
    ns = seq // SEL_BLOCK
    nkt = seq // tk
    n = NSA_R * tq
    gw = NSA_R * HP
    bg = lambda b, g, i, f: (b, g, 0, 0)
    grid_spec = pltpu.PrefetchScalarGridSpec(
        num_scalar_prefetch=1,
        grid=(batch, NSA_G, nq_t),
        in_specs=[pl.BlockSpec(memory_space=pltpu.SMEM),
                  pl.BlockSpec((tq, gw), lambda b, g, i, f: (b * nq_t + i, g)),
                  pl.BlockSpec((None, None, tq, ns), lambda b, g, i, f: (b, g, i, 0)),
                  pl.BlockSpec((None, None, seq, ns + HP), bg),
                  pl.BlockSpec((None, None, VROWS, seq), bg),
                  pl.BlockSpec((None, None, seq, HP), bg),
                  pl.BlockSpec((None, None, VROWS, seq), bg),
                  pl.BlockSpec((None, None, NSA_R, NSA_DH, tq), lambda b, g, i, f: (b, g, 0, 0, i)),
                  pl.BlockSpec((None, None, 3, NSA_R, tq), lambda b, g, i, f: (b, g, 0, 0, i))],
        out_specs=pl.BlockSpec((None, None, NSA_R, NSA_DH, tq), lambda b, g, i, f: (b, g, 0, 0, i)),
        scratch_shapes=[pltpu.SMEM((nkt + NSA_U,), jnp.int32),
                        pltpu.VMEM((NSA_DH, n), F32), pltpu.VMEM((NSA_DH, n), F32)])
    return pl.pallas_call(
        functools.partial(_nsa_attn_body, tq=tq, tk=tk, nkt=nkt),
        grid_spec=grid_spec,
        out_shape=jax.ShapeDtypeStruct((batch, NSA_G, NSA_R, NSA_DH, seq), F32),
        compiler_params=_params(3),
        name="nsa_sel_win_attn",
    )(flags, slopes, nq, sel, kaug, vst, kwaug, vwt, oct_, gt)


_TN = (((0,), (0,)), ((), ()))


def _outproj_body(x_ref, oa_ref, ob_ref, wo_ref, g_ref, b_ref, y_ref):
    wa = oa_ref.shape[0]
    mix = lax.dot_general(oa_ref[...].astype(MXU_DT), wo_ref[0:wa, :], _TN,
                          preferred_element_type=F32)
    mix = mix + lax.dot_general(ob_ref[...].astype(MXU_DT), wo_ref[wa:, :], _TN,
                                preferred_element_type=F32)
    y_ref[...] = _ln_rows(ALPHA * x_ref[...] + mix, g_ref[...], b_ref[...])


def _outproj(x2d, oat, obt, wo, g, b, tm):
    batch, _, seq = oat.shape
    ns = seq // tm
    row = lambda bb, i: (bb * ns + i, 0)
    col = lambda bb, i: (bb, 0, i)
    return pl.pallas_call(
        _outproj_body,
        grid=(batch, ns),
        in_specs=[pl.BlockSpec((tm, D_MODEL), row),
                  pl.BlockSpec((None, oat.shape[1], tm), col),
                  pl.BlockSpec((None, obt.shape[1], tm), col),
                  _const_spec(wo.shape),
                  _const_spec((1, D_MODEL)), _const_spec((1, D_MODEL))],
        out_specs=pl.BlockSpec((tm, D_MODEL), row),
        out_shape=jax.ShapeDtypeStruct((batch * seq, D_MODEL), F32),
        compiler_params=_params(2),
        name="out_proj_ln",
    )(x2d, oat, obt, wo, g.reshape(1, -1), b.reshape(1, -1))


def _xattn_body(x_ref, wq_ref, kv_ref, wo_ref, g_ref, b_ref, y_ref):
    x = x_ref[...]
    q = _dot(x.astype(MXU_DT), wq_ref[...]) * (XA_DH ** -0.5)
    hw = XA_HEADS * XA_DH
    y = jnp.zeros_like(x)
    for h in range(XA_HEADS):
        sl = slice(h * XA_DH, (h + 1) * XA_DH)
        s = _dot_nt(q[:, sl].astype(MXU_DT), kv_ref[:, sl])
        e = jnp.exp(s - jnp.max(s, -1, keepdims=True))
        p = e / jnp.sum(e, -1, keepdims=True)
        o = _dot(p.astype(MXU_DT), kv_ref[:, hw + h * XA_DH:hw + (h + 1) * XA_DH])
        y = y + _dot(o.astype(MXU_DT), wo_ref[sl, :])
    y_ref[...] = _ln_rows(ALPHA * x + y, g_ref[...], b_ref[...])


def _xattn(x2d, wq, memkv, wo, g, b, batch, seq, tm):
    ns = seq // tm
    mem = memkv.shape[0] // batch
    row = lambda bb, i: (bb * ns + i, 0)
    return pl.pallas_call(
        _xattn_body,
        grid=(batch, ns),
        in_specs=[pl.BlockSpec((tm, D_MODEL), row),
                  _const_spec(wq.shape),
                  pl.BlockSpec((mem, memkv.shape[1]), lambda bb, i: (bb, 0)),
                  _const_spec(wo.shape),
                  _const_spec((1, D_MODEL)), _const_spec((1, D_MODEL))],
        out_specs=pl.BlockSpec((tm, D_MODEL), row),
        out_shape=jax.ShapeDtypeStruct((batch * seq, D_MODEL), F32),
        compiler_params=_params(2),
        name="mem_xattn_ln",
    )(x2d, wq, memkv, wo, g.reshape(1, -1), b.reshape(1, -1))


_FF_CHUNK = D_FF
_HALO = 8


def _ffn_body(x_ref, xp_ref, wup_ref, cw_ref, cb_ref, wdn_ref, g_ref, b_ref, y_ref):
    first = pl.program_id(1) == 0
    x = x_ref[...]
    xp = jnp.where(first, 0.0, xp_ref[...])
    xe = jnp.concatenate([xp, x], axis=0).astype(MXU_DT)

    def conv(c0):
        u = _dot(xe, wup_ref[:, c0:c0 + _FF_CHUNK])
        u1 = pltpu.roll(u, 1, 0)
        u2 = pltpu.roll(u, 2, 0)
        cw = cw_ref[:, c0:c0 + _FF_CHUNK]
        r = u * cw[2:3, :] + u1 * cw[1:2, :] + u2 * cw[0:1, :] + cb_ref[:, c0:c0 + _FF_CHUNK]
        return r[_HALO:, :]

    acc = jnp.zeros_like(x)
    for c in range(D_FF // _FF_CHUNK):
        a = conv(c * _FF_CHUNK)
        gg = conv(D_FF + c * _FF_CHUNK)
        act = (a * jax.nn.silu(gg)).astype(MXU_DT)
        acc = acc + _dot(act, wdn_ref[c * _FF_CHUNK:(c + 1) * _FF_CHUNK, :])
    y_ref[...] = _ln_rows(ALPHA * x + acc, g_ref[...], b_ref[...])


def _ffn(x2d, wup, cw, cb, wdn, g, b, batch, seq, tm):
    ns = seq // tm
    hb = tm // _HALO
    row = lambda bb, i: (bb * ns + i, 0)
    prev = lambda bb, i: (jnp.maximum((bb * ns + i) * hb - 1, 0), 0)
    return pl.pallas_call(
        _ffn_body,
        grid=(batch, ns),
        in_specs=[pl.BlockSpec((tm, D_MODEL), row),
                  pl.BlockSpec((_HALO, D_MODEL), prev),
                  _const_spec(wup.shape), _const_spec(cw.shape), _const_spec(cb.shape),
                  _const_spec(wdn.shape),
                  _const_spec((1, D_MODEL)), _const_spec((1, D_MODEL))],
        out_specs=pl.BlockSpec((tm, D_MODEL), row),
        out_shape=jax.ShapeDtypeStruct((batch * seq, D_MODEL), F32),
        compiler_params=_params(2),
        name="conv_ffn_ln",
    )(x2d, x2d, wup, cw, cb, wdn, g.reshape(1, -1), b.reshape(1, -1))


def _rot_half_cols(w):
    half = MLA_ROPE // 2
    return jnp.concatenate([-w[..., half:], w[..., :half]], axis=-1)


def _prep_layer(w_in, cq_g, w_uq, ckv_g, w_ukv):
    d = w_in.shape[0]
    z = lambda n: jnp.zeros((d, n), F32)
    kr = w_in[:, _O_KR:_O_NQ]
    wnq = w_in[:, _O_NQ:_O_KV6].reshape(d, NSA_HEADS, NSA_DH) * (NSA_DH ** -0.5)
    wnq = jnp.concatenate([wnq, jnp.zeros((d, NSA_HEADS, HP - NSA_DH), F32)], axis=-1)
    kv6 = w_in[:, _O_KV6:_O_GL].reshape(d, 6, NSA_G, NSA_DH)
    wide = lambda w: jnp.concatenate([w, jnp.zeros_like(w)], axis=-1).reshape(d, NSA_G * HP)
    w_ext = jnp.concatenate([
        w_in[:, _O_CQ:_O_CKV], w_in[:, _O_CKV:_O_KR],
        wnq.reshape(d, NSA_HEADS * HP),
        kv6[:, 0:2].reshape(d, 2 * NSA_G * NSA_DH),
        wide(kv6[:, 2]), kv6[:, 3].reshape(d, NSA_G * NSA_DH),
        wide(kv6[:, 4]), kv6[:, 5].reshape(d, NSA_G * NSA_DH),
        z(MLA_NOPE), kr, z(HP - MLA_NOPE - MLA_ROPE),
        z(MLA_NOPE), _rot_half_cols(kr), z(HP - MLA_NOPE - MLA_ROPE),
        w_in[:, _O_GL:_IN_COLS], z(_E_END - _E_GL - (_IN_COLS - _O_GL))], axis=1)
    wq = w_uq.reshape(MLA_Q_RANK, MLA_HEADS, MLA_NOPE + MLA_ROPE)
    zq = jnp.zeros((MLA_Q_RANK, MLA_HEADS, HP - MLA_NOPE - MLA_ROPE), F32)
    wqa = jnp.concatenate([wq, zq], axis=-1).reshape(MLA_Q_RANK, MLA_HEADS * HP)
    wqb = jnp.concatenate([jnp.zeros_like(wq[..., :MLA_NOPE]), _rot_half_cols(wq[..., MLA_NOPE:]), zq],
                          axis=-1).reshape(MLA_Q_RANK, MLA_HEADS * HP)
    wkv = w_ukv.reshape(MLA_KV_RANK, MLA_HEADS, MLA_NOPE + MLA_V)
    zk = jnp.zeros((MLA_KV_RANK, MLA_HEADS, HP - MLA_NOPE), F32)
    wkk = jnp.concatenate([wkv[..., :MLA_NOPE], zk], axis=-1).reshape(MLA_KV_RANK, MLA_HEADS * HP)
    wvv = wkv[..., MLA_NOPE:].reshape(MLA_KV_RANK, MLA_HEADS * MLA_V)
    return dict(w_in=w_ext.astype(MXU_DT), cq_g=cq_g.reshape(1, -1), ckv_g=ckv_g.reshape(1, -1),
                wqa=wqa.astype(MXU_DT), wqb=wqb.astype(MXU_DT),
                wkk=wkk.astype(MXU_DT), wkv=wvv.astype(MXU_DT))


def _rope_tabs(seq):
    inv = ROPE_BASE ** (-jnp.arange(0, MLA_ROPE, 2, dtype=F32) / MLA_ROPE)
    ang = jnp.arange(seq, dtype=F32)[:, None] * inv[None, :]
    cos, sin = jnp.cos(ang), jnp.sin(ang)
    pad = jnp.zeros((seq, HP - MLA_NOPE - MLA_ROPE), F32)
    tc = jnp.concatenate([jnp.ones((seq, MLA_NOPE), F32), cos, cos, pad], axis=1)
    ts = jnp.concatenate([jnp.zeros((seq, MLA_NOPE), F32), sin, sin, pad], axis=1)
    scale = (MLA_NOPE + MLA_ROPE) ** -0.5
    return dict(cq=tc * (scale * LOG2E), sq=ts * (scale * LOG2E), ck=tc, sk=ts)


def _pos_cols(pos, width, first):
    lane = jnp.arange(width)[None, :]
    p = pos[:, None]
    return jnp.where(lane == first, p // POS_SPLIT,
                     jnp.where(lane == first + 1, p % POS_SPLIT,
                               jnp.where(lane == first + 2, 1, 0))).astype(F32)


def _ones_row(n):
    return (jnp.arange(VROWS - NSA_DH)[:, None] == jnp.zeros((1, n), jnp.int32)).astype(MXU_DT)


def _nsa_consts(seq):
    nch = seq // CMP_STRIDE
    ns = seq // SEL_BLOCK
    c_start = jnp.arange(nch) * CMP_STRIDE
    s_start = jnp.arange(ns) * SEL_BLOCK
    ovl = jnp.clip(jnp.minimum(c_start[None, :] + CMP_BLOCK, s_start[:, None] + SEL_BLOCK)
                   - jnp.maximum(c_start[None, :], s_start[:, None]), 0).astype(F32) / CMP_BLOCK
    pos = jnp.arange(seq)
    onehot = (pos[:, None] // SEL_BLOCK == jnp.arange(ns)[None, :]).astype(MXU_DT)
    pool = (jnp.arange(ns)[:, None] // (NSA_TK // SEL_BLOCK) == jnp.arange(LANE)[None, :])
    slopes = 2.0 ** (-8.0 * jnp.arange(1, NSA_HEADS + 1, dtype=F32) / NSA_HEADS)
    return dict(ovl=ovl.astype(MXU_DT),
                onehot=onehot,
                posc=_pos_cols(pos, HP, NSA_DH),
                cendc=_pos_cols(c_start + CMP_BLOCK - 1, HP - NSA_DH, 0).astype(MXU_DT),
                pool=pool.astype(MXU_DT), slopes=slopes)


def kernel(x, mem, ln_in_g, ln_in_b, ln_mem_g, ln_mem_b, w_in, mla_cq_g, mla_w_uq, mla_ckv_g, mla_w_ukv, nsa_pe_k, nsa_w1_k, nsa_b1_k, nsa_w2_k, nsa_pe_v, nsa_w1_v, nsa_b1_v, nsa_w2_v, w_out, ln1_g, ln1_b, xa_wq, xa_wkv, xa_wo, ln2_g, ln2_b, ffn_w_up, ffn_conv_w, ffn_conv_b, ffn_w_down, ln3_g, ln3_b):
    batch, seq, d = x.shape
    t = batch * seq
    depth = w_in.shape[0]
    tm = min(TM, seq)
    nsa_tq, nsa_tk = min(NSA_TQ, seq), min(NSA_TK, seq)
    tabs = _rope_tabs(seq)
    consts = _nsa_consts(seq)
    nch = seq // CMP_STRIDE
    nkt = seq // nsa_tk
    bg = (batch, NSA_G)
    bcast = lambda a: jnp.broadcast_to(a, bg + a.shape)

    xs = _layer_norm(x.reshape(t, d), ln_in_g, ln_in_b, tm)
    memn = _layer_norm(mem.reshape(-1, d), ln_mem_g, ln_mem_b, mem.shape[0] * mem.shape[1])

    for l in range(depth):
        lw = _prep_layer(w_in[l], mla_cq_g[l], mla_w_uq[l], mla_ckv_g[l], mla_w_ukv[l])
        q, k, vt, nq, kcvc, kaug, vst, kwaug, vwt, gates = _inproj(xs, lw, tabs, consts, seq, tm)

        oa = _mla_attn(q, k, vt, batch, seq, min(MLA_T, seq))

        pe = jnp.tile(jnp.stack([nsa_pe_k[l], nsa_pe_v[l]]), (1, 1, NSA_G))
        w1 = jnp.stack([nsa_w1_k[l], nsa_w1_v[l]]).reshape(2, CMP_BLOCK, NSA_DH, CMP_HIDDEN)
        w1 = _head_blockdiag(w1).astype(MXU_DT)
        b1 = jnp.tile(jnp.stack([nsa_b1_k[l], nsa_b1_v[l]]).reshape(2, 1, CMP_HIDDEN), (1, 1, NSA_G))
        w2 = _head_blockdiag(jnp.stack([nsa_w2_k[l], nsa_w2_v[l]])).astype(MXU_DT)
        cmp = _compress(kcvc, pe, w1, b1, w2, batch, seq)
        cmp = cmp.reshape(2, batch, nch, NSA_G, NSA_DH).transpose(0, 1, 3, 2, 4)
        kcaug = jnp.concatenate([cmp[0], bcast(consts["cendc"])], axis=-1)
        vct = jnp.concatenate([cmp[1].transpose(0, 1, 3, 2), bcast(_ones_row(nch))], axis=2)
        oc_t, sel, cnt = _cmp_attn(consts["slopes"], nq, kcaug, vct, consts["ovl"], consts["pool"],
                                   batch, seq, min(CMP_TQ, seq))
        flags = (cnt[:, :, :, 0, :nkt] > 0).astype(jnp.int32).reshape(-1)
        gt = gates[:, :3 * NSA_HEADS].reshape(batch, seq, NSA_G, NSA_R, 3).transpose(0, 2, 4, 3, 1)
        ob = _nsa_attn(flags, consts["slopes"], nq, sel, kaug, vst, kwaug, vwt, oc_t, gt,
                       batch, seq, nsa_tq, nsa_tk)
        ob = ob.reshape(batch, NSA_HEADS * NSA_DH, seq)

        xs = _outproj(xs, oa, ob, w_out[l].astype(MXU_DT), ln1_g[l], ln1_b[l], tm)

        memkv = _matmul(memn, xa_wkv[l].astype(MXU_DT), 512, MXU_DT)
        xs = _xattn(xs, xa_wq[l].astype(MXU_DT), memkv, xa_wo[l].astype(MXU_DT),
                    ln2_g[l], ln2_b[l], batch, seq, tm)

        xs = _ffn(xs, ffn_w_up[l].astype(MXU_DT), ffn_conv_w[l], ffn_conv_b[l].reshape(1, -1),
                  ffn_w_down[l].astype(MXU_DT), ln3_g[l], ln3_b[l], batch, seq, tm)

    return xs.reshape(batch, seq, d)
```

```python
import functools

import jax
import jax.numpy as jnp
from jax import lax
from jax.experimental import pallas as pl
from jax.experimental.pallas import tpu as pltpu

F32 = jnp.float32
MXU_DT = jnp.bfloat16

D_MODEL = 1024
MLA_HEADS = 8
MLA_NOPE = 64
MLA_ROPE = 32
MLA_V = 64
MLA_Q_RANK = 384
MLA_KV_RANK = 256
ROPE_BASE = 10000.0
NSA_HEADS = 8
NSA_G = 2
NSA_R = 4
NSA_DH = 64
CMP_STRIDE = 16
CMP_BLOCK = 32
CMP_HIDDEN = 128
SEL_BLOCK = 64
N_SEL = 16
WINDOW = 512
XA_HEADS = 4
XA_DH = 256
D_FF = 2816
DEPTH = 2
ALPHA = (2.0 * DEPTH) ** 0.25
LN_EPS = 1e-5
RMS_EPS = 1e-6
NEG = -1e30

LANE = 128
HP = 128
VMEM_LIMIT = 56 * 1024 * 1024

TM = 512
MLA_T = 512
MLA_HPS = 4
CMP_TQ = 512
CMP_CH = 128
NSA_TQ = 256
NSA_TK = 256
NSA_U = 2
POS_SPLIT = 128
VROWS = 80
N_FORCED = 3
LOG2E = 1.4426950408889634

_O_CQ, _O_CKV, _O_KR, _O_NQ = 0, 384, 640, 672
_O_KV6, _O_GL, _IN_COLS = 1184, 1952, 1976
_E_CQ, _E_CKV, _E_NQ, _E_KCVC, _E_KS, _E_VS, _E_KW, _E_VW, _E_MISC, _E_END = (
    0, 384, 640, 1152, 1408, 1536, 1664, 1792, 1920, 2048)
_M_KR, _M_GL, _M_KROT = 0, 32, 64

_NT = (((1,), (1,)), ((), ()))


def _params(n_axes=1):
    return pltpu.CompilerParams(
        dimension_semantics=("arbitrary",) * n_axes,
        vmem_limit_bytes=VMEM_LIMIT)


def _const_spec(shape):
    nd = len(shape)
    return pl.BlockSpec(shape, lambda *_: (0,) * nd)


def _ln_rows(v, g, b):
    mu = jnp.mean(v, -1, keepdims=True)
    d = v - mu
    var = jnp.mean(d * d, -1, keepdims=True)
    return d * lax.rsqrt(var + LN_EPS) * g + b


def _dot(a, b):
    return jnp.dot(a, b, preferred_element_type=F32)


def _dot_nt(a, b):
    return lax.dot_general(a, b, _NT, preferred_element_type=F32)


def _ln_body(x_ref, g_ref, b_ref, o_ref):
    o_ref[...] = _ln_rows(x_ref[...], g_ref[...], b_ref[...])


def _layer_norm(x2d, g, b, tm):
    t, d = x2d.shape
    return pl.pallas_call(
        _ln_body,
        grid=(t // tm,),
        in_specs=[pl.BlockSpec((tm, d), lambda i: (i, 0)),
                  _const_spec((1, d)), _const_spec((1, d))],
        out_specs=pl.BlockSpec((tm, d), lambda i: (i, 0)),
        out_shape=jax.ShapeDtypeStruct((t, d), F32),
        compiler_params=_params(1),
        name="ln_rows",
    )(x2d, g.reshape(1, d), b.reshape(1, d))


def _matmul_body(a_ref, w_ref, o_ref):
    o_ref[...] = _dot(a_ref[...].astype(MXU_DT), w_ref[...]).astype(o_ref.dtype)


def _matmul(a, w, tn, out_dtype):
    m, k = a.shape
    n = w.shape[1]
    return pl.pallas_call(
        _matmul_body,
        grid=(n // tn,),
        in_specs=[_const_spec((m, k)), pl.BlockSpec((k, tn), lambda j: (0, j))],
        out_specs=pl.BlockSpec((m, tn), lambda j: (0, j)),
        out_shape=jax.ShapeDtypeStruct((m, n), out_dtype),
        compiler_params=_params(1),
        name="mem_kv_proj",
    )(a, w)


def _rms_rows(v, g):
    return v * lax.rsqrt(jnp.mean(v * v, -1, keepdims=True) + RMS_EPS) * g


def _inproj_body(x_ref, win_ref, cqg_ref, ckvg_ref, wqa_ref, wqb_ref, wkk_ref, wkv_ref,
                 tcq_ref, tsq_ref, tck_ref, tsk_ref, onehot_ref, posc_ref,
                 q_ref, k_ref, vt_ref, nq_ref, kcvc_ref, kaug_ref, vst_ref, kwaug_ref, vwt_ref,
                 gate_ref):
    xb = x_ref[...].astype(MXU_DT)
    h = _dot(xb, win_ref[...])
    cqn = _rms_rows(h[:, _E_CQ:_E_CKV], cqg_ref[...]).astype(MXU_DT)
    qa = _dot(cqn, wqa_ref[...])
    qb = _dot(cqn, wqb_ref[...])
    ckvn = _rms_rows(h[:, _E_CKV:_E_NQ], ckvg_ref[...]).astype(MXU_DT)
    kn = _dot(ckvn, wkk_ref[...])
    tail = jnp.where(lax.broadcasted_iota(jnp.int32, (VROWS - MLA_V, h.shape[0]), 0) == 0,
                     1.0, 0.0).astype(MXU_DT)
    vt = _dot(ckvn, wkv_ref[...]).T.astype(vt_ref.dtype)
    for hh in range(MLA_HEADS):
        vt_ref[hh * VROWS:hh * VROWS + MLA_V, :] = vt[hh * MLA_V:(hh + 1) * MLA_V, :]
        vt_ref[hh * VROWS + MLA_V:(hh + 1) * VROWS, :] = tail
    ns = onehot_ref.shape[1]
    posc = posc_ref[...]
    lane = lax.broadcasted_iota(jnp.int32, (h.shape[0], HP), 1)
    low = lane < NSA_DH

    def halves(blk):
        return jnp.where(low, blk, 0.0), jnp.where(low, pltpu.roll(blk, NSA_DH, 1), 0.0)

    ks = halves(h[:, _E_KS:_E_VS])
    kw = halves(h[:, _E_KW:_E_VW])
    vst = h[:, _E_VS:_E_KW].T.astype(MXU_DT)
    vwt = h[:, _E_VW:_E_MISC].T.astype(MXU_DT)
    for g in range(NSA_G):
        kaug_ref[g, :, 0:ns] = onehot_ref[...]
        kaug_ref[g, :, ns:ns + HP] = (ks[g] + posc).astype(MXU_DT)
        kwaug_ref[g] = (kw[g] + posc).astype(MXU_DT)
        for t_ref, tv in ((vst_ref, vst), (vwt_ref, vwt)):
            t_ref[g, 0:NSA_DH, :] = tv[g * NSA_DH:(g + 1) * NSA_DH, :]
            t_ref[g, NSA_DH:VROWS, :] = tail
    for j in range(NSA_HEADS // 2):
        pair = halves(h[:, _E_NQ + j * HP:_E_NQ + (j + 1) * HP])
        for i in range(2):
            nq_ref[:, (2 * j + i) * HP:(2 * j + i + 1) * HP] = pair[i].astype(nq_ref.dtype)
    misc = h[:, _E_MISC:_E_END]
    krr = (jnp.where(low, 0.0, pltpu.roll(misc, _M_KROT - _M_KR, 1)) * tck_ref[...]
           + misc * tsk_ref[...])
    tcq = tcq_ref[...]
    tsq = tsq_ref[...]
    for hh in range(MLA_HEADS):
        sl = slice(hh * HP, (hh + 1) * HP)
        q_ref[:, sl] = (qa[:, sl] * tcq + qb[:, sl] * tsq).astype(q_ref.dtype)
        k_ref[:, sl] = (kn[:, sl] + krr).astype(k_ref.dtype)
    kcvc_ref[...] = h[:, _E_KCVC:_E_KS]
    gate_ref[...] = jax.nn.sigmoid(misc)


def _inproj(x2d, lw, tabs, consts, seq, tm):
    t = x2d.shape[0]
    batch = t // seq
    npos = seq // tm
    ns = seq // SEL_BLOCK
    row = lambda i: (i, 0)
    pos = lambda i: (i % npos, 0)
    tok = lambda i: (i // npos, 0, i % npos, 0)
    ttr = lambda i: (i // npos, 0, 0, i % npos)
    hw = MLA_HEADS * HP
    vrows = MLA_HEADS * VROWS
    sds = jax.ShapeDtypeStruct
    outs = [
        (sds((t, hw), MXU_DT), pl.BlockSpec((tm, hw), row)),
        (sds((t, hw), MXU_DT), pl.BlockSpec((tm, hw), row)),
        (sds((batch, vrows, seq), MXU_DT),
         pl.BlockSpec((None, vrows, tm), lambda i: (i // npos, 0, i % npos))),
        (sds((t, NSA_HEADS * HP), MXU_DT), pl.BlockSpec((tm, NSA_HEADS * HP), row)),
        (sds((t, 256), F32), pl.BlockSpec((tm, 256), row)),
        (sds((batch, NSA_G, seq, ns + HP), MXU_DT), pl.BlockSpec((None, NSA_G, tm, ns + HP), tok)),
        (sds((batch, NSA_G, VROWS, seq), MXU_DT), pl.BlockSpec((None, NSA_G, VROWS, tm), ttr)),
        (sds((batch, NSA_G, seq, HP), MXU_DT), pl.BlockSpec((None, NSA_G, tm, HP), tok)),
        (sds((batch, NSA_G, VROWS, seq), MXU_DT), pl.BlockSpec((None, NSA_G, VROWS, tm), ttr)),
        (sds((t, LANE), F32), pl.BlockSpec((tm, LANE), row)),
    ]
    out_shape = [o[0] for o in outs]
    out_specs = [o[1] for o in outs]
    return pl.pallas_call(
        _inproj_body,
        grid=(t // tm,),
        in_specs=[pl.BlockSpec((tm, D_MODEL), row),
                  _const_spec((D_MODEL, _E_END)),
                  _const_spec((1, MLA_Q_RANK)), _const_spec((1, MLA_KV_RANK)),
                  _const_spec((MLA_Q_RANK, hw)), _const_spec((MLA_Q_RANK, hw)),
                  _const_spec((MLA_KV_RANK, hw)), _const_spec((MLA_KV_RANK, MLA_HEADS * MLA_V)),
                  pl.BlockSpec((tm, HP), pos), pl.BlockSpec((tm, HP), pos),
                  pl.BlockSpec((tm, HP), pos), pl.BlockSpec((tm, HP), pos),
                  pl.BlockSpec((tm, ns), pos), pl.BlockSpec((tm, HP), pos)],
        out_specs=out_specs,
        out_shape=out_shape,
        compiler_params=_params(1),
        name="in_proj",
    )(x2d, lw["w_in"], lw["cq_g"], lw["ckv_g"], lw["wqa"], lw["wqb"], lw["wkk"], lw["wkv"],
      tabs["cq"], tabs["sq"], tabs["ck"], tabs["sk"], consts["onehot"], consts["posc"])


def _mla_attn_body(q_ref, k_ref, vt_ref, o_ref, *, tq):
    qi = pl.program_id(2)

    tk2 = 2 * tq

    def step(off, nk, carry, diag_from):
        ss = []
        for h in range(MLA_HPS):
            hs = slice(h * HP, (h + 1) * HP)
            s = _dot_nt(k_ref[pl.ds(off, nk), hs], q_ref[:, hs])
            if diag_from is not None:
                kpos = lax.broadcasted_iota(jnp.int32, (nk, tq), 0) - diag_from
                qpos = lax.broadcasted_iota(jnp.int32, (nk, tq), 1)
                s = jnp.where(kpos <= qpos, s, NEG)
            ss.append(s)
        ms, alphas, ps = [], [], []
        for h in range(MLA_HPS):
            m = carry[h][0]
            m_new = jnp.maximum(m, jnp.max(ss[h], 0, keepdims=True))
            alphas.append(jnp.exp2(m - m_new))
            ps.append(jnp.exp2(ss[h] - m_new).astype(MXU_DT))
            ms.append(m_new)
        out = []
        for h in range(MLA_HPS):
            vt = vt_ref[h * VROWS:(h + 1) * VROWS, pl.ds(off, nk)]
            out.append((ms[h], alphas[h] * carry[h][1] + _dot(vt, ps[h])))
        return tuple(out)

    def finish(carry):
        for h in range(MLA_HPS):
            acc = carry[h][1]
            o_ref[h * MLA_V:(h + 1) * MLA_V, :] = acc[0:MLA_V, :] / acc[MLA_V:MLA_V + 1, :]

    one = (jnp.full((1, tq), NEG, F32), jnp.zeros((VROWS, tq), F32))
    carry = lax.fori_loop(
        0, qi // 2, lambda kp, c: step(pl.multiple_of(kp * tk2, tk2), tk2, c, None),
        (one,) * MLA_HPS)

    @pl.when(qi % 2 == 1)
    def _():
        finish(step(pl.multiple_of((qi - 1) * tq, tq), tk2, carry, tq))

    @pl.when(qi % 2 == 0)
    def _():
        finish(step(pl.multiple_of(qi * tq, tq), tq, carry, 0))


def _mla_attn(q, k, vt, batch, seq, tq):
    nq = seq // tq
    pw = MLA_HPS * HP
    return pl.pallas_call(
        functools.partial(_mla_attn_body, tq=tq),
        grid=(batch, MLA_HEADS // MLA_HPS, nq),
        in_specs=[pl.BlockSpec((tq, pw), lambda b, h, i: (b * nq + i, h)),
                  pl.BlockSpec((seq, pw), lambda b, h, i: (b, h)),
                  pl.BlockSpec((None, MLA_HPS * VROWS, seq), lambda b, h, i: (b, h, 0))],
        out_specs=pl.BlockSpec((None, MLA_HPS * MLA_V, tq), lambda b, h, i: (b, h, i)),
        out_shape=jax.ShapeDtypeStruct((batch, MLA_HEADS * MLA_V, seq), F32),
        compiler_params=_params(3),
        name="mla_attn",
    )(q, k, vt)


def _compress_body(c_ref, pe_ref, w1_ref, b1_ref, w2_ref, o_ref):
    nch = c_ref.shape[0] // CMP_STRIDE
    u = jnp.zeros((nch, NSA_G * CMP_HIDDEN), F32)
    w = jnp.zeros((nch, NSA_G * CMP_HIDDEN), F32)
    for p in range(CMP_STRIDE):
        x = c_ref[pl.ds(p, nch, stride=CMP_STRIDE), :]
        u = u + _dot((x + pe_ref[p:p + 1, :]).astype(MXU_DT), w1_ref[p])
        w = w + _dot((x + pe_ref[CMP_STRIDE + p:CMP_STRIDE + p + 1, :]).astype(MXU_DT),
                     w1_ref[CMP_STRIDE + p])
    w = pltpu.roll(w, nch - 1, 0)
    hid = jax.nn.gelu(u + w + b1_ref[...])
    o_ref[...] = _dot(hid.astype(MXU_DT), w2_ref[...]).astype(o_ref.dtype)


def _compress(kcvc, pe, w1, b1, w2, batch, seq):
    nch = seq // CMP_STRIDE
    gw = NSA_G * NSA_DH
    gh = NSA_G * CMP_HIDDEN
    return pl.pallas_call(
        _compress_body,
        grid=(2, batch),
        in_specs=[pl.BlockSpec((seq, gw), lambda s, j: (j, s)),
                  pl.BlockSpec((None, CMP_BLOCK, gw), lambda s, j: (s, 0, 0)),
                  pl.BlockSpec((None, CMP_BLOCK, gw, gh), lambda s, j: (s, 0, 0, 0)),
                  pl.BlockSpec((None, 1, gh), lambda s, j: (s, 0, 0)),
                  pl.BlockSpec((None, gh, gw), lambda s, j: (s, 0, 0))],
        out_specs=pl.BlockSpec((None, None, nch, gw), lambda s, j: (s, j, 0, 0)),
        out_shape=jax.ShapeDtypeStruct((2, batch, nch, gw), MXU_DT),
        compiler_params=_params(2),
        name="nsa_compress",
    )(kcvc, pe, w1, b1, w2)


def _head_blockdiag(w):
    eye = jnp.eye(NSA_G, dtype=w.dtype)
    out = jnp.einsum("gh,...ab->...gahb", eye, w)
    return out.reshape(w.shape[:-2] + (NSA_G * w.shape[-2], NSA_G * w.shape[-1]))


_BIG = 1e30


def _nsa_queries(q_ref, slope_ref, g, q0, tq):
    lane = lax.broadcasted_iota(jnp.int32, (tq, HP), 1)
    q0f = q0.astype(F32)
    qs = []
    for r in range(NSA_R):
        slope = slope_ref[g * NSA_R + r]
        c = jnp.where(lane == NSA_DH, slope * POS_SPLIT,
                      jnp.where(lane == NSA_DH + 1, slope,
                                jnp.where(lane == NSA_DH + 2, -slope * q0f, 0.0)))
        qs.append(q_ref[:, r * HP:(r + 1) * HP] + c.astype(MXU_DT))
    return qs


def _cmp_attn_body(slope_ref, q_ref, kc_ref, vct_ref, ovl_ref, pool_ref,
                   oc_ref, sel_ref, flag_ref, s_scr, e_scr, *, tq, n_sel):
    g = pl.program_id(1)
    q0 = pl.program_id(2) * tq
    ns = sel_ref.shape[1]
    n = NSA_R * tq
    ch = min(CMP_CH, kc_ref.shape[0])
    qc = jnp.concatenate(_nsa_queries(q_ref, slope_ref, g, q0, tq), axis=0)
    t_lane = q0 + (lax.broadcasted_iota(jnp.int32, (1, n), 1) & (tq - 1))
    cend0 = lax.broadcasted_iota(jnp.int32, (ch, n), 0) * CMP_STRIDE + (CMP_BLOCK - 1)
    nproc = jnp.maximum(q0 + tq - CMP_BLOCK, 0) // (ch * CMP_STRIDE) + 1

    def scores(c, m):
        off = pl.multiple_of(c * ch, ch)
        s = _dot_nt(kc_ref[pl.ds(off, ch), :], qc)
        s = jnp.where(cend0 <= t_lane - off * CMP_STRIDE, s, NEG)
        s_scr[pl.ds(off, ch), :] = s
        return jnp.maximum(m, jnp.max(s, 0, keepdims=True))

    m = lax.fori_loop(0, nproc, scores, jnp.full((1, n), NEG, F32))

    def weights(c, carry):
        off = pl.multiple_of(c * ch, ch)
        e_scr[pl.ds(off, ch), :] = jnp.exp(s_scr[pl.ds(off, ch), :] - m).astype(MXU_DT)
        return carry

    e_scr[...] = jnp.zeros(e_scr.shape, MXU_DT)
    lax.fori_loop(0, nproc, weights, 0)
    e = e_scr[...]
    acc = _dot(vct_ref[...], e)
    has_key = t_lane >= CMP_BLOCK - 1
    inv = jnp.where(has_key, 1.0 / acc[NSA_DH:NSA_DH + 1, :], 0.0)
    impn = _dot(ovl_ref[...], e) * inv
    imp = impn[:, 0:tq]
    for r in range(NSA_R):
        oc_ref[r] = acc[0:NSA_DH, r * tq:(r + 1) * tq] * inv[:, r * tq:(r + 1) * tq]
        if r:
            imp = imp + impn[:, r * tq:(r + 1) * tq]

    blk = lax.broadcasted_iota(jnp.int32, (ns, tq), 0)
    cur = (q0 + lax.broadcasted_iota(jnp.int32, (ns, tq), 1)) // SEL_BLOCK
    blkf = blk.astype(F32)
    forced = (blk == 0) | (blk == cur) | (blk == cur - 1)
    score0 = jnp.where((blk <= cur) & jnp.logical_not(forced), imp, -1.0)
    score = score0
    for _ in range(n_sel - N_FORCED):
        mx = jnp.max(score, 0, keepdims=True)
        first = jnp.min(jnp.where(score == mx, blkf, float(ns)), 0, keepdims=True)
        score = jnp.where(blkf == first, -1.0, score)
    sel = jnp.where(forced, 1.0, jnp.where(score != score0, 1.0, 0.0))
    sel_ref[...] = sel.T.astype(sel_ref.dtype)
    ones = jnp.ones((8, NSA_TQ), MXU_DT)
    selb = sel.astype(MXU_DT)
    for j in range(tq // NSA_TQ):
        cnt = _dot_nt(ones, selb[:, j * NSA_TQ:(j + 1) * NSA_TQ])
        flag_ref[j] = _dot(cnt.astype(MXU_DT), pool_ref[...])


def _cmp_attn(slopes, nq, kc, vct, ovl, pool, batch, seq, tq):
    nq_t = seq // tq
    nch = kc.shape[-2]
    ns = seq // SEL_BLOCK
    n_sel = min(N_SEL, ns)
    sub = tq // NSA_TQ
    gw = NSA_R * HP
    n = NSA_R * tq
    return pl.pallas_call(
        functools.partial(_cmp_attn_body, tq=tq, n_sel=n_sel),
        grid=(batch, NSA_G, nq_t),
        in_specs=[pl.BlockSpec(memory_space=pltpu.SMEM),
                  pl.BlockSpec((tq, gw), lambda b, g, i: (b * nq_t + i, g)),
                  pl.BlockSpec((None, None, nch, HP), lambda b, g, i: (b, g, 0, 0)),
                  pl.BlockSpec((None, None, VROWS, nch), lambda b, g, i: (b, g, 0, 0)),
                  _const_spec(ovl.shape), _const_spec(pool.shape)],
        out_specs=[pl.BlockSpec((None, None, NSA_R, NSA_DH, tq), lambda b, g, i: (b, g, 0, 0, i)),
                   pl.BlockSpec((None, None, tq, ns), lambda b, g, i: (b, g, i, 0)),
                   pl.BlockSpec((None, None, sub, 8, LANE), lambda b, g, i: (b, g, i, 0, 0))],
        out_shape=[jax.ShapeDtypeStruct((batch, NSA_G, NSA_R, NSA_DH, seq), F32),
                   jax.ShapeDtypeStruct((batch, NSA_G, seq, ns), MXU_DT),
                   jax.ShapeDtypeStruct((batch, NSA_G, seq // NSA_TQ, 8, LANE), F32)],
        scratch_shapes=[pltpu.VMEM((nch, n), F32), pltpu.VMEM((nch, n), MXU_DT)],
        compiler_params=_params(3),
        name="nsa_cmp_attn_topk",
    )(slopes, nq, kc, vct, ovl, pool)


def _nsa_attn_body(flag_ref, slope_ref, q_ref, sel_ref, kaug_ref, vst_ref, kwaug_ref, vwt_ref,
                   oct_ref, gt_ref, o_ref, list_scr, os_scr, ow_scr, *, tq, tk, nkt):
    b = pl.program_id(0)
    g = pl.program_id(1)
    qi = pl.program_id(2)
    q0 = qi * tq
    n = NSA_R * tq
    wk = WINDOW + tq

    selneg = ((sel_ref[...].astype(F32) - 1.0) * _BIG).astype(MXU_DT)
    qs = _nsa_queries(q_ref, slope_ref, g, q0, tq)
    qc = jnp.concatenate(qs, axis=0)
    qa = jnp.concatenate([jnp.concatenate([selneg, qr], axis=1) for qr in qs], axis=0)
    t_loc = lax.broadcasted_iota(jnp.int32, (1, n), 1) & (tq - 1)
    t_lane = q0 + t_loc

    kd = q0 // tk
    fbase = ((b * NSA_G + g) * pl.num_programs(2) + qi) * nkt

    def compact(kt, cnt):
        act = flag_ref[fbase + kt] > 0

        @pl.when(act)
        def _():
            list_scr[cnt] = kt
        return cnt + act.astype(jnp.int32)

    cnt = lax.fori_loop(0, kd, compact, 0)
    for j in range(NSA_U - 1):
        list_scr[cnt + j] = kd

    def tiles(idx):
        ss, offs = [], []
        for i in idx:
            off = pl.multiple_of(i * tk, tk)
            ss.append(_dot_nt(kaug_ref[pl.ds(off, tk), :], qa))
            offs.append(off)
        return ss, offs

    def update(ss, offs, carry):
        m_old, acc = carry
        m_new = m_old
        for s in ss:
            m_new = jnp.maximum(m_new, jnp.max(s, 0, keepdims=True))
        acc = jnp.exp(m_old - m_new) * acc
        for s, off in zip(ss, offs):
            acc = acc + _dot(vst_ref[:, pl.ds(off, tk)], jnp.exp(s - m_new).astype(MXU_DT))
        return m_new, acc

    def group(gi, carry):
        return update(*tiles([list_scr[gi * NSA_U + j] for j in range(NSA_U)]), carry)

    nfull = cnt // NSA_U
    carry = lax.fori_loop(0, nfull, group,
                          (jnp.full((1, n), NEG, F32), jnp.zeros((VROWS, n), F32)))
    k_loc = lax.broadcasted_iota(jnp.int32, (tq, n), 0)

    def tail(start, banded):
        ss, offs = tiles([list_scr[nfull * NSA_U + j] for j in range(NSA_U - 1)] + [kd])
        sw = _dot_nt(kwaug_ref[pl.ds(start, wk), :], qc)
        for j in range(NSA_U - 1):
            ss[j] = jnp.where(nfull * NSA_U + j < cnt, ss[j], NEG)
        kpos = offs[-1] + lax.broadcasted_iota(jnp.int32, (tk, n), 0)
        ss[-1] = jnp.where(kpos <= t_lane, ss[-1], NEG)
        if banded:
            parts = [jnp.where(k_loc > t_loc, sw[0:tq], NEG), sw[tq:WINDOW],
                     jnp.where(k_loc <= t_loc, sw[WINDOW:wk], NEG)]
        else:
            dist = t_lane - (start + lax.broadcasted_iota(jnp.int32, (wk, n), 0))
            parts = [jnp.where((dist >= 0) & (dist < WINDOW), sw, NEG)]
        m_old, acc = carry
        m_new = m_old
        for s in ss:
            m_new = jnp.maximum(m_new, jnp.max(s, 0, keepdims=True))
        mw = jnp.max(parts[0], 0, keepdims=True)
        for p in parts[1:]:
            mw = jnp.maximum(mw, jnp.max(p, 0, keepdims=True))
        ps = [jnp.exp(s - m_new).astype(MXU_DT) for s in ss]
        e = jnp.concatenate([jnp.exp(p - mw).astype(MXU_DT) for p in parts], axis=0)
        acc = jnp.exp(m_old - m_new) * acc
        for p, off in zip(ps, offs):
            acc = acc + _dot(vst_ref[:, pl.ds(off, tk)], p)
        ow = _dot(vwt_ref[:, pl.ds(start, wk)], e)
        os_scr[...] = acc[0:NSA_DH, :] / acc[NSA_DH:NSA_DH + 1, :]
        ow_scr[...] = ow[0:NSA_DH, :] / ow[NSA_DH:NSA_DH + 1, :]

    @pl.when(q0 >= WINDOW)
    def _():
        tail(pl.multiple_of(q0 - WINDOW, LANE), True)

    @pl.when(q0 < WINDOW)
    def _():
        tail(0, False)

    o_sel = os_scr[...]
    o_win = ow_scr[...]

    for r in range(NSA_R):
        ls = slice(r * tq, (r + 1) * tq)
        o_ref[r] = (gt_ref[0, r:r + 1, :] * oct_ref[r] + gt_ref[1, r:r + 1, :] * o_sel[:, ls]
                    + gt_ref[2, r:r + 1, :] * o_win[:, ls])


def _nsa_attn(flags, slopes, nq, sel, kaug, vst, kwaug, vwt, oct_, gt, batch, seq, tq, tk):
    nq_t = seq // tq
    ns = seq // SEL_BLOCK
    nkt = seq // tk
    n = NSA_R * tq
    gw = NSA_R * HP
    bg = lambda b, g, i, f: (b, g, 0, 0)
    grid_spec = pltpu.PrefetchScalarGridSpec(
        num_scalar_prefetch=1,
        grid=(batch, NSA_G, nq_t),
        in_specs=[pl.BlockSpec(memory_space=pltpu.SMEM),
                  pl.BlockSpec((tq, gw), lambda b, g, i, f: (b * nq_t + i, g)),
                  pl.BlockSpec((None, None, tq, ns), lambda b, g, i, f: (b, g, i, 0)),
                  pl.BlockSpec((None, None, seq, ns + HP), bg),
                  pl.BlockSpec((None, None, VROWS, seq), bg),
                  pl.BlockSpec((None, None, seq, HP), bg),
                  pl.BlockSpec((None, None, VROWS, seq), bg),
                  pl.BlockSpec((None, None, NSA_R, NSA_DH, tq), lambda b, g, i, f: (b, g, 0, 0, i)),
                  pl.BlockSpec((None, None, 3, NSA_R, tq), lambda b, g, i, f: (b, g, 0, 0, i))],
        out_specs=pl.BlockSpec((None, None, NSA_R, NSA_DH, tq), lambda b, g, i, f: (b, g, 0, 0, i)),
        scratch_shapes=[pltpu.SMEM((nkt + NSA_U,), jnp.int32),
                        pltpu.VMEM((NSA_DH, n), F32), pltpu.VMEM((NSA_DH, n), F32)])
    return pl.pallas_call(
        functools.partial(_nsa_attn_body, tq=tq, tk=tk, nkt=nkt),
        grid_spec=grid_spec,
        out_shape=jax.ShapeDtypeStruct((batch, NSA_G, NSA_R, NSA_DH, seq), F32),
        compiler_params=_params(3),
        name="nsa_sel_win_attn",
    )(flags, slopes, nq, sel, kaug, vst, kwaug, vwt, oct_, gt)


_TN = (((0,), (0,)), ((), ()))


def _outproj_body(x_ref, oa_ref, ob_ref, wo_ref, g_ref, b_ref, y_ref):
    wa = oa_ref.shape[0]
    mix = lax.dot_general(oa_ref[...].astype(MXU_DT), wo_ref[0:wa, :], _TN,
                          preferred_element_type=F32)
    mix = mix + lax.dot_general(ob_ref[...].astype(MXU_DT), wo_ref[wa:, :], _TN,
                                preferred_element_type=F32)
    y_ref[...] = _ln_rows(ALPHA * x_ref[...] + mix, g_ref[...], b_ref[...])


def _outproj(x2d, oat, obt, wo, g, b, tm):
    batch, _, seq = oat.shape
    ns = seq // tm
    row = lambda bb, i: (bb * ns + i, 0)
    col = lambda bb, i: (bb, 0, i)
    return pl.pallas_call(
        _outproj_body,
        grid=(batch, ns),
        in_specs=[pl.BlockSpec((tm, D_MODEL), row),
                  pl.BlockSpec((None, oat.shape[1], tm), col),
                  pl.BlockSpec((None, obt.shape[1], tm), col),
                  _const_spec(wo.shape),
                  _const_spec((1, D_MODEL)), _const_spec((1, D_MODEL))],
        out_specs=pl.BlockSpec((tm, D_MODEL), row),
        out_shape=jax.ShapeDtypeStruct((batch * seq, D_MODEL), F32),
        compiler_params=_params(2),
        name="out_proj_ln",
    )(x2d, oat, obt, wo, g.reshape(1, -1), b.reshape(1, -1))


def _xattn_body(x_ref, wq_ref, kv_ref, wo_ref, g_ref, b_ref, y_ref):
    x = x_ref[...]
    q = _dot(x.astype(MXU_DT), wq_ref[...]) * (XA_DH ** -0.5)
    hw = XA_HEADS * XA_DH
    y = jnp.zeros_like(x)
    for h in range(XA_HEADS):
        sl = slice(h * XA_DH, (h + 1) * XA_DH)
        s = _dot_nt(q[:, sl].astype(MXU_DT), kv_ref[:, sl])
        e = jnp.exp(s - jnp.max(s, -1, keepdims=True))
        p = e / jnp.sum(e, -1, keepdims=True)
        o = _dot(p.astype(MXU_DT), kv_ref[:, hw + h * XA_DH:hw + (h + 1) * XA_DH])
        y = y + _dot(o.astype(MXU_DT), wo_ref[sl, :])
    y_ref[...] = _ln_rows(ALPHA * x + y, g_ref[...], b_ref[...])


def _xattn(x2d, wq, memkv, wo, g, b, batch, seq, tm):
    ns = seq // tm
    mem = memkv.shape[0] // batch
    row = lambda bb, i: (bb * ns + i, 0)
    return pl.pallas_call(
        _xattn_body,
        grid=(batch, ns),
        in_specs=[pl.BlockSpec((tm, D_MODEL), row),
                  _const_spec(wq.shape),
                  pl.BlockSpec((mem, memkv.shape[1]), lambda bb, i: (bb, 0)),
                  _const_spec(wo.shape),
                  _const_spec((1, D_MODEL)), _const_spec((1, D_MODEL))],
        out_specs=pl.BlockSpec((tm, D_MODEL), row),
        out_shape=jax.ShapeDtypeStruct((batch * seq, D_MODEL), F32),
        compiler_params=_params(2),
        name="mem_xattn_ln",
    )(x2d, wq, memkv, wo, g.reshape(1, -1), b.reshape(1, -1))


_FF_CHUNK = D_FF
_HALO = 8


def _ffn_body(x_ref, xp_ref, wup_ref, cw_ref, cb_ref, wdn_ref, g_ref, b_ref, y_ref):
    first = pl.program_id(1) == 0
    x = x_ref[...]
    xp = jnp.where(first, 0.0, xp_ref[...])
    xe = jnp.concatenate([xp, x], axis=0).astype(MXU_DT)

    def conv(c0):
        u = _dot(xe, wup_ref[:, c0:c0 + _FF_CHUNK])
        u1 = pltpu.roll(u, 1, 0)
        u2 = pltpu.roll(u, 2, 0)
        cw = cw_ref[:, c0:c0 + _FF_CHUNK]
        r = u * cw[2:3, :] + u1 * cw[1:2, :] + u2 * cw[0:1, :] + cb_ref[:, c0:c0 + _FF_CHUNK]
        return r[_HALO:, :]

    acc = jnp.zeros_like(x)
    for c in range(D_FF // _FF_CHUNK):
        a = conv(c * _FF_CHUNK)
        gg = conv(D_FF + c * _FF_CHUNK)
        act = (a * jax.nn.silu(gg)).astype(MXU_DT)
        acc = acc + _dot(act, wdn_ref[c * _FF_CHUNK:(c + 1) * _FF_CHUNK, :])
    y_ref[...] = _ln_rows(ALPHA * x + acc, g_ref[...], b_ref[...])


def _ffn(x2d, wup, cw, cb, wdn, g, b, batch, seq, tm):
    ns = seq // tm
    hb = tm // _HALO
    row = lambda bb, i: (bb * ns + i, 0)
    prev = lambda bb, i: (jnp.maximum((bb * ns + i) * hb - 1, 0), 0)
    return pl.pallas_call(
        _ffn_body,
        grid=(batch, ns),
        in_specs=[pl.BlockSpec((tm, D_MODEL), row),
                  pl.BlockSpec((_HALO, D_MODEL), prev),
                  _const_spec(wup.shape), _const_spec(cw.shape), _const_spec(cb.shape),
                  _const_spec(wdn.shape),
                  _const_spec((1, D_MODEL)), _const_spec((1, D_MODEL))],
        out_specs=pl.BlockSpec((tm, D_MODEL), row),
        out_shape=jax.ShapeDtypeStruct((batch * seq, D_MODEL), F32),
        compiler_params=_params(2),
        name="conv_ffn_ln",
    )(x2d, x2d, wup, cw, cb, wdn, g.reshape(1, -1), b.reshape(1, -1))


def _rot_half_cols(w):
    half = MLA_ROPE // 2
    return jnp.concatenate([-w[..., half:], w[..., :half]], axis=-1)


def _prep_layer(w_in, cq_g, w_uq, ckv_g, w_ukv):
    d = w_in.shape[0]
    z = lambda n: jnp.zeros((d, n), F32)
    kr = w_in[:, _O_KR:_O_NQ]
    ngl = _IN_COLS - _O_GL
    w_ext = jnp.concatenate([
        w_in[:, _O_CQ:_O_CKV], w_in[:, _O_CKV:_O_KR],
        w_in[:, _O_NQ:_O_KV6] * (NSA_DH ** -0.5),
        w_in[:, _O_KV6:_O_GL],
        kr, w_in[:, _O_GL:_IN_COLS], z(_M_KROT - _M_GL - ngl),
        _rot_half_cols(kr), z(HP - _M_KROT - MLA_ROPE)], axis=1)
    wq = w_uq.reshape(MLA_Q_RANK, MLA_HEADS, MLA_NOPE + MLA_ROPE)
    zq = jnp.zeros((MLA_Q_RANK, MLA_HEADS, HP - MLA_NOPE - MLA_ROPE), F32)
    wqa = jnp.concatenate([wq, zq], axis=-1).reshape(MLA_Q_RANK, MLA_HEADS * HP)
    wqb = jnp.concatenate([jnp.zeros_like(wq[..., :MLA_NOPE]), _rot_half_cols(wq[..., MLA_NOPE:]), zq],
                          axis=-1).reshape(MLA_Q_RANK, MLA_HEADS * HP)
    wkv = w_ukv.reshape(MLA_KV_RANK, MLA_HEADS, MLA_NOPE + MLA_V)
    zk = jnp.zeros((MLA_KV_RANK, MLA_HEADS, HP - MLA_NOPE), F32)
    wkk = jnp.concatenate([wkv[..., :MLA_NOPE], zk], axis=-1).reshape(MLA_KV_RANK, MLA_HEADS * HP)
    wvv = wkv[..., MLA_NOPE:].reshape(MLA_KV_RANK, MLA_HEADS * MLA_V)
    return dict(w_in=w_ext.astype(MXU_DT), cq_g=cq_g.reshape(1, -1), ckv_g=ckv_g.reshape(1, -1),
                wqa=wqa.astype(MXU_DT), wqb=wqb.astype(MXU_DT),
                wkk=wkk.astype(MXU_DT), wkv=wvv.astype(MXU_DT))


def _rope_tabs(seq):
    inv = ROPE_BASE ** (-jnp.arange(0, MLA_ROPE, 2, dtype=F32) / MLA_ROPE)
    ang = jnp.arange(seq, dtype=F32)[:, None] * inv[None, :]
    cos, sin = jnp.cos(ang), jnp.sin(ang)
    pad = jnp.zeros((seq, HP - MLA_NOPE - MLA_ROPE), F32)
    tc = jnp.concatenate([jnp.ones((seq, MLA_NOPE), F32), cos, cos, pad], axis=1)
    ts = jnp.concatenate([jnp.zeros((seq, MLA_NOPE), F32), sin, sin, pad], axis=1)
    scale = (MLA_NOPE + MLA_ROPE) ** -0.5
    return dict(cq=tc * (scale * LOG2E), sq=ts * (scale * LOG2E), ck=tc, sk=ts)


def _pos_cols(pos, width, first):
    lane = jnp.arange(width)[None, :]
    p = pos[:, None]
    return jnp.where(lane == first, p // POS_SPLIT,
                     jnp.where(lane == first + 1, p % POS_SPLIT,
                               jnp.where(lane == first + 2, 1, 0))).astype(F32)


def _ones_row(n):
    return (jnp.arange(VROWS - NSA_DH)[:, None] == jnp.zeros((1, n), jnp.int32)).astype(MXU_DT)


def _nsa_consts(seq):
    nch = seq // CMP_STRIDE
    ns = seq // SEL_BLOCK
    c_start = jnp.arange(nch) * CMP_STRIDE
    s_start = jnp.arange(ns) * SEL_BLOCK
    ovl = jnp.clip(jnp.minimum(c_start[None, :] + CMP_BLOCK, s_start[:, None] + SEL_BLOCK)
                   - jnp.maximum(c_start[None, :], s_start[:, None]), 0).astype(F32) / CMP_BLOCK
    pos = jnp.arange(seq)
    onehot = (pos[:, None] // SEL_BLOCK == jnp.arange(ns)[None, :]).astype(MXU_DT)
    pool = (jnp.arange(ns)[:, None] // (NSA_TK // SEL_BLOCK) == jnp.arange(LANE)[None, :])
    slopes = 2.0 ** (-8.0 * jnp.arange(1, NSA_HEADS + 1, dtype=F32) / NSA_HEADS)
    return dict(ovl=ovl.astype(MXU_DT),
                onehot=onehot,
                posc=_pos_cols(pos, HP, NSA_DH),
                cendc=_pos_cols(c_start + CMP_BLOCK - 1, HP - NSA_DH, 0).astype(MXU_DT),
                pool=pool.astype(MXU_DT), slopes=slopes)


def kernel(x, mem, ln_in_g, ln_in_b, ln_mem_g, ln_mem_b, w_in, mla_cq_g, mla_w_uq, mla_ckv_g, mla_w_ukv, nsa_pe_k, nsa_w1_k, nsa_b1_k, nsa_w2_k, nsa_pe_v, nsa_w1_v, nsa_b1_v, nsa_w2_v, w_out, ln1_g, ln1_b, xa_wq, xa_wkv, xa_wo, ln2_g, ln2_b, ffn_w_up, ffn_conv_w, ffn_conv_b, ffn_w_down, ln3_g, ln3_b):
    batch, seq, d = x.shape
    t = batch * seq
    depth = w_in.shape[0]
    tm = min(TM, seq)
    nsa_tq, nsa_tk = min(NSA_TQ, seq), min(NSA_TK, seq)
    tabs = _rope_tabs(seq)
    consts = _nsa_consts(seq)
    nch = seq // CMP_STRIDE
    nkt = seq // nsa_tk
    bg = (batch, NSA_G)
    bcast = lambda a: jnp.broadcast_to(a, bg + a.shape)

    xs = _layer_norm(x.reshape(t, d), ln_in_g, ln_in_b, tm)
    memn = _layer_norm(mem.reshape(-1, d), ln_mem_g, ln_mem_b, mem.shape[0] * mem.shape[1])

    for l in range(depth):
        lw = _prep_layer(w_in[l], mla_cq_g[l], mla_w_uq[l], mla_ckv_g[l], mla_w_ukv[l])
        q, k, vt, nq, kcvc, kaug, vst, kwaug, vwt, gates = _inproj(xs, lw, tabs, consts, seq, tm)

        oa = _mla_attn(q, k, vt, batch, seq, min(MLA_T, seq))

        pe = jnp.tile(jnp.stack([nsa_pe_k[l], nsa_pe_v[l]]), (1, 1, NSA_G))
        w1 = jnp.stack([nsa_w1_k[l], nsa_w1_v[l]]).reshape(2, CMP_BLOCK, NSA_DH, CMP_HIDDEN)
        w1 = _head_blockdiag(w1).astype(MXU_DT)
        b1 = jnp.tile(jnp.stack([nsa_b1_k[l], nsa_b1_v[l]]).reshape(2, 1, CMP_HIDDEN), (1, 1, NSA_G))
        w2 = _head_blockdiag(jnp.stack([nsa_w2_k[l], nsa_w2_v[l]])).astype(MXU_DT)
        cmp = _compress(kcvc, pe, w1, b1, w2, batch, seq)
        cmp = cmp.reshape(2, batch, nch, NSA_G, NSA_DH).transpose(0, 1, 3, 2, 4)
        kcaug = jnp.concatenate([cmp[0], bcast(consts["cendc"])], axis=-1)
        vct = jnp.concatenate([cmp[1].transpose(0, 1, 3, 2), bcast(_ones_row(nch))], axis=2)
        oc_t, sel, cnt = _cmp_attn(consts["slopes"], nq, kcaug, vct, consts["ovl"], consts["pool"],
                                   batch, seq, min(CMP_TQ, seq))
        flags = (cnt[:, :, :, 0, :nkt] > 0).astype(jnp.int32).reshape(-1)
        gt = gates[:, _M_GL:_M_GL + 3 * NSA_HEADS].reshape(batch, seq, NSA_G, NSA_R, 3)
        gt = gt.transpose(0, 2, 4, 3, 1)
        ob = _nsa_attn(flags, consts["slopes"], nq, sel, kaug, vst, kwaug, vwt, oc_t, gt,
                       batch, seq, nsa_tq, nsa_tk)
        ob = ob.reshape(batch, NSA_HEADS * NSA_DH, seq)

        xs = _outproj(xs, oa, ob, w_out[l].astype(MXU_DT), ln1_g[l], ln1_b[l], tm)

        memkv = _matmul(memn, xa_wkv[l].astype(MXU_DT), 512, MXU_DT)
        xs = _xattn(xs, xa_wq[l].astype(MXU_DT), memkv, xa_wo[l].astype(MXU_DT),
                    ln2_g[l], ln2_b[l], batch, seq, tm)

        xs = _ffn(xs, ffn_w_up[l].astype(MXU_DT), ffn_conv_w[l], ffn_conv_b[l].reshape(1, -1),
                  ffn_w_down[l].astype(MXU_DT), ln3_g[l], ln3_b[l], batch, seq, tm)

    return xs.reshape(batch, seq, d)
```

```python
import functools

import jax
import jax.numpy as jnp
from jax import lax
from jax.experimental import pallas as pl
from jax.experimental.pallas import tpu as pltpu

F32 = jnp.float32
MXU_DT = jnp.bfloat16

D_MODEL = 1024
MLA_HEADS = 8
MLA_NOPE = 64
MLA_ROPE = 32
MLA_V = 64
MLA_Q_RANK = 384
MLA_KV_RANK = 256
ROPE_BASE = 10000.0
NSA_HEADS = 8
NSA_G = 2
NSA_R = 4
NSA_DH = 64
CMP_STRIDE = 16
CMP_BLOCK = 32
CMP_HIDDEN = 128
SEL_BLOCK = 64
N_SEL = 16
WINDOW = 512
XA_HEADS = 4
XA_DH = 256
D_FF = 2816
DEPTH = 2
ALPHA = (2.0 * DEPTH) ** 0.25
LN_EPS = 1e-5
RMS_EPS = 1e-6
NEG = -1e30

LANE = 128
HP = 128
VMEM_LIMIT = 56 * 1024 * 1024

TM = 512
MLA_T = 512
MLA_HPS = 4
CMP_TQ = 512
CMP_CH = 128
NSA_TQ = 256
NSA_TK = 256
NSA_U = 2
POS_SPLIT = 128
VROWS = 80
N_FORCED = 3
LOG2E = 1.4426950408889634

_O_CQ, _O_CKV, _O_KR, _O_NQ = 0, 384, 640, 672
_O_KV6, _O_GL, _IN_COLS = 1184, 1952, 1976
_E_CQ, _E_CKV, _E_NQ, _E_KCVC, _E_KS, _E_VS, _E_KW, _E_VW, _E_MISC, _E_END = (
    0, 384, 640, 1152, 1408, 1536, 1664, 1792, 1920, 2048)
_M_KR, _M_GL, _M_KROT = 0, 32, 64

_NT = (((1,), (1,)), ((), ()))


def _params(n_axes=1):
    return pltpu.CompilerParams(
        dimension_semantics=("arbitrary",) * n_axes,
        vmem_limit_bytes=VMEM_LIMIT)


def _const_spec(shape):
    nd = len(shape)
    return pl.BlockSpec(shape, lambda *_: (0,) * nd)


def _ln_rows(v, g, b):
    mu = jnp.mean(v, -1, keepdims=True)
    d = v - mu
    var = jnp.mean(d * d, -1, keepdims=True)
    return d * lax.rsqrt(var + LN_EPS) * g + b


def _dot(a, b):
    return jnp.dot(a, b, preferred_element_type=F32)


def _dot_nt(a, b):
    return lax.dot_general(a, b, _NT, preferred_element_type=F32)


def _ln_body(x_ref, g_ref, b_ref, o_ref):
    o_ref[...] = _ln_rows(x_ref[...], g_ref[...], b_ref[...])


def _layer_norm(x2d, g, b, tm):
    t, d = x2d.shape
    return pl.pallas_call(
        _ln_body,
        grid=(t // tm,),
        in_specs=[pl.BlockSpec((tm, d), lambda i: (i, 0)),
                  _const_spec((1, d)), _const_spec((1, d))],
        out_specs=pl.BlockSpec((tm, d), lambda i: (i, 0)),
        out_shape=jax.ShapeDtypeStruct((t, d), F32),
        compiler_params=_params(1),
        name="ln_rows",
    )(x2d, g.reshape(1, d), b.reshape(1, d))


def _matmul_body(a_ref, w_ref, o_ref):
    o_ref[...] = _dot(a_ref[...].astype(MXU_DT), w_ref[...]).astype(o_ref.dtype)


def _matmul(a, w, tn, out_dtype):
    m, k = a.shape
    n = w.shape[1]
    return pl.pallas_call(
        _matmul_body,
        grid=(n // tn,),
        in_specs=[_const_spec((m, k)), pl.BlockSpec((k, tn), lambda j: (0, j))],
        out_specs=pl.BlockSpec((m, tn), lambda j: (0, j)),
        out_shape=jax.ShapeDtypeStruct((m, n), out_dtype),
        compiler_params=_params(1),
        name="mem_kv_proj",
    )(a, w)


def _rms_rows(v, g):
    return v * lax.rsqrt(jnp.mean(v * v, -1, keepdims=True) + RMS_EPS) * g


def _inproj_body(x_ref, win_ref, cqg_ref, ckvg_ref, wqa_ref, wqb_ref, wkk_ref, wkv_ref,
                 tcq_ref, tsq_ref, tck_ref, tsk_ref, onehot_ref, posc_ref,
                 q_ref, k_ref, vt_ref, nq_ref, kcvc_ref, kaug_ref, vst_ref, kwaug_ref, vwt_ref,
                 gate_ref):
    xb = x_ref[...].astype(MXU_DT)
    h = _dot(xb, win_ref[...])
    cqn = _rms_rows(h[:, _E_CQ:_E_CKV], cqg_ref[...]).astype(MXU_DT)
    qa = _dot(cqn, wqa_ref[...])
    qb = _dot(cqn, wqb_ref[...])
    ckvn = _rms_rows(h[:, _E_CKV:_E_NQ], ckvg_ref[...]).astype(MXU_DT)
    kn = _dot(ckvn, wkk_ref[...])
    tail = jnp.where(lax.broadcasted_iota(jnp.int32, (VROWS - MLA_V, h.shape[0]), 0) == 0,
                     1.0, 0.0).astype(MXU_DT)
    vt = _dot(ckvn, wkv_ref[...]).T.astype(vt_ref.dtype)
    for hh in range(MLA_HEADS):
        vt_ref[hh * VROWS:hh * VROWS + MLA_V, :] = vt[hh * MLA_V:(hh + 1) * MLA_V, :]
        vt_ref[hh * VROWS + MLA_V:(hh + 1) * VROWS, :] = tail
    ns = onehot_ref.shape[1]
    posc = posc_ref[...]
    lane = lax.broadcasted_iota(jnp.int32, (h.shape[0], HP), 1)
    low = lane < NSA_DH

    def halves(blk):
        return jnp.where(low, blk, 0.0), jnp.where(low, pltpu.roll(blk, NSA_DH, 1), 0.0)

    ks = halves(h[:, _E_KS:_E_VS])
    kw = halves(h[:, _E_KW:_E_VW])
    vst = h[:, _E_VS:_E_KW].T.astype(MXU_DT)
    vwt = h[:, _E_VW:_E_MISC].T.astype(MXU_DT)
    for g in range(NSA_G):
        kaug_ref[g, :, 0:ns] = onehot_ref[...]
        kaug_ref[g, :, ns:ns + HP] = (ks[g] + posc).astype(MXU_DT)
        kwaug_ref[g] = (kw[g] + posc).astype(MXU_DT)
        for t_ref, tv in ((vst_ref, vst), (vwt_ref, vwt)):
            t_ref[g, 0:NSA_DH, :] = tv[g * NSA_DH:(g + 1) * NSA_DH, :]
            t_ref[g, NSA_DH:VROWS, :] = tail
    for j in range(NSA_HEADS // 2):
        pair = halves(h[:, _E_NQ + j * HP:_E_NQ + (j + 1) * HP])
        for i in range(2):
            nq_ref[:, (2 * j + i) * HP:(2 * j + i + 1) * HP] = pair[i].astype(nq_ref.dtype)
    misc = h[:, _E_MISC:_E_END]
    krr = (jnp.where(low, 0.0, pltpu.roll(misc, _M_KROT - _M_KR, 1)) * tck_ref[...]
           + misc * tsk_ref[...])
    tcq = tcq_ref[...]
    tsq = tsq_ref[...]
    for hh in range(MLA_HEADS):
        sl = slice(hh * HP, (hh + 1) * HP)
        q_ref[:, sl] = (qa[:, sl] * tcq + qb[:, sl] * tsq).astype(q_ref.dtype)
        k_ref[:, sl] = (kn[:, sl] + krr).astype(k_ref.dtype)
    kcvc_ref[...] = h[:, _E_KCVC:_E_KS]
    gate_ref[...] = jax.nn.sigmoid(misc)


def _inproj(x2d, lw, tabs, consts, seq, tm):
    t = x2d.shape[0]
    batch = t // seq
    npos = seq // tm
    ns = seq // SEL_BLOCK
    row = lambda i: (i, 0)
    pos = lambda i: (i % npos, 0)
    tok = lambda i: (i // npos, 0, i % npos, 0)
    ttr = lambda i: (i // npos, 0, 0, i % npos)
    hw = MLA_HEADS * HP
    vrows = MLA_HEADS * VROWS
    sds = jax.ShapeDtypeStruct
    outs = [
        (sds((t, hw), MXU_DT), pl.BlockSpec((tm, hw), row)),
        (sds((t, hw), MXU_DT), pl.BlockSpec((tm, hw), row)),
        (sds((batch, vrows, seq), MXU_DT),
         pl.BlockSpec((None, vrows, tm), lambda i: (i // npos, 0, i % npos))),
        (sds((t, NSA_HEADS * HP), MXU_DT), pl.BlockSpec((tm, NSA_HEADS * HP), row)),
        (sds((t, 256), F32), pl.BlockSpec((tm, 256), row)),
        (sds((batch, NSA_G, seq, ns + HP), MXU_DT), pl.BlockSpec((None, NSA_G, tm, ns + HP), tok)),
        (sds((batch, NSA_G, VROWS, seq), MXU_DT), pl.BlockSpec((None, NSA_G, VROWS, tm), ttr)),
        (sds((batch, NSA_G, seq, HP), MXU_DT), pl.BlockSpec((None, NSA_G, tm, HP), tok)),
        (sds((batch, NSA_G, VROWS, seq), MXU_DT), pl.BlockSpec((None, NSA_G, VROWS, tm), ttr)),
        (sds((t, LANE), F32), pl.BlockSpec((tm, LANE), row)),
    ]
    out_shape = [o[0] for o in outs]
    out_specs = [o[1] for o in outs]
    return pl.pallas_call(
        _inproj_body,
        grid=(t // tm,),
        in_specs=[pl.BlockSpec((tm, D_MODEL), row),
                  _const_spec((D_MODEL, _E_END)),
                  _const_spec((1, MLA_Q_RANK)), _const_spec((1, MLA_KV_RANK)),
                  _const_spec((MLA_Q_RANK, hw)), _const_spec((MLA_Q_RANK, hw)),
                  _const_spec((MLA_KV_RANK, hw)), _const_spec((MLA_KV_RANK, MLA_HEADS * MLA_V)),
                  pl.BlockSpec((tm, HP), pos), pl.BlockSpec((tm, HP), pos),
                  pl.BlockSpec((tm, HP), pos), pl.BlockSpec((tm, HP), pos),
                  pl.BlockSpec((tm, ns), pos), pl.BlockSpec((tm, HP), pos)],
        out_specs=out_specs,
        out_shape=out_shape,
        compiler_params=_params(1),
        name="in_proj",
    )(x2d, lw["w_in"], lw["cq_g"], lw["ckv_g"], lw["wqa"], lw["wqb"], lw["wkk"], lw["wkv"],
      tabs["cq"], tabs["sq"], tabs["ck"], tabs["sk"], consts["onehot"], consts["posc"])


def _mla_attn_body(q_ref, k_ref, vt_ref, o_ref, *, tq):
    qi = pl.program_id(2)

    tk2 = 2 * tq

    def step(off, nk, carry, diag_from):
        ss = []
        for h in range(MLA_HPS):
            hs = slice(h * HP, (h + 1) * HP)
            s = _dot_nt(k_ref[pl.ds(off, nk), hs], q_ref[:, hs])
            if diag_from is not None:
                kpos = lax.broadcasted_iota(jnp.int32, (nk, tq), 0) - diag_from
                qpos = lax.broadcasted_iota(jnp.int32, (nk, tq), 1)
                s = jnp.where(kpos <= qpos, s, NEG)
            ss.append(s)
        ms, alphas, ps = [], [], []
        for h in range(MLA_HPS):
            m = carry[h][0]
            m_new = jnp.maximum(m, jnp.max(ss[h], 0, keepdims=True))
            alphas.append(jnp.exp2(m - m_new))
            ps.append(jnp.exp2(ss[h] - m_new).astype(MXU_DT))
            ms.append(m_new)
        out = []
        for h in range(MLA_HPS):
            vt = vt_ref[h * VROWS:(h + 1) * VROWS, pl.ds(off, nk)]
            out.append((ms[h], alphas[h] * carry[h][1] + _dot(vt, ps[h])))
        return tuple(out)

    def finish(carry):
        for h in range(MLA_HPS):
            acc = carry[h][1]
            o_ref[h * MLA_V:(h + 1) * MLA_V, :] = (
                acc[0:MLA_V, :] / acc[MLA_V:MLA_V + 1, :]).astype(o_ref.dtype)

    one = (jnp.full((1, tq), NEG, F32), jnp.zeros((VROWS, tq), F32))
    carry = lax.fori_loop(
        0, qi // 2, lambda kp, c: step(pl.multiple_of(kp * tk2, tk2), tk2, c, None),
        (one,) * MLA_HPS)

    @pl.when(qi % 2 == 1)
    def _():
        finish(step(pl.multiple_of((qi - 1) * tq, tq), tk2, carry, tq))

    @pl.when(qi % 2 == 0)
    def _():
        finish(step(pl.multiple_of(qi * tq, tq), tq, carry, 0))


def _mla_attn(q, k, vt, batch, seq, tq):
    nq = seq // tq
    pw = MLA_HPS * HP
    return pl.pallas_call(
        functools.partial(_mla_attn_body, tq=tq),
        grid=(batch, MLA_HEADS // MLA_HPS, nq),
        in_specs=[pl.BlockSpec((tq, pw), lambda b, h, i: (b * nq + i, h)),
                  pl.BlockSpec((seq, pw), lambda b, h, i: (b, h)),
                  pl.BlockSpec((None, MLA_HPS * VROWS, seq), lambda b, h, i: (b, h, 0))],
        out_specs=pl.BlockSpec((None, MLA_HPS * MLA_V, tq), lambda b, h, i: (b, h, i)),
        out_shape=jax.ShapeDtypeStruct((batch, MLA_HEADS * MLA_V, seq), MXU_DT),
        compiler_params=_params(3),
        name="mla_attn",
    )(q, k, vt)


def _compress_body(c_ref, pe_ref, w1_ref, b1_ref, w2_ref, o_ref):
    nch = c_ref.shape[0] // CMP_STRIDE
    u = jnp.zeros((nch, NSA_G * CMP_HIDDEN), F32)
    w = jnp.zeros((nch, NSA_G * CMP_HIDDEN), F32)
    for p in range(CMP_STRIDE):
        x = c_ref[pl.ds(p, nch, stride=CMP_STRIDE), :]
        u = u + _dot((x + pe_ref[p:p + 1, :]).astype(MXU_DT), w1_ref[p])
        w = w + _dot((x + pe_ref[CMP_STRIDE + p:CMP_STRIDE + p + 1, :]).astype(MXU_DT),
                     w1_ref[CMP_STRIDE + p])
    w = pltpu.roll(w, nch - 1, 0)
    hid = jax.nn.gelu(u + w + b1_ref[...])
    o_ref[...] = _dot(hid.astype(MXU_DT), w2_ref[...]).astype(o_ref.dtype)


def _compress(kcvc, pe, w1, b1, w2, batch, seq):
    nch = seq // CMP_STRIDE
    gw = NSA_G * NSA_DH
    gh = NSA_G * CMP_HIDDEN
    return pl.pallas_call(
        _compress_body,
        grid=(2, batch),
        in_specs=[pl.BlockSpec((seq, gw), lambda s, j: (j, s)),
                  pl.BlockSpec((None, CMP_BLOCK, gw), lambda s, j: (s, 0, 0)),
                  pl.BlockSpec((None, CMP_BLOCK, gw, gh), lambda s, j: (s, 0, 0, 0)),
                  pl.BlockSpec((None, 1, gh), lambda s, j: (s, 0, 0)),
                  pl.BlockSpec((None, gh, gw), lambda s, j: (s, 0, 0))],
        out_specs=pl.BlockSpec((None, None, nch, gw), lambda s, j: (s, j, 0, 0)),
        out_shape=jax.ShapeDtypeStruct((2, batch, nch, gw), MXU_DT),
        compiler_params=_params(2),
        name="nsa_compress",
    )(kcvc, pe, w1, b1, w2)


def _head_blockdiag(w):
    eye = jnp.eye(NSA_G, dtype=w.dtype)
    out = jnp.einsum("gh,...ab->...gahb", eye, w)
    return out.reshape(w.shape[:-2] + (NSA_G * w.shape[-2], NSA_G * w.shape[-1]))


_BIG = 1e30


def _nsa_queries(q_ref, slope_ref, g, q0, tq):
    lane = lax.broadcasted_iota(jnp.int32, (tq, HP), 1)
    q0f = q0.astype(F32)
    qs = []
    for r in range(NSA_R):
        slope = slope_ref[g * NSA_R + r]
        c = jnp.where(lane == NSA_DH, slope * POS_SPLIT,
                      jnp.where(lane == NSA_DH + 1, slope,
                                jnp.where(lane == NSA_DH + 2, -slope * q0f, 0.0)))
        qs.append(q_ref[:, r * HP:(r + 1) * HP] + c.astype(MXU_DT))
    return qs


def _cmp_attn_body(slope_ref, q_ref, kc_ref, vct_ref, ovl_ref, pool_ref,
                   oc_ref, sel_ref, flag_ref, s_scr, e_scr, *, tq, n_sel):
    g = pl.program_id(1)
    q0 = pl.program_id(2) * tq
    ns = sel_ref.shape[1]
    n = NSA_R * tq
    ch = min(CMP_CH, kc_ref.shape[0])
    qc = jnp.concatenate(_nsa_queries(q_ref, slope_ref, g, q0, tq), axis=0)
    t_lane = q0 + (lax.broadcasted_iota(jnp.int32, (1, n), 1) & (tq - 1))
    cend0 = lax.broadcasted_iota(jnp.int32, (ch, n), 0) * CMP_STRIDE + (CMP_BLOCK - 1)
    nproc = jnp.maximum(q0 + tq - CMP_BLOCK, 0) // (ch * CMP_STRIDE) + 1

    def scores(c, m):
        off = pl.multiple_of(c * ch, ch)
        s = _dot_nt(kc_ref[pl.ds(off, ch), :], qc)
        s = jnp.where(cend0 <= t_lane - off * CMP_STRIDE, s, NEG)
        s_scr[pl.ds(off, ch), :] = s
        return jnp.maximum(m, jnp.max(s, 0, keepdims=True))

    m = lax.fori_loop(0, nproc, scores, jnp.full((1, n), NEG, F32))

    def weights(c, carry):
        off = pl.multiple_of(c * ch, ch)
        e_scr[pl.ds(off, ch), :] = jnp.exp(s_scr[pl.ds(off, ch), :] - m).astype(MXU_DT)
        return carry

    def clear(c, carry):
        e_scr[pl.ds(pl.multiple_of(c * ch, ch), ch), :] = jnp.zeros((ch, n), MXU_DT)
        return carry

    lax.fori_loop(0, nproc, weights, 0)
    lax.fori_loop(nproc, kc_ref.shape[0] // ch, clear, 0)
    e = e_scr[...]
    acc = _dot(vct_ref[...], e)
    has_key = t_lane >= CMP_BLOCK - 1
    inv = jnp.where(has_key, 1.0 / acc[NSA_DH:NSA_DH + 1, :], 0.0)
    impn = _dot(ovl_ref[...], e) * inv
    imp = impn[:, 0:tq]
    for r in range(NSA_R):
        oc_ref[r] = acc[0:NSA_DH, r * tq:(r + 1) * tq] * inv[:, r * tq:(r + 1) * tq]
        if r:
            imp = imp + impn[:, r * tq:(r + 1) * tq]

    blk = lax.broadcasted_iota(jnp.int32, (ns, tq), 0)
    cur = (q0 + lax.broadcasted_iota(jnp.int32, (ns, tq), 1)) // SEL_BLOCK
    blkf = blk.astype(F32)
    forced = (blk == 0) | (blk == cur) | (blk == cur - 1)
    score0 = jnp.where((blk <= cur) & jnp.logical_not(forced), imp, -1.0)
    score = score0
    for _ in range(n_sel - N_FORCED):
        mx = jnp.max(score, 0, keepdims=True)
        first = jnp.min(jnp.where(score == mx, blkf, float(ns)), 0, keepdims=True)
        score = jnp.where(blkf == first, -1.0, score)
    sel = jnp.where(forced, 1.0, jnp.where(score != score0, 1.0, 0.0))
    sel_ref[...] = sel.T.astype(sel_ref.dtype)
    ones = jnp.ones((8, NSA_TQ), MXU_DT)
    selb = sel.astype(MXU_DT)
    for j in range(tq // NSA_TQ):
        cnt = _dot_nt(ones, selb[:, j * NSA_TQ:(j + 1) * NSA_TQ])
        flag_ref[j] = _dot(cnt.astype(MXU_DT), pool_ref[...])


def _cmp_attn(slopes, nq, kc, vct, ovl, pool, batch, seq, tq):
    nq_t = seq // tq
    nch = kc.shape[-2]
    ns = seq // SEL_BLOCK
    n_sel = min(N_SEL, ns)
    sub = tq // NSA_TQ
    gw = NSA_R * HP
    n = NSA_R * tq
    return pl.pallas_call(
        functools.partial(_cmp_attn_body, tq=tq, n_sel=n_sel),
        grid=(batch, NSA_G, nq_t),
        in_specs=[pl.BlockSpec(memory_space=pltpu.SMEM),
                  pl.BlockSpec((tq, gw), lambda b, g, i: (b * nq_t + i, g)),
                  pl.BlockSpec((None, None, nch, HP), lambda b, g, i: (b, g, 0, 0)),
                  pl.BlockSpec((None, None, VROWS, nch), lambda b, g, i: (b, g, 0, 0)),
                  _const_spec(ovl.shape), _const_spec(pool.shape)],
        out_specs=[pl.BlockSpec((None, None, NSA_R, NSA_DH, tq), lambda b, g, i: (b, g, 0, 0, i)),
                   pl.BlockSpec((None, None, tq, ns), lambda b, g, i: (b, g, i, 0)),
                   pl.BlockSpec((None, None, sub, 8, LANE), lambda b, g, i: (b, g, i, 0, 0))],
        out_shape=[jax.ShapeDtypeStruct((batch, NSA_G, NSA_R, NSA_DH, seq), F32),
                   jax.ShapeDtypeStruct((batch, NSA_G, seq, ns), MXU_DT),
                   jax.ShapeDtypeStruct((batch, NSA_G, seq // NSA_TQ, 8, LANE), F32)],
        scratch_shapes=[pltpu.VMEM((nch, n), F32), pltpu.VMEM((nch, n), MXU_DT)],
        compiler_params=_params(3),
        name="nsa_cmp_attn_topk",
    )(slopes, nq, kc, vct, ovl, pool)


def _nsa_attn_body(flag_ref, slope_ref, q_ref, sel_ref, kaug_ref, vst_ref, kwaug_ref, vwt_ref,
                   oct_ref, gt_ref, o_ref, list_scr, os_scr, ow_scr, *, tq, tk, nkt):
    b = pl.program_id(0)
    g = pl.program_id(1)
    qi = pl.program_id(2)
    q0 = qi * tq
    n = NSA_R * tq
    wk = WINDOW + tq

    selneg = ((sel_ref[...].astype(F32) - 1.0) * _BIG).astype(MXU_DT)
    qs = _nsa_queries(q_ref, slope_ref, g, q0, tq)
    qc = jnp.concatenate(qs, axis=0)
    qa = jnp.concatenate([jnp.concatenate([selneg, qr], axis=1) for qr in qs], axis=0)
    t_loc = lax.broadcasted_iota(jnp.int32, (1, n), 1) & (tq - 1)
    t_lane = q0 + t_loc

    kd = q0 // tk
    fbase = ((b * NSA_G + g) * pl.num_programs(2) + qi) * nkt

    def compact(kt, cnt):
        act = flag_ref[fbase + kt] > 0

        @pl.when(act)
        def _():
            list_scr[cnt] = kt
        return cnt + act.astype(jnp.int32)

    cnt = lax.fori_loop(0, kd, compact, 0)
    for j in range(NSA_U - 1):
        list_scr[cnt + j] = kd

    def tiles(idx):
        ss, offs = [], []
        for i in idx:
            off = pl.multiple_of(i * tk, tk)
            ss.append(_dot_nt(kaug_ref[pl.ds(off, tk), :], qa))
            offs.append(off)
        return ss, offs

    def update(ss, offs, carry):
        m_old, acc = carry
        m_new = m_old
        for s in ss:
            m_new = jnp.maximum(m_new, jnp.max(s, 0, keepdims=True))
        acc = jnp.exp(m_old - m_new) * acc
        for s, off in zip(ss, offs):
            acc = acc + _dot(vst_ref[:, pl.ds(off, tk)], jnp.exp(s - m_new).astype(MXU_DT))
        return m_new, acc

    def group(gi, carry):
        return update(*tiles([list_scr[gi * NSA_U + j] for j in range(NSA_U)]), carry)

    nfull = cnt // NSA_U
    carry = lax.fori_loop(0, nfull, group,
                          (jnp.full((1, n), NEG, F32), jnp.zeros((VROWS, n), F32)))
    k_loc = lax.broadcasted_iota(jnp.int32, (tq, n), 0)

    def tail(start, banded):
        ss, offs = tiles([list_scr[nfull * NSA_U + j] for j in range(NSA_U - 1)] + [kd])
        sw = _dot_nt(kwaug_ref[pl.ds(start, wk), :], qc)
        for j in range(NSA_U - 1):
            ss[j] = jnp.where(nfull * NSA_U + j < cnt, ss[j], NEG)
        kpos = offs[-1] + lax.broadcasted_iota(jnp.int32, (tk, n), 0)
        ss[-1] = jnp.where(kpos <= t_lane, ss[-1], NEG)
        if banded:
            parts = [jnp.where(k_loc > t_loc, sw[0:tq], NEG), sw[tq:WINDOW],
                     jnp.where(k_loc <= t_loc, sw[WINDOW:wk], NEG)]
        else:
            dist = t_lane - (start + lax.broadcasted_iota(jnp.int32, (wk, n), 0))
            parts = [jnp.where((dist >= 0) & (dist < WINDOW), sw, NEG)]
        m_old, acc = carry
        m_new = m_old
        for s in ss:
            m_new = jnp.maximum(m_new, jnp.max(s, 0, keepdims=True))
        mw = jnp.max(parts[0], 0, keepdims=True)
        for p in parts[1:]:
            mw = jnp.maximum(mw, jnp.max(p, 0, keepdims=True))
        ps = [jnp.exp(s - m_new).astype(MXU_DT) for s in ss]
        e = jnp.concatenate([jnp.exp(p - mw).astype(MXU_DT) for p in parts], axis=0)
        acc = jnp.exp(m_old - m_new) * acc
        for p, off in zip(ps, offs):
            acc = acc + _dot(vst_ref[:, pl.ds(off, tk)], p)
        ow = _dot(vwt_ref[:, pl.ds(start, wk)], e)
        os_scr[...] = acc[0:NSA_DH, :] / acc[NSA_DH:NSA_DH + 1, :]
        ow_scr[...] = ow[0:NSA_DH, :] / ow[NSA_DH:NSA_DH + 1, :]

    @pl.when(q0 >= WINDOW)
    def _():
        tail(pl.multiple_of(q0 - WINDOW, LANE), True)

    @pl.when(q0 < WINDOW)
    def _():
        tail(0, False)

    o_sel = os_scr[...]
    o_win = ow_scr[...]

    for r in range(NSA_R):
        ls = slice(r * tq, (r + 1) * tq)
        o_ref[r] = (gt_ref[0, r:r + 1, :] * oct_ref[r] + gt_ref[1, r:r + 1, :] * o_sel[:, ls]
                    + gt_ref[2, r:r + 1, :] * o_win[:, ls]).astype(o_ref.dtype)


def _nsa_attn(flags, slopes, nq, sel, kaug, vst, kwaug, vwt, oct_, gt, batch, seq, tq, tk):
    nq_t = seq // tq
    ns = seq // SEL_BLOCK
    nkt = seq // tk
    n = NSA_R * tq
    gw = NSA_R * HP
    bg = lambda b, g, i, f: (b, g, 0, 0)
    grid_spec = pltpu.PrefetchScalarGridSpec(
        num_scalar_prefetch=1,
        grid=(batch, NSA_G, nq_t),
        in_specs=[pl.BlockSpec(memory_space=pltpu.SMEM),
                  pl.BlockSpec((tq, gw), lambda b, g, i, f: (b * nq_t + i, g)),
                  pl.BlockSpec((None, None, tq, ns), lambda b, g, i, f: (b, g, i, 0)),
                  pl.BlockSpec((None, None, seq, ns + HP), bg),
                  pl.BlockSpec((None, None, VROWS, seq), bg),
                  pl.BlockSpec((None, None, seq, HP), bg),
                  pl.BlockSpec((None, None, VROWS, seq), bg),
                  pl.BlockSpec((None, None, NSA_R, NSA_DH, tq), lambda b, g, i, f: (b, g, 0, 0, i)),
                  pl.BlockSpec((None, None, 3, NSA_R, tq), lambda b, g, i, f: (b, g, 0, 0, i))],
        out_specs=pl.BlockSpec((None, None, NSA_R, NSA_DH, tq), lambda b, g, i, f: (b, g, 0, 0, i)),
        scratch_shapes=[pltpu.SMEM((nkt + NSA_U,), jnp.int32),
                        pltpu.VMEM((NSA_DH, n), F32), pltpu.VMEM((NSA_DH, n), F32)])
    return pl.pallas_call(
        functools.partial(_nsa_attn_body, tq=tq, tk=tk, nkt=nkt),
        grid_spec=grid_spec,
        out_shape=jax.ShapeDtypeStruct((batch, NSA_G, NSA_R, NSA_DH, seq), MXU_DT),
        compiler_params=_params(3),
        name="nsa_sel_win_attn",
    )(flags, slopes, nq, sel, kaug, vst, kwaug, vwt, oct_, gt)


_TN = (((0,), (0,)), ((), ()))


def _outproj_body(x_ref, oa_ref, ob_ref, wo_ref, g_ref, b_ref, y_ref):
    wa = oa_ref.shape[0]
    mix = lax.dot_general(oa_ref[...].astype(MXU_DT), wo_ref[0:wa, :], _TN,
                          preferred_element_type=F32)
    mix = mix + lax.dot_general(ob_ref[...].astype(MXU_DT), wo_ref[wa:, :], _TN,
                                preferred_element_type=F32)
    y_ref[...] = _ln_rows(ALPHA * x_ref[...] + mix, g_ref[...], b_ref[...])


def _outproj(x2d, oat, obt, wo, g, b, tm):
    batch, _, seq = oat.shape
    ns = seq // tm
    row = lambda bb, i: (bb * ns + i, 0)
    col = lambda bb, i: (bb, 0, i)
    return pl.pallas_call(
        _outproj_body,
        grid=(batch, ns),
        in_specs=[pl.BlockSpec((tm, D_MODEL), row),
                  pl.BlockSpec((None, oat.shape[1], tm), col),
                  pl.BlockSpec((None, obt.shape[1], tm), col),
                  _const_spec(wo.shape),
                  _const_spec((1, D_MODEL)), _const_spec((1, D_MODEL))],
        out_specs=pl.BlockSpec((tm, D_MODEL), row),
        out_shape=jax.ShapeDtypeStruct((batch * seq, D_MODEL), F32),
        compiler_params=_params(2),
        name="out_proj_ln",
    )(x2d, oat, obt, wo, g.reshape(1, -1), b.reshape(1, -1))


def _xattn_body(x_ref, wq_ref, kv_ref, wo_ref, g_ref, b_ref, y_ref):
    x = x_ref[...]
    q = _dot(x.astype(MXU_DT), wq_ref[...]) * (XA_DH ** -0.5)
    hw = XA_HEADS * XA_DH
    y = jnp.zeros_like(x)
    for h in range(XA_HEADS):
        sl = slice(h * XA_DH, (h + 1) * XA_DH)
        s = _dot_nt(q[:, sl].astype(MXU_DT), kv_ref[:, sl])
        e = jnp.exp(s - jnp.max(s, -1, keepdims=True))
        p = e / jnp.sum(e, -1, keepdims=True)
        o = _dot(p.astype(MXU_DT), kv_ref[:, hw + h * XA_DH:hw + (h + 1) * XA_DH])
        y = y + _dot(o.astype(MXU_DT), wo_ref[sl, :])
    y_ref[...] = _ln_rows(ALPHA * x + y, g_ref[...], b_ref[...])


def _xattn(x2d, wq, memkv, wo, g, b, batch, seq, tm):
    ns = seq // tm
    mem = memkv.shape[0] // batch
    row = lambda bb, i: (bb * ns + i, 0)
    return pl.pallas_call(
        _xattn_body,
        grid=(batch, ns),
        in_specs=[pl.BlockSpec((tm, D_MODEL), row),
                  _const_spec(wq.shape),
                  pl.BlockSpec((mem, memkv.shape[1]), lambda bb, i: (bb, 0)),
                  _const_spec(wo.shape),
                  _const_spec((1, D_MODEL)), _const_spec((1, D_MODEL))],
        out_specs=pl.BlockSpec((tm, D_MODEL), row),
        out_shape=jax.ShapeDtypeStruct((batch * seq, D_MODEL), F32),
        compiler_params=_params(2),
        name="mem_xattn_ln",
    )(x2d, wq, memkv, wo, g.reshape(1, -1), b.reshape(1, -1))


_FF_CHUNK = D_FF
_HALO = 8


def _ffn_body(x_ref, xp_ref, wup_ref, cw_ref, cb_ref, wdn_ref, g_ref, b_ref, y_ref):
    first = pl.program_id(1) == 0
    x = x_ref[...]
    xp = jnp.where(first, 0.0, xp_ref[...])
    xe = jnp.concatenate([xp, x], axis=0).astype(MXU_DT)

    def conv(c0):
        u = _dot(xe, wup_ref[:, c0:c0 + _FF_CHUNK])
        u1 = pltpu.roll(u, 1, 0)
        u2 = pltpu.roll(u, 2, 0)
        cw = cw_ref[:, c0:c0 + _FF_CHUNK]
        r = u * cw[2:3, :] + u1 * cw[1:2, :] + u2 * cw[0:1, :] + cb_ref[:, c0:c0 + _FF_CHUNK]
        return r[_HALO:, :]

    acc = jnp.zeros_like(x)
    for c in range(D_FF // _FF_CHUNK):
        a = conv(c * _FF_CHUNK)
        gg = conv(D_FF + c * _FF_CHUNK)
        act = (a * jax.nn.silu(gg)).astype(MXU_DT)
        acc = acc + _dot(act, wdn_ref[c * _FF_CHUNK:(c + 1) * _FF_CHUNK, :])
    y_ref[...] = _ln_rows(ALPHA * x + acc, g_ref[...], b_ref[...])


def _ffn(x2d, wup, cw, cb, wdn, g, b, batch, seq, tm):
    ns = seq // tm
    hb = tm // _HALO
    row = lambda bb, i: (bb * ns + i, 0)
    prev = lambda bb, i: (jnp.maximum((bb * ns + i) * hb - 1, 0), 0)
    return pl.pallas_call(
        _ffn_body,
        grid=(batch, ns),
        in_specs=[pl.BlockSpec((tm, D_MODEL), row),
                  pl.BlockSpec((_HALO, D_MODEL), prev),
                  _const_spec(wup.shape), _const_spec(cw.shape), _const_spec(cb.shape),
                  _const_spec(wdn.shape),
                  _const_spec((1, D_MODEL)), _const_spec((1, D_MODEL))],
        out_specs=pl.BlockSpec((tm, D_MODEL), row),
        out_shape=jax.ShapeDtypeStruct((batch * seq, D_MODEL), F32),
        compiler_params=_params(2),
        name="conv_ffn_ln",
    )(x2d, x2d, wup, cw, cb, wdn, g.reshape(1, -1), b.reshape(1, -1))


def _rot_half_cols(w):
    half = MLA_ROPE // 2
    return jnp.concatenate([-w[..., half:], w[..., :half]], axis=-1)


def _prep_layer(w_in, cq_g, w_uq, ckv_g, w_ukv):
    d = w_in.shape[0]
    z = lambda n: jnp.zeros((d, n), F32)
    kr = w_in[:, _O_KR:_O_NQ]
    ngl = _IN_COLS - _O_GL
    w_ext = jnp.concatenate([
        w_in[:, _O_CQ:_O_CKV], w_in[:, _O_CKV:_O_KR],
        w_in[:, _O_NQ:_O_KV6] * (NSA_DH ** -0.5),
        w_in[:, _O_KV6:_O_GL],
        kr, w_in[:, _O_GL:_IN_COLS], z(_M_KROT - _M_GL - ngl),
        _rot_half_cols(kr), z(HP - _M_KROT - MLA_ROPE)], axis=1)
    wq = w_uq.reshape(MLA_Q_RANK, MLA_HEADS, MLA_NOPE + MLA_ROPE)
    zq = jnp.zeros((MLA_Q_RANK, MLA_HEADS, HP - MLA_NOPE - MLA_ROPE), F32)
    wqa = jnp.concatenate([wq, zq], axis=-1).reshape(MLA_Q_RANK, MLA_HEADS * HP)
    wqb = jnp.concatenate([jnp.zeros_like(wq[..., :MLA_NOPE]), _rot_half_cols(wq[..., MLA_NOPE:]), zq],
                          axis=-1).reshape(MLA_Q_RANK, MLA_HEADS * HP)
    wkv = w_ukv.reshape(MLA_KV_RANK, MLA_HEADS, MLA_NOPE + MLA_V)
    zk = jnp.zeros((MLA_KV_RANK, MLA_HEADS, HP - MLA_NOPE), F32)
    wkk = jnp.concatenate([wkv[..., :MLA_NOPE], zk], axis=-1).reshape(MLA_KV_RANK, MLA_HEADS * HP)
    wvv = wkv[..., MLA_NOPE:].reshape(MLA_KV_RANK, MLA_HEADS * MLA_V)
    return dict(w_in=w_ext.astype(MXU_DT), cq_g=cq_g.reshape(1, -1), ckv_g=ckv_g.reshape(1, -1),
                wqa=wqa.astype(MXU_DT), wqb=wqb.astype(MXU_DT),
                wkk=wkk.astype(MXU_DT), wkv=wvv.astype(MXU_DT))


def _rope_tabs(seq):
    inv = ROPE_BASE ** (-jnp.arange(0, MLA_ROPE, 2, dtype=F32) / MLA_ROPE)
    ang = jnp.arange(seq, dtype=F32)[:, None] * inv[None, :]
    cos, sin = jnp.cos(ang), jnp.sin(ang)
    pad = jnp.zeros((seq, HP - MLA_NOPE - MLA_ROPE), F32)
    tc = jnp.concatenate([jnp.ones((seq, MLA_NOPE), F32), cos, cos, pad], axis=1)
    ts = jnp.concatenate([jnp.zeros((seq, MLA_NOPE), F32), sin, sin, pad], axis=1)
    scale = (MLA_NOPE + MLA_ROPE) ** -0.5
    return dict(cq=tc * (scale * LOG2E), sq=ts * (scale * LOG2E), ck=tc, sk=ts)


def _pos_cols(pos, width, first):
    lane = jnp.arange(width)[None, :]
    p = pos[:, None]
    return jnp.where(lane == first, p // POS_SPLIT,
                     jnp.where(lane == first + 1, p % POS_SPLIT,
                               jnp.where(lane == first + 2, 1, 0))).astype(F32)


def _ones_row(n):
    return (jnp.arange(VROWS - NSA_DH)[:, None] == jnp.zeros((1, n), jnp.int32)).astype(MXU_DT)


def _nsa_consts(seq):
    nch = seq // CMP_STRIDE
    ns = seq // SEL_BLOCK
    c_start = jnp.arange(nch) * CMP_STRIDE
    s_start = jnp.arange(ns) * SEL_BLOCK
    ovl = jnp.clip(jnp.minimum(c_start[None, :] + CMP_BLOCK, s_start[:, None] + SEL_BLOCK)
                   - jnp.maximum(c_start[None, :], s_start[:, None]), 0).astype(F32) / CMP_BLOCK
    pos = jnp.arange(seq)
    onehot = (pos[:, None] // SEL_BLOCK == jnp.arange(ns)[None, :]).astype(MXU_DT)
    pool = (jnp.arange(ns)[:, None] // (NSA_TK // SEL_BLOCK) == jnp.arange(LANE)[None, :])
    slopes = 2.0 ** (-8.0 * jnp.arange(1, NSA_HEADS + 1, dtype=F32) / NSA_HEADS)
    return dict(ovl=ovl.astype(MXU_DT),
                onehot=onehot,
                posc=_pos_cols(pos, HP, NSA_DH),
                cendc=_pos_cols(c_start + CMP_BLOCK - 1, HP - NSA_DH, 0).astype(MXU_DT),
                pool=pool.astype(MXU_DT), slopes=slopes)


def kernel(x, mem, ln_in_g, ln_in_b, ln_mem_g, ln_mem_b, w_in, mla_cq_g, mla_w_uq, mla_ckv_g, mla_w_ukv, nsa_pe_k, nsa_w1_k, nsa_b1_k, nsa_w2_k, nsa_pe_v, nsa_w1_v, nsa_b1_v, nsa_w2_v, w_out, ln1_g, ln1_b, xa_wq, xa_wkv, xa_wo, ln2_g, ln2_b, ffn_w_up, ffn_conv_w, ffn_conv_b, ffn_w_down, ln3_g, ln3_b):
    batch, seq, d = x.shape
    t = batch * seq
    depth = w_in.shape[0]
    tm = min(TM, seq)
    nsa_tq, nsa_tk = min(NSA_TQ, seq), min(NSA_TK, seq)
    tabs = _rope_tabs(seq)
    consts = _nsa_consts(seq)
    nch = seq // CMP_STRIDE
    nkt = seq // nsa_tk
    bg = (batch, NSA_G)
    bcast = lambda a: jnp.broadcast_to(a, bg + a.shape)

    xs = _layer_norm(x.reshape(t, d), ln_in_g, ln_in_b, tm)
    memn = _layer_norm(mem.reshape(-1, d), ln_mem_g, ln_mem_b, mem.shape[0] * mem.shape[1])

    for l in range(depth):
        lw = _prep_layer(w_in[l], mla_cq_g[l], mla_w_uq[l], mla_ckv_g[l], mla_w_ukv[l])
        q, k, vt, nq, kcvc, kaug, vst, kwaug, vwt, gates = _inproj(xs, lw, tabs, consts, seq, tm)

        oa = _mla_attn(q, k, vt, batch, seq, min(MLA_T, seq))

        pe = jnp.tile(jnp.stack([nsa_pe_k[l], nsa_pe_v[l]]), (1, 1, NSA_G))
        w1 = jnp.stack([nsa_w1_k[l], nsa_w1_v[l]]).reshape(2, CMP_BLOCK, NSA_DH, CMP_HIDDEN)
        w1 = _head_blockdiag(w1).astype(MXU_DT)
        b1 = jnp.tile(jnp.stack([nsa_b1_k[l], nsa_b1_v[l]]).reshape(2, 1, CMP_HIDDEN), (1, 1, NSA_G))
        w2 = _head_blockdiag(jnp.stack([nsa_w2_k[l], nsa_w2_v[l]])).astype(MXU_DT)
        cmp = _compress(kcvc, pe, w1, b1, w2, batch, seq)
        cmp = cmp.reshape(2, batch, nch, NSA_G, NSA_DH).transpose(0, 1, 3, 2, 4)
        kcaug = jnp.concatenate([cmp[0], bcast(consts["cendc"])], axis=-1)
        vct = jnp.concatenate([cmp[1].transpose(0, 1, 3, 2), bcast(_ones_row(nch))], axis=2)
        oc_t, sel, cnt = _cmp_attn(consts["slopes"], nq, kcaug, vct, consts["ovl"], consts["pool"],
                                   batch, seq, min(CMP_TQ, seq))
        flags = (cnt[:, :, :, 0, :nkt] > 0).astype(jnp.int32).reshape(-1)
        gt = gates[:, _M_GL:_M_GL + 3 * NSA_HEADS].reshape(batch, seq, NSA_G, NSA_R, 3)
        gt = gt.transpose(0, 2, 4, 3, 1)
        ob = _nsa_attn(flags, consts["slopes"], nq, sel, kaug, vst, kwaug, vwt, oc_t, gt,
                       batch, seq, nsa_tq, nsa_tk)
        ob = ob.reshape(batch, NSA_HEADS * NSA_DH, seq)

        xs = _outproj(xs, oa, ob, w_out[l].astype(MXU_DT), ln1_g[l], ln1_b[l], tm)

        memkv = _matmul(memn, xa_wkv[l].astype(MXU_DT), 512, MXU_DT)
        xs = _xattn(xs, xa_wq[l].astype(MXU_DT), memkv, xa_wo[l].astype(MXU_DT),
                    ln2_g[l], ln2_b[l], batch, seq, tm)

        xs = _ffn(xs, ffn_w_up[l].astype(MXU_DT), ffn_conv_w[l], ffn_conv_b[l].reshape(1, -1),
                  ffn_w_down[l].astype(MXU_DT), ln3_g[l], ln3_b[l], batch, seq, tm)

    return xs.reshape(batch, seq, d)
```

```python
import functools

import jax
import jax.numpy as jnp
from jax import lax
from jax.experimental import pallas as pl
from jax.experimental.pallas import tpu as pltpu

F32 = jnp.float32
MXU_DT = jnp.bfloat16

D_MODEL = 1024
MLA_HEADS = 8
MLA_NOPE = 64
MLA_ROPE = 32
MLA_V = 64
MLA_Q_RANK = 384
MLA_KV_RANK = 256
ROPE_BASE = 10000.0
NSA_HEADS = 8
NSA_G = 2
NSA_R = 4
NSA_DH = 64
CMP_STRIDE = 16
CMP_BLOCK = 32
CMP_HIDDEN = 128
SEL_BLOCK = 64
N_SEL = 16
WINDOW = 512
XA_HEADS = 4
XA_DH = 256
D_FF = 2816
DEPTH = 2
ALPHA = (2.0 * DEPTH) ** 0.25
LN_EPS = 1e-5
RMS_EPS = 1e-6
NEG = -1e30

LANE = 128
HP = 128
VMEM_LIMIT = 56 * 1024 * 1024

TM = 512
MLA_T = 512
MLA_HPS = 4
CMP_TQ = 512
CMP_CH = 128
NSA_TQ = 256
NSA_TK = 256
NSA_U = 2
POS_SPLIT = 128
VROWS = 80
N_FORCED = 3
LOG2E = 1.4426950408889634

_O_CQ, _O_CKV, _O_KR, _O_NQ = 0, 384, 640, 672
_O_KV6, _O_GL, _IN_COLS = 1184, 1952, 1976
_E_CQ, _E_CKV, _E_NQ, _E_KCVC, _E_KS, _E_VS, _E_KW, _E_VW, _E_MISC, _E_END = (
    0, 384, 640, 1152, 1408, 1536, 1664, 1792, 1920, 2048)
_M_KR, _M_GL, _M_KROT = 0, 32, 64

_NT = (((1,), (1,)), ((), ()))


def _params(n_axes=1):
    return pltpu.CompilerParams(
        dimension_semantics=("arbitrary",) * n_axes,
        vmem_limit_bytes=VMEM_LIMIT)


def _const_spec(shape):
    nd = len(shape)
    return pl.BlockSpec(shape, lambda *_: (0,) * nd)


def _ln_rows(v, g, b):
    mu = jnp.mean(v, -1, keepdims=True)
    d = v - mu
    var = jnp.mean(d * d, -1, keepdims=True)
    return d * lax.rsqrt(var + LN_EPS) * g + b


def _dot(a, b):
    return jnp.dot(a, b, preferred_element_type=F32)


def _dot_nt(a, b):
    return lax.dot_general(a, b, _NT, preferred_element_type=F32)


def _ln_body(x_ref, g_ref, b_ref, o_ref):
    o_ref[...] = _ln_rows(x_ref[...], g_ref[...], b_ref[...])


def _layer_norm(x2d, g, b, tm):
    t, d = x2d.shape
    return pl.pallas_call(
        _ln_body,
        grid=(t // tm,),
        in_specs=[pl.BlockSpec((tm, d), lambda i: (i, 0)),
                  _const_spec((1, d)), _const_spec((1, d))],
        out_specs=pl.BlockSpec((tm, d), lambda i: (i, 0)),
        out_shape=jax.ShapeDtypeStruct((t, d), F32),
        compiler_params=_params(1),
        name="ln_rows",
    )(x2d, g.reshape(1, d), b.reshape(1, d))


def _matmul_body(a_ref, w_ref, o_ref):
    o_ref[...] = _dot(a_ref[...].astype(MXU_DT), w_ref[...]).astype(o_ref.dtype)


def _matmul(a, w, tn, out_dtype):
    m, k = a.shape
    n = w.shape[1]
    return pl.pallas_call(
        _matmul_body,
        grid=(n // tn,),
        in_specs=[_const_spec((m, k)), pl.BlockSpec((k, tn), lambda j: (0, j))],
        out_specs=pl.BlockSpec((m, tn), lambda j: (0, j)),
        out_shape=jax.ShapeDtypeStruct((m, n), out_dtype),
        compiler_params=_params(1),
        name="mem_kv_proj",
    )(a, w)


def _rms_rows(v, g):
    return v * lax.rsqrt(jnp.mean(v * v, -1, keepdims=True) + RMS_EPS) * g


def _inproj_body(x_ref, win_ref, cqg_ref, ckvg_ref, wqa_ref, wqb_ref, wkk_ref, wkv_ref,
                 tcq_ref, tsq_ref, tck_ref, tsk_ref, onehot_ref, posc_ref,
                 q_ref, k_ref, vt_ref, nq_ref, kcvc_ref, kaug_ref, vst_ref, kwaug_ref, vwt_ref,
                 gate_ref):
    xb = x_ref[...].astype(MXU_DT)
    h = _dot(xb, win_ref[...])
    cqn = _rms_rows(h[:, _E_CQ:_E_CKV], cqg_ref[...]).astype(MXU_DT)
    qa = _dot(cqn, wqa_ref[...])
    qb = _dot(cqn, wqb_ref[...])
    ckvn = _rms_rows(h[:, _E_CKV:_E_NQ], ckvg_ref[...]).astype(MXU_DT)
    kn = _dot(ckvn, wkk_ref[...])
    tail = jnp.where(lax.broadcasted_iota(jnp.int32, (VROWS - MLA_V, h.shape[0]), 0) == 0,
                     1.0, 0.0).astype(MXU_DT)
    vt = _dot(ckvn, wkv_ref[...]).T.astype(vt_ref.dtype)
    for hh in range(MLA_HEADS):
        vt_ref[hh * VROWS:hh * VROWS + MLA_V, :] = vt[hh * MLA_V:(hh + 1) * MLA_V, :]
        vt_ref[hh * VROWS + MLA_V:(hh + 1) * VROWS, :] = tail
    ns = onehot_ref.shape[1]
    posc = posc_ref[...]
    lane = lax.broadcasted_iota(jnp.int32, (h.shape[0], HP), 1)
    low = lane < NSA_DH

    def halves(blk):
        return jnp.where(low, blk, 0.0), jnp.where(low, pltpu.roll(blk, NSA_DH, 1), 0.0)

    ks = halves(h[:, _E_KS:_E_VS])
    kw = halves(h[:, _E_KW:_E_VW])
    vst = h[:, _E_VS:_E_KW].T.astype(MXU_DT)
    vwt = h[:, _E_VW:_E_MISC].T.astype(MXU_DT)
    for g in range(NSA_G):
        kaug_ref[g, :, 0:ns] = onehot_ref[...]
        kaug_ref[g, :, ns:ns + HP] = (ks[g] + posc).astype(MXU_DT)
        kwaug_ref[g] = (kw[g] + posc).astype(MXU_DT)
        for t_ref, tv in ((vst_ref, vst), (vwt_ref, vwt)):
            t_ref[g, 0:NSA_DH, :] = tv[g * NSA_DH:(g + 1) * NSA_DH, :]
            t_ref[g, NSA_DH:VROWS, :] = tail
    for j in range(NSA_HEADS // 2):
        pair = halves(h[:, _E_NQ + j * HP:_E_NQ + (j + 1) * HP])
        for i in range(2):
            nq_ref[:, (2 * j + i) * HP:(2 * j + i + 1) * HP] = pair[i].astype(nq_ref.dtype)
    misc = h[:, _E_MISC:_E_END]
    krr = (jnp.where(low, 0.0, pltpu.roll(misc, _M_KROT - _M_KR, 1)) * tck_ref[...]
           + misc * tsk_ref[...])
    tcq = tcq_ref[...]
    tsq = tsq_ref[...]
    for hh in range(MLA_HEADS):
        sl = slice(hh * HP, (hh + 1) * HP)
        q_ref[:, sl] = (qa[:, sl] * tcq + qb[:, sl] * tsq).astype(q_ref.dtype)
        k_ref[:, sl] = (kn[:, sl] + krr).astype(k_ref.dtype)
    kcvc_ref[...] = h[:, _E_KCVC:_E_KS]
    gate_ref[...] = jax.nn.sigmoid(misc)


def _inproj(x2d, lw, tabs, consts, seq, tm):
    t = x2d.shape[0]
    batch = t // seq
    npos = seq // tm
    ns = seq // SEL_BLOCK
    row = lambda i: (i, 0)
    pos = lambda i: (i % npos, 0)
    tok = lambda i: (i // npos, 0, i % npos, 0)
    ttr = lambda i: (i // npos, 0, 0, i % npos)
    hw = MLA_HEADS * HP
    vrows = MLA_HEADS * VROWS
    sds = jax.ShapeDtypeStruct
    outs = [
        (sds((t, hw), MXU_DT), pl.BlockSpec((tm, hw), row)),
        (sds((t, hw), MXU_DT), pl.BlockSpec((tm, hw), row)),
        (sds((batch, vrows, seq), MXU_DT),
         pl.BlockSpec((None, vrows, tm), lambda i: (i // npos, 0, i % npos))),
        (sds((t, NSA_HEADS * HP), MXU_DT), pl.BlockSpec((tm, NSA_HEADS * HP), row)),
        (sds((t, 256), F32), pl.BlockSpec((tm, 256), row)),
        (sds((batch, NSA_G, seq, ns + HP), MXU_DT), pl.BlockSpec((None, NSA_G, tm, ns + HP), tok)),
        (sds((batch, NSA_G, VROWS, seq), MXU_DT), pl.BlockSpec((None, NSA_G, VROWS, tm), ttr)),
        (sds((batch, NSA_G, seq, HP), MXU_DT), pl.BlockSpec((None, NSA_G, tm, HP), tok)),
        (sds((batch, NSA_G, VROWS, seq), MXU_DT), pl.BlockSpec((None, NSA_G, VROWS, tm), ttr)),
        (sds((t, LANE), F32), pl.BlockSpec((tm, LANE), row)),
    ]
    out_shape = [o[0] for o in outs]
    out_specs = [o[1] for o in outs]
    return pl.pallas_call(
        _inproj_body,
        grid=(t // tm,),
        in_specs=[pl.BlockSpec((tm, D_MODEL), row),
                  _const_spec((D_MODEL, _E_END)),
                  _const_spec((1, MLA_Q_RANK)), _const_spec((1, MLA_KV_RANK)),
                  _const_spec((MLA_Q_RANK, hw)), _const_spec((MLA_Q_RANK, hw)),
                  _const_spec((MLA_KV_RANK, hw)), _const_spec((MLA_KV_RANK, MLA_HEADS * MLA_V)),
                  pl.BlockSpec((tm, HP), pos), pl.BlockSpec((tm, HP), pos),
                  pl.BlockSpec((tm, HP), pos), pl.BlockSpec((tm, HP), pos),
                  pl.BlockSpec((tm, ns), pos), pl.BlockSpec((tm, HP), pos)],
        out_specs=out_specs,
        out_shape=out_shape,
        compiler_params=_params(1),
        name="in_proj",
    )(x2d, lw["w_in"], lw["cq_g"], lw["ckv_g"], lw["wqa"], lw["wqb"], lw["wkk"], lw["wkv"],
      tabs["cq"], tabs["sq"], tabs["ck"], tabs["sk"], consts["onehot"], consts["posc"])


def _mla_attn_body(q_ref, k_ref, vt_ref, o_ref, *, tq):
    qi = pl.program_id(2)

    tk2 = 2 * tq

    def step(off, nk, carry, diag_from):
        ss = []
        for h in range(MLA_HPS):
            hs = slice(h * HP, (h + 1) * HP)
            s = _dot_nt(k_ref[pl.ds(off, nk), hs], q_ref[:, hs])
            if diag_from is not None:
                kpos = lax.broadcasted_iota(jnp.int32, (nk, tq), 0) - diag_from
                qpos = lax.broadcasted_iota(jnp.int32, (nk, tq), 1)
                s = jnp.where(kpos <= qpos, s, NEG)
            ss.append(s)
        ms, alphas, ps = [], [], []
        for h in range(MLA_HPS):
            m = carry[h][0]
            m_new = jnp.maximum(m, jnp.max(ss[h], 0, keepdims=True))
            alphas.append(jnp.exp2(m - m_new))
            ps.append(jnp.exp2(ss[h] - m_new).astype(MXU_DT))
            ms.append(m_new)
        out = []
        for h in range(MLA_HPS):
            vt = vt_ref[h * VROWS:(h + 1) * VROWS, pl.ds(off, nk)]
            out.append((ms[h], alphas[h] * carry[h][1] + _dot(vt, ps[h])))
        return tuple(out)

    def finish(carry):
        for h in range(MLA_HPS):
            acc = carry[h][1]
            o_ref[h * MLA_V:(h + 1) * MLA_V, :] = (
                acc[0:MLA_V, :] / acc[MLA_V:MLA_V + 1, :]).astype(o_ref.dtype)

    one = (jnp.full((1, tq), NEG, F32), jnp.zeros((VROWS, tq), F32))
    carry = lax.fori_loop(
        0, qi // 2, lambda kp, c: step(pl.multiple_of(kp * tk2, tk2), tk2, c, None),
        (one,) * MLA_HPS)

    @pl.when(qi % 2 == 1)
    def _():
        finish(step(pl.multiple_of((qi - 1) * tq, tq), tk2, carry, tq))

    @pl.when(qi % 2 == 0)
    def _():
        finish(step(pl.multiple_of(qi * tq, tq), tq, carry, 0))


def _mla_attn(q, k, vt, batch, seq, tq):
    nq = seq // tq
    pw = MLA_HPS * HP
    return pl.pallas_call(
        functools.partial(_mla_attn_body, tq=tq),
        grid=(batch, MLA_HEADS // MLA_HPS, nq),
        in_specs=[pl.BlockSpec((tq, pw), lambda b, h, i: (b * nq + i, h)),
                  pl.BlockSpec((seq, pw), lambda b, h, i: (b, h)),
                  pl.BlockSpec((None, MLA_HPS * VROWS, seq), lambda b, h, i: (b, h, 0))],
        out_specs=pl.BlockSpec((None, MLA_HPS * MLA_V, tq), lambda b, h, i: (b, h, i)),
        out_shape=jax.ShapeDtypeStruct((batch, MLA_HEADS * MLA_V, seq), MXU_DT),
        compiler_params=_params(3),
        name="mla_attn",
    )(q, k, vt)


def _compress_body(c_ref, pe_ref, w1_ref, b1_ref, w2_ref, o_ref):
    nch = c_ref.shape[0] // CMP_STRIDE
    u = jnp.zeros((nch, NSA_G * CMP_HIDDEN), F32)
    w = jnp.zeros((nch, NSA_G * CMP_HIDDEN), F32)
    for p in range(CMP_STRIDE):
        x = c_ref[pl.ds(p, nch, stride=CMP_STRIDE), :]
        u = u + _dot((x + pe_ref[p:p + 1, :]).astype(MXU_DT), w1_ref[p])
        w = w + _dot((x + pe_ref[CMP_STRIDE + p:CMP_STRIDE + p + 1, :]).astype(MXU_DT),
                     w1_ref[CMP_STRIDE + p])
    w = pltpu.roll(w, nch - 1, 0)
    hid = jax.nn.gelu(u + w + b1_ref[...])
    o_ref[...] = _dot(hid.astype(MXU_DT), w2_ref[...]).astype(o_ref.dtype)


def _compress(kcvc, pe, w1, b1, w2, batch, seq):
    nch = seq // CMP_STRIDE
    gw = NSA_G * NSA_DH
    gh = NSA_G * CMP_HIDDEN
    return pl.pallas_call(
        _compress_body,
        grid=(2, batch),
        in_specs=[pl.BlockSpec((seq, gw), lambda s, j: (j, s)),
                  pl.BlockSpec((None, CMP_BLOCK, gw), lambda s, j: (s, 0, 0)),
                  pl.BlockSpec((None, CMP_BLOCK, gw, gh), lambda s, j: (s, 0, 0, 0)),
                  pl.BlockSpec((None, 1, gh), lambda s, j: (s, 0, 0)),
                  pl.BlockSpec((None, gh, gw), lambda s, j: (s, 0, 0))],
        out_specs=pl.BlockSpec((None, None, nch, gw), lambda s, j: (s, j, 0, 0)),
        out_shape=jax.ShapeDtypeStruct((2, batch, nch, gw), MXU_DT),
        compiler_params=_params(2),
        name="nsa_compress",
    )(kcvc, pe, w1, b1, w2)


def _head_blockdiag(w):
    eye = jnp.eye(NSA_G, dtype=w.dtype)
    out = jnp.einsum("gh,...ab->...gahb", eye, w)
    return out.reshape(w.shape[:-2] + (NSA_G * w.shape[-2], NSA_G * w.shape[-1]))


_BIG = 1e30


def _nsa_queries(q_ref, slope_ref, g, q0, tq, first_head):
    lane = lax.broadcasted_iota(jnp.int32, (tq, HP), 1)
    q0f = q0.astype(F32)
    qs = []
    for r in range(NSA_R):
        slope = slope_ref[g * NSA_R + r]
        c = jnp.where(lane == NSA_DH, slope * POS_SPLIT,
                      jnp.where(lane == NSA_DH + 1, slope,
                                jnp.where(lane == NSA_DH + 2, -slope * q0f, 0.0)))
        h0 = (first_head + r) * HP
        qs.append(q_ref[:, h0:h0 + HP] + c.astype(MXU_DT))
    return qs


def _cmp_attn_body(slope_ref, q_ref, kc_ref, vct_ref, ovl_ref, pool_ref,
                   oc_ref, sel_ref, flag_ref, s_scr, e_scr, *, tq, n_sel):
    g = pl.program_id(1)
    q0 = pl.program_id(2) * tq
    ns = sel_ref.shape[1]
    n = NSA_R * tq
    ch = min(CMP_CH, kc_ref.shape[0])
    qc = jnp.concatenate(_nsa_queries(q_ref, slope_ref, g, q0, tq, 0), axis=0)
    t_lane = q0 + (lax.broadcasted_iota(jnp.int32, (1, n), 1) & (tq - 1))
    cend0 = lax.broadcasted_iota(jnp.int32, (ch, n), 0) * CMP_STRIDE + (CMP_BLOCK - 1)
    nproc = jnp.maximum(q0 + tq - CMP_BLOCK, 0) // (ch * CMP_STRIDE) + 1

    def scores(c, m):
        off = pl.multiple_of(c * ch, ch)
        s = _dot_nt(kc_ref[pl.ds(off, ch), :], qc)
        s = jnp.where(cend0 <= t_lane - off * CMP_STRIDE, s, NEG)
        s_scr[pl.ds(off, ch), :] = s
        return jnp.maximum(m, jnp.max(s, 0, keepdims=True))

    m = lax.fori_loop(0, nproc, scores, jnp.full((1, n), NEG, F32))

    def weights(c, carry):
        off = pl.multiple_of(c * ch, ch)
        e_scr[pl.ds(off, ch), :] = jnp.exp(s_scr[pl.ds(off, ch), :] - m).astype(MXU_DT)
        return carry

    def clear(c, carry):
        e_scr[pl.ds(pl.multiple_of(c * ch, ch), ch), :] = jnp.zeros((ch, n), MXU_DT)
        return carry

    lax.fori_loop(0, nproc, weights, 0)
    lax.fori_loop(nproc, kc_ref.shape[0] // ch, clear, 0)
    e = e_scr[...]
    acc = _dot(vct_ref[...], e)
    has_key = t_lane >= CMP_BLOCK - 1
    inv = jnp.where(has_key, 1.0 / acc[NSA_DH:NSA_DH + 1, :], 0.0)
    impn = _dot(ovl_ref[...], e) * inv
    imp = impn[:, 0:tq]
    for r in range(NSA_R):
        oc_ref[r] = acc[0:NSA_DH, r * tq:(r + 1) * tq] * inv[:, r * tq:(r + 1) * tq]
        if r:
            imp = imp + impn[:, r * tq:(r + 1) * tq]

    blk = lax.broadcasted_iota(jnp.int32, (ns, tq), 0)
    cur = (q0 + lax.broadcasted_iota(jnp.int32, (ns, tq), 1)) // SEL_BLOCK
    blkf = blk.astype(F32)
    forced = (blk == 0) | (blk == cur) | (blk == cur - 1)
    score0 = jnp.where((blk <= cur) & jnp.logical_not(forced), imp, -1.0)
    score = score0
    for _ in range(n_sel - N_FORCED):
        mx = jnp.max(score, 0, keepdims=True)
        first = jnp.min(jnp.where(score == mx, blkf, float(ns)), 0, keepdims=True)
        score = jnp.where(blkf == first, -1.0, score)
    sel = jnp.where(forced, 1.0, jnp.where(score != score0, 1.0, 0.0))
    sel_ref[...] = sel.T.astype(sel_ref.dtype)
    ones = jnp.ones((8, NSA_TQ), MXU_DT)
    selb = sel.astype(MXU_DT)
    for j in range(tq // NSA_TQ):
        cnt = _dot_nt(ones, selb[:, j * NSA_TQ:(j + 1) * NSA_TQ])
        flag_ref[j] = _dot(cnt.astype(MXU_DT), pool_ref[...])


def _cmp_attn(slopes, nq, kc, vct, ovl, pool, batch, seq, tq):
    nq_t = seq // tq
    nch = kc.shape[-2]
    ns = seq // SEL_BLOCK
    n_sel = min(N_SEL, ns)
    sub = tq // NSA_TQ
    gw = NSA_R * HP
    n = NSA_R * tq
    return pl.pallas_call(
        functools.partial(_cmp_attn_body, tq=tq, n_sel=n_sel),
        grid=(batch, NSA_G, nq_t),
        in_specs=[pl.BlockSpec(memory_space=pltpu.SMEM),
                  pl.BlockSpec((tq, gw), lambda b, g, i: (b * nq_t + i, g)),
                  pl.BlockSpec((None, None, nch, HP), lambda b, g, i: (b, g, 0, 0)),
                  pl.BlockSpec((None, None, VROWS, nch), lambda b, g, i: (b, g, 0, 0)),
                  _const_spec(ovl.shape), _const_spec(pool.shape)],
        out_specs=[pl.BlockSpec((None, None, NSA_R, NSA_DH, tq), lambda b, g, i: (b, g, 0, 0, i)),
                   pl.BlockSpec((None, None, tq, ns), lambda b, g, i: (b, g, i, 0)),
                   pl.BlockSpec((None, None, sub, 8, LANE), lambda b, g, i: (b, g, i, 0, 0))],
        out_shape=[jax.ShapeDtypeStruct((batch, NSA_G, NSA_R, NSA_DH, seq), F32),
                   jax.ShapeDtypeStruct((batch, NSA_G, seq, ns), MXU_DT),
                   jax.ShapeDtypeStruct((batch, NSA_G, seq // NSA_TQ, 8, LANE), F32)],
        scratch_shapes=[pltpu.VMEM((nch, n), F32), pltpu.VMEM((nch, n), MXU_DT)],
        compiler_params=_params(3),
        name="nsa_cmp_attn_topk",
    )(slopes, nq, kc, vct, ovl, pool)


def _nsa_attn_body(flag_ref, slope_ref, q_ref, sel_ref, kaug_ref, vst_ref, kwaug_ref, vwt_ref,
                   oct_ref, gt_ref, o_ref, list0_scr, list1_scr, os_scr, ow_scr, *, tq, tk, nkt):
    b = pl.program_id(0)
    qi = pl.program_id(1)
    q0 = qi * tq
    n = NSA_R * tq
    wk = WINDOW + tq
    lists = (list0_scr, list1_scr)
    heads = range(NSA_G)

    t_loc = lax.broadcasted_iota(jnp.int32, (1, n), 1) & (tq - 1)
    t_lane = q0 + t_loc
    kd = q0 // tk

    qa, qc, cnt = [], [], []
    for g in heads:
        selneg = ((sel_ref[g].astype(F32) - 1.0) * _BIG).astype(MXU_DT)
        qs = _nsa_queries(q_ref, slope_ref, g, q0, tq, g * NSA_R)
        qc.append(jnp.concatenate(qs, axis=0))
        qa.append(jnp.concatenate([jnp.concatenate([selneg, qr], axis=1) for qr in qs], axis=0))
        fbase = ((b * NSA_G + g) * pl.num_programs(1) + qi) * nkt

        def compact(kt, c, fbase=fbase, lst=lists[g]):
            act = flag_ref[fbase + kt] > 0

            @pl.when(act)
            def _():
                lst[c] = kt
            return c + act.astype(jnp.int32)

        c = lax.fori_loop(0, kd, compact, 0)
        for j in range(NSA_U - 1):
            lists[g][c + j] = kd
        cnt.append(c)
    nfull = [c // NSA_U for c in cnt]

    def scores(g, idx):
        offs = [pl.multiple_of(i * tk, tk) for i in idx]
        return [_dot_nt(kaug_ref[g, pl.ds(off, tk), :], qa[g]) for off in offs], offs

    def soft(ss, m_old):
        m_new = m_old
        for s in ss:
            m_new = jnp.maximum(m_new, jnp.max(s, 0, keepdims=True))
        return m_new, jnp.exp(m_old - m_new), [jnp.exp(s - m_new).astype(MXU_DT) for s in ss]

    def pv(g, ps, offs, alpha, acc):
        acc = alpha * acc
        for p, off in zip(ps, offs):
            acc = acc + _dot(vst_ref[g, :, pl.ds(off, tk)], p)
        return acc

    def groups(gs, gi, carry):
        sc = [scores(g, [lists[g][gi * NSA_U + j] for j in range(NSA_U)]) for g in gs]
        sm = [soft(sc[k][0], carry[k][0]) for k in range(len(gs))]
        return tuple((sm[k][0], pv(g, sm[k][2], sc[k][1], sm[k][1], carry[k][1]))
                     for k, g in enumerate(gs))

    init = (jnp.full((1, n), NEG, F32), jnp.zeros((VROWS, n), F32))
    nboth = jnp.minimum(nfull[0], nfull[1])
    carry = lax.fori_loop(0, nboth, lambda gi, c: groups((0, 1), gi, c), (init, init))
    carry = [lax.fori_loop(nboth, nfull[g], lambda gi, c, g=g: groups((g,), gi, (c,))[0], carry[g])
             for g in heads]

    k_loc = lax.broadcasted_iota(jnp.int32, (tq, n), 0)

    def tail(start, banded):
        sc = [scores(g, [lists[g][nfull[g] * NSA_U + j] for j in range(NSA_U - 1)] + [kd])
              for g in heads]
        sw = [_dot_nt(kwaug_ref[g, pl.ds(start, wk), :], qc[g]) for g in heads]
        parts = []
        for g in heads:
            ss, offs = sc[g]
            for j in range(NSA_U - 1):
                ss[j] = jnp.where(nfull[g] * NSA_U + j < cnt[g], ss[j], NEG)
            kpos = offs[-1] + lax.broadcasted_iota(jnp.int32, (tk, n), 0)
            ss[-1] = jnp.where(kpos <= t_lane, ss[-1], NEG)
            if banded:
                parts.append([jnp.where(k_loc > t_loc, sw[g][0:tq], NEG), sw[g][tq:WINDOW],
                              jnp.where(k_loc <= t_loc, sw[g][WINDOW:wk], NEG)])
            else:
                dist = t_lane - (start + lax.broadcasted_iota(jnp.int32, (wk, n), 0))
                parts.append([jnp.where((dist >= 0) & (dist < WINDOW), sw[g], NEG)])
        sm, es = [], []
        for g in heads:
            sm.append(soft(sc[g][0], carry[g][0]))
            mw = jnp.max(parts[g][0], 0, keepdims=True)
            for p in parts[g][1:]:
                mw = jnp.maximum(mw, jnp.max(p, 0, keepdims=True))
            es.append(jnp.concatenate([jnp.exp(p - mw).astype(MXU_DT) for p in parts[g]], axis=0))
        for g in heads:
            acc = pv(g, sm[g][2], sc[g][1], sm[g][1], carry[g][1])
            ow = _dot(vwt_ref[g, :, pl.ds(start, wk)], es[g])
            os_scr[g] = acc[0:NSA_DH, :] / acc[NSA_DH:NSA_DH + 1, :]
            ow_scr[g] = ow[0:NSA_DH, :] / ow[NSA_DH:NSA_DH + 1, :]

    @pl.when(q0 >= WINDOW)
    def _():
        tail(pl.multiple_of(q0 - WINDOW, LANE), True)

    @pl.when(q0 < WINDOW)
    def _():
        tail(0, False)

    for g in heads:
        o_sel = os_scr[g]
        o_win = ow_scr[g]
        for r in range(NSA_R):
            ls = slice(r * tq, (r + 1) * tq)
            o_ref[g, r] = (gt_ref[g, 0, r:r + 1, :] * oct_ref[g, r]
                           + gt_ref[g, 1, r:r + 1, :] * o_sel[:, ls]
                           + gt_ref[g, 2, r:r + 1, :] * o_win[:, ls]).astype(o_ref.dtype)


def _nsa_attn(flags, slopes, nq, sel, kaug, vst, kwaug, vwt, oct_, gt, batch, seq, tq, tk):
    nq_t = seq // tq
    ns = seq // SEL_BLOCK
    nkt = seq // tk
    n = NSA_R * tq
    once = dict(pipeline_mode=pl.Buffered(1))
    res = lambda b, i, f: (b, 0, 0, 0)
    til = lambda b, i, f: (b, 0, 0, 0, i)
    grid_spec = pltpu.PrefetchScalarGridSpec(
        num_scalar_prefetch=1,
        grid=(batch, nq_t),
        in_specs=[pl.BlockSpec(memory_space=pltpu.SMEM),
                  pl.BlockSpec((tq, NSA_HEADS * HP), lambda b, i, f: (b * nq_t + i, 0)),
                  pl.BlockSpec((None, NSA_G, tq, ns), lambda b, i, f: (b, 0, i, 0)),
                  pl.BlockSpec((None, NSA_G, seq, ns + HP), res, **once),
                  pl.BlockSpec((None, NSA_G, VROWS, seq), res, **once),
                  pl.BlockSpec((None, NSA_G, seq, HP), res, **once),
                  pl.BlockSpec((None, NSA_G, VROWS, seq), res, **once),
                  pl.BlockSpec((None, NSA_G, NSA_R, NSA_DH, tq), til),
                  pl.BlockSpec((None, NSA_G, 3, NSA_R, tq), til)],
        out_specs=pl.BlockSpec((None, NSA_G, NSA_R, NSA_DH, tq), til),
        scratch_shapes=[pltpu.SMEM((nkt + NSA_U,), jnp.int32), pltpu.SMEM((nkt + NSA_U,), jnp.int32),
                        pltpu.VMEM((NSA_G, NSA_DH, n), F32), pltpu.VMEM((NSA_G, NSA_DH, n), F32)])
    return pl.pallas_call(
        functools.partial(_nsa_attn_body, tq=tq, tk=tk, nkt=nkt),
        grid_spec=grid_spec,
        out_shape=jax.ShapeDtypeStruct((batch, NSA_G, NSA_R, NSA_DH, seq), MXU_DT),
        compiler_params=_params(2),
        name="nsa_sel_win_attn",
    )(flags, slopes, nq, sel, kaug, vst, kwaug, vwt, oct_, gt)


_TN = (((0,), (0,)), ((), ()))


def _outproj_body(x_ref, oa_ref, ob_ref, wo_ref, g_ref, b_ref, y_ref):
    wa = oa_ref.shape[0]
    mix = lax.dot_general(oa_ref[...].astype(MXU_DT), wo_ref[0:wa, :], _TN,
                          preferred_element_type=F32)
    mix = mix + lax.dot_general(ob_ref[...].astype(MXU_DT), wo_ref[wa:, :], _TN,
                                preferred_element_type=F32)
    y_ref[...] = _ln_rows(ALPHA * x_ref[...] + mix, g_ref[...], b_ref[...])


def _outproj(x2d, oat, obt, wo, g, b, tm):
    batch, _, seq = oat.shape
    ns = seq // tm
    row = lambda bb, i: (bb * ns + i, 0)
    col = lambda bb, i: (bb, 0, i)
    return pl.pallas_call(
        _outproj_body,
        grid=(batch, ns),
        in_specs=[pl.BlockSpec((tm, D_MODEL), row),
                  pl.BlockSpec((None, oat.shape[1], tm), col),
                  pl.BlockSpec((None, obt.shape[1], tm), col),
                  _const_spec(wo.shape),
                  _const_spec((1, D_MODEL)), _const_spec((1, D_MODEL))],
        out_specs=pl.BlockSpec((tm, D_MODEL), row),
        out_shape=jax.ShapeDtypeStruct((batch * seq, D_MODEL), F32),
        compiler_params=_params(2),
        name="out_proj_ln",
    )(x2d, oat, obt, wo, g.reshape(1, -1), b.reshape(1, -1))


def _xattn_body(x_ref, wq_ref, kv_ref, wo_ref, g_ref, b_ref, y_ref):
    x = x_ref[...]
    q = _dot(x.astype(MXU_DT), wq_ref[...]) * (XA_DH ** -0.5)
    hw = XA_HEADS * XA_DH
    y = jnp.zeros_like(x)
    for h in range(XA_HEADS):
        sl = slice(h * XA_DH, (h + 1) * XA_DH)
        s = _dot_nt(q[:, sl].astype(MXU_DT), kv_ref[:, sl])
        e = jnp.exp(s - jnp.max(s, -1, keepdims=True))
        p = e / jnp.sum(e, -1, keepdims=True)
        o = _dot(p.astype(MXU_DT), kv_ref[:, hw + h * XA_DH:hw + (h + 1) * XA_DH])
        y = y + _dot(o.astype(MXU_DT), wo_ref[sl, :])
    y_ref[...] = _ln_rows(ALPHA * x + y, g_ref[...], b_ref[...])


def _xattn(x2d, wq, memkv, wo, g, b, batch, seq, tm):
    ns = seq // tm
    mem = memkv.shape[0] // batch
    row = lambda bb, i: (bb * ns + i, 0)
    return pl.pallas_call(
        _xattn_body,
        grid=(batch, ns),
        in_specs=[pl.BlockSpec((tm, D_MODEL), row),
                  _const_spec(wq.shape),
                  pl.BlockSpec((mem, memkv.shape[1]), lambda bb, i: (bb, 0)),
                  _const_spec(wo.shape),
                  _const_spec((1, D_MODEL)), _const_spec((1, D_MODEL))],
        out_specs=pl.BlockSpec((tm, D_MODEL), row),
        out_shape=jax.ShapeDtypeStruct((batch * seq, D_MODEL), F32),
        compiler_params=_params(2),
        name="mem_xattn_ln",
    )(x2d, wq, memkv, wo, g.reshape(1, -1), b.reshape(1, -1))


_FF_CHUNK = D_FF
_HALO = 8


def _ffn_body(x_ref, xp_ref, wup_ref, cw_ref, cb_ref, wdn_ref, g_ref, b_ref, y_ref):
    first = pl.program_id(1) == 0
    x = x_ref[...]
    xp = jnp.where(first, 0.0, xp_ref[...])
    xe = jnp.concatenate([xp, x], axis=0).astype(MXU_DT)

    def conv(c0):
        u = _dot(xe, wup_ref[:, c0:c0 + _FF_CHUNK])
        u1 = pltpu.roll(u, 1, 0)
        u2 = pltpu.roll(u, 2, 0)
        cw = cw_ref[:, c0:c0 + _FF_CHUNK]
        r = u * cw[2:3, :] + u1 * cw[1:2, :] + u2 * cw[0:1, :] + cb_ref[:, c0:c0 + _FF_CHUNK]
        return r[_HALO:, :]

    acc = jnp.zeros_like(x)
    for c in range(D_FF // _FF_CHUNK):
        a = conv(c * _FF_CHUNK)
        gg = conv(D_FF + c * _FF_CHUNK)
        act = (a * jax.nn.silu(gg)).astype(MXU_DT)
        acc = acc + _dot(act, wdn_ref[c * _FF_CHUNK:(c + 1) * _FF_CHUNK, :])
    y_ref[...] = _ln_rows(ALPHA * x + acc, g_ref[...], b_ref[...])


def _ffn(x2d, wup, cw, cb, wdn, g, b, batch, seq, tm):
    ns = seq // tm
    hb = tm // _HALO
    row = lambda bb, i: (bb * ns + i, 0)
    prev = lambda bb, i: (jnp.maximum((bb * ns + i) * hb - 1, 0), 0)
    return pl.pallas_call(
        _ffn_body,
        grid=(batch, ns),
        in_specs=[pl.BlockSpec((tm, D_MODEL), row),
                  pl.BlockSpec((_HALO, D_MODEL), prev),
                  _const_spec(wup.shape), _const_spec(cw.shape), _const_spec(cb.shape),
                  _const_spec(wdn.shape),
                  _const_spec((1, D_MODEL)), _const_spec((1, D_MODEL))],
        out_specs=pl.BlockSpec((tm, D_MODEL), row),
        out_shape=jax.ShapeDtypeStruct((batch * seq, D_MODEL), F32),
        compiler_params=_params(2),
        name="conv_ffn_ln",
    )(x2d, x2d, wup, cw, cb, wdn, g.reshape(1, -1), b.reshape(1, -1))


def _rot_half_cols(w):
    half = MLA_ROPE // 2
    return jnp.concatenate([-w[..., half:], w[..., :half]], axis=-1)


def _prep_layer(w_in, cq_g, w_uq, ckv_g, w_ukv):
    d = w_in.shape[0]
    z = lambda n: jnp.zeros((d, n), F32)
    kr = w_in[:, _O_KR:_O_NQ]
    ngl = _IN_COLS - _O_GL
    w_ext = jnp.concatenate([
        w_in[:, _O_CQ:_O_CKV], w_in[:, _O_CKV:_O_KR],
        w_in[:, _O_NQ:_O_KV6] * (NSA_DH ** -0.5),
        w_in[:, _O_KV6:_O_GL],
        kr, w_in[:, _O_GL:_IN_COLS], z(_M_KROT - _M_GL - ngl),
        _rot_half_cols(kr), z(HP - _M_KROT - MLA_ROPE)], axis=1)
    wq = w_uq.reshape(MLA_Q_RANK, MLA_HEADS, MLA_NOPE + MLA_ROPE)
    zq = jnp.zeros((MLA_Q_RANK, MLA_HEADS, HP - MLA_NOPE - MLA_ROPE), F32)
    wqa = jnp.concatenate([wq, zq], axis=-1).reshape(MLA_Q_RANK, MLA_HEADS * HP)
    wqb = jnp.concatenate([jnp.zeros_like(wq[..., :MLA_NOPE]), _rot_half_cols(wq[..., MLA_NOPE:]), zq],
                          axis=-1).reshape(MLA_Q_RANK, MLA_HEADS * HP)
    wkv = w_ukv.reshape(MLA_KV_RANK, MLA_HEADS, MLA_NOPE + MLA_V)
    zk = jnp.zeros((MLA_KV_RANK, MLA_HEADS, HP - MLA_NOPE), F32)
    wkk = jnp.concatenate([wkv[..., :MLA_NOPE], zk], axis=-1).reshape(MLA_KV_RANK, MLA_HEADS * HP)
    wvv = wkv[..., MLA_NOPE:].reshape(MLA_KV_RANK, MLA_HEADS * MLA_V)
    return dict(w_in=w_ext.astype(MXU_DT), cq_g=cq_g.reshape(1, -1), ckv_g=ckv_g.reshape(1, -1),
                wqa=wqa.astype(MXU_DT), wqb=wqb.astype(MXU_DT),
                wkk=wkk.astype(MXU_DT), wkv=wvv.astype(MXU_DT))


def _rope_tabs(seq):
    inv = ROPE_BASE ** (-jnp.arange(0, MLA_ROPE, 2, dtype=F32) / MLA_ROPE)
    ang = jnp.arange(seq, dtype=F32)[:, None] * inv[None, :]
    cos, sin = jnp.cos(ang), jnp.sin(ang)
    pad = jnp.zeros((seq, HP - MLA_NOPE - MLA_ROPE), F32)
    tc = jnp.concatenate([jnp.ones((seq, MLA_NOPE), F32), cos, cos, pad], axis=1)
    ts = jnp.concatenate([jnp.zeros((seq, MLA_NOPE), F32), sin, sin, pad], axis=1)
    scale = (MLA_NOPE + MLA_ROPE) ** -0.5
    return dict(cq=tc * (scale * LOG2E), sq=ts * (scale * LOG2E), ck=tc, sk=ts)


def _pos_cols(pos, width, first):
    lane = jnp.arange(width)[None, :]
    p = pos[:, None]
    return jnp.where(lane == first, p // POS_SPLIT,
                     jnp.where(lane == first + 1, p % POS_SPLIT,
                               jnp.where(lane == first + 2, 1, 0))).astype(F32)


def _ones_row(n):
    return (jnp.arange(VROWS - NSA_DH)[:, None] == jnp.zeros((1, n), jnp.int32)).astype(MXU_DT)


def _nsa_consts(seq):
    nch = seq // CMP_STRIDE
    ns = seq // SEL_BLOCK
    c_start = jnp.arange(nch) * CMP_STRIDE
    s_start = jnp.arange(ns) * SEL_BLOCK
    ovl = jnp.clip(jnp.minimum(c_start[None, :] + CMP_BLOCK, s_start[:, None] + SEL_BLOCK)
                   - jnp.maximum(c_start[None, :], s_start[:, None]), 0).astype(F32) / CMP_BLOCK
    pos = jnp.arange(seq)
    onehot = (pos[:, None] // SEL_BLOCK == jnp.arange(ns)[None, :]).astype(MXU_DT)
    pool = (jnp.arange(ns)[:, None] // (NSA_TK // SEL_BLOCK) == jnp.arange(LANE)[None, :])
    slopes = 2.0 ** (-8.0 * jnp.arange(1, NSA_HEADS + 1, dtype=F32) / NSA_HEADS)
    return dict(ovl=ovl.astype(MXU_DT),
                onehot=onehot,
                posc=_pos_cols(pos, HP, NSA_DH),
                cendc=_pos_cols(c_start + CMP_BLOCK - 1, HP - NSA_DH, 0).astype(MXU_DT),
                pool=pool.astype(MXU_DT), slopes=slopes)


def kernel(x, mem, ln_in_g, ln_in_b, ln_mem_g, ln_mem_b, w_in, mla_cq_g, mla_w_uq, mla_ckv_g, mla_w_ukv, nsa_pe_k, nsa_w1_k, nsa_b1_k, nsa_w2_k, nsa_pe_v, nsa_w1_v, nsa_b1_v, nsa_w2_v, w_out, ln1_g, ln1_b, xa_wq, xa_wkv, xa_wo, ln2_g, ln2_b, ffn_w_up, ffn_conv_w, ffn_conv_b, ffn_w_down, ln3_g, ln3_b):
    batch, seq, d = x.shape
    t = batch * seq
    depth = w_in.shape[0]
    tm = min(TM, seq)
    nsa_tq, nsa_tk = min(NSA_TQ, seq), min(NSA_TK, seq)
    tabs = _rope_tabs(seq)
    consts = _nsa_consts(seq)
    nch = seq // CMP_STRIDE
    nkt = seq // nsa_tk
    bg = (batch, NSA_G)
    bcast = lambda a: jnp.broadcast_to(a, bg + a.shape)

    xs = _layer_norm(x.reshape(t, d), ln_in_g, ln_in_b, tm)
    memn = _layer_norm(mem.reshape(-1, d), ln_mem_g, ln_mem_b, mem.shape[0] * mem.shape[1])

    for l in range(depth):
        lw = _prep_layer(w_in[l], mla_cq_g[l], mla_w_uq[l], mla_ckv_g[l], mla_w_ukv[l])
        q, k, vt, nq, kcvc, kaug, vst, kwaug, vwt, gates = _inproj(xs, lw, tabs, consts, seq, tm)

        oa = _mla_attn(q, k, vt, batch, seq, min(MLA_T, seq))

        pe = jnp.tile(jnp.stack([nsa_pe_k[l], nsa_pe_v[l]]), (1, 1, NSA_G))
        w1 = jnp.stack([nsa_w1_k[l], nsa_w1_v[l]]).reshape(2, CMP_BLOCK, NSA_DH, CMP_HIDDEN)
        w1 = _head_blockdiag(w1).astype(MXU_DT)
        b1 = jnp.tile(jnp.stack([nsa_b1_k[l], nsa_b1_v[l]]).reshape(2, 1, CMP_HIDDEN), (1, 1, NSA_G))
        w2 = _head_blockdiag(jnp.stack([nsa_w2_k[l], nsa_w2_v[l]])).astype(MXU_DT)
        cmp = _compress(kcvc, pe, w1, b1, w2, batch, seq)
        cmp = cmp.reshape(2, batch, nch, NSA_G, NSA_DH).transpose(0, 1, 3, 2, 4)
        kcaug = jnp.concatenate([cmp[0], bcast(consts["cendc"])], axis=-1)
        vct = jnp.concatenate([cmp[1].transpose(0, 1, 3, 2), bcast(_ones_row(nch))], axis=2)
        oc_t, sel, cnt = _cmp_attn(consts["slopes"], nq, kcaug, vct, consts["ovl"], consts["pool"],
                                   batch, seq, min(CMP_TQ, seq))
        flags = (cnt[:, :, :, 0, :nkt] > 0).astype(jnp.int32).reshape(-1)
        gt = gates[:, _M_GL:_M_GL + 3 * NSA_HEADS].reshape(batch, seq, NSA_G, NSA_R, 3)
        gt = gt.transpose(0, 2, 4, 3, 1)
        ob = _nsa_attn(flags, consts["slopes"], nq, sel, kaug, vst, kwaug, vwt, oc_t, gt,
                       batch, seq, nsa_tq, nsa_tk)
        ob = ob.reshape(batch, NSA_HEADS * NSA_DH, seq)

        xs = _outproj(xs, oa, ob, w_out[l].astype(MXU_DT), ln1_g[l], ln1_b[l], tm)

        memkv = _matmul(memn, xa_wkv[l].astype(MXU_DT), 512, MXU_DT)
        xs = _xattn(xs, xa_wq[l].astype(MXU_DT), memkv, xa_wo[l].astype(MXU_DT),
                    ln2_g[l], ln2_b[l], batch, seq, tm)

        xs = _ffn(xs, ffn_w_up[l].astype(MXU_DT), ffn_conv_w[l], ffn_conv_b[l].reshape(1, -1),
                  ffn_w_down[l].astype(MXU_DT), ln3_g[l], ln3_b[l], batch, seq, tm)

    return xs.reshape(batch, seq, d)
```

```python
import functools

import jax
import jax.numpy as jnp
from jax import lax
from jax.experimental import pallas as pl
from jax.experimental.pallas import tpu as pltpu

F32 = jnp.float32
MXU_DT = jnp.bfloat16

D_MODEL = 1024
MLA_HEADS = 8
MLA_NOPE = 64
MLA_ROPE = 32
MLA_V = 64
MLA_Q_RANK = 384
MLA_KV_RANK = 256
ROPE_BASE = 10000.0
NSA_HEADS = 8
NSA_G = 2
NSA_R = 4
NSA_DH = 64
CMP_STRIDE = 16
CMP_BLOCK = 32
CMP_HIDDEN = 128
SEL_BLOCK = 64
N_SEL = 16
WINDOW = 512
XA_HEADS = 4
XA_DH = 256
D_FF = 2816
DEPTH = 2
ALPHA = (2.0 * DEPTH) ** 0.25
LN_EPS = 1e-5
RMS_EPS = 1e-6
NEG = -1e30

LANE = 128
HP = 128
VMEM_LIMIT = 56 * 1024 * 1024

TM = 512
MLA_T = 512
MLA_HPS = 4
CMP_TQ = 512
CMP_CH = 128
NSA_TQ = 256
NSA_TK = 256
NSA_U = 2
POS_SPLIT = 128
VROWS = 80
N_FORCED = 3
LOG2E = 1.4426950408889634

_O_CQ, _O_CKV, _O_KR, _O_NQ = 0, 384, 640, 672
_O_KV6, _O_GL, _IN_COLS = 1184, 1952, 1976
_E_CQ, _E_CKV, _E_NQ, _E_KCVC, _E_KS, _E_VS, _E_KW, _E_VW, _E_MISC, _E_END = (
    0, 384, 640, 1152, 1408, 1536, 1664, 1792, 1920, 2048)
_M_KR, _M_GL, _M_KROT = 0, 32, 64

_NT = (((1,), (1,)), ((), ()))


def _params(n_axes=1):
    return pltpu.CompilerParams(
        dimension_semantics=("arbitrary",) * n_axes,
        vmem_limit_bytes=VMEM_LIMIT)


def _const_spec(shape):
    nd = len(shape)
    return pl.BlockSpec(shape, lambda *_: (0,) * nd)


def _ln_rows(v, g, b):
    mu = jnp.mean(v, -1, keepdims=True)
    d = v - mu
    var = jnp.mean(d * d, -1, keepdims=True)
    return d * lax.rsqrt(var + LN_EPS) * g + b


def _dot(a, b):
    return jnp.dot(a, b, preferred_element_type=F32)


def _dot_nt(a, b):
    return lax.dot_general(a, b, _NT, preferred_element_type=F32)


def _ln_body(x_ref, g_ref, b_ref, o_ref):
    o_ref[...] = _ln_rows(x_ref[...], g_ref[...], b_ref[...])


def _layer_norm(x2d, g, b, tm):
    t, d = x2d.shape
    return pl.pallas_call(
        _ln_body,
        grid=(t // tm,),
        in_specs=[pl.BlockSpec((tm, d), lambda i: (i, 0)),
                  _const_spec((1, d)), _const_spec((1, d))],
        out_specs=pl.BlockSpec((tm, d), lambda i: (i, 0)),
        out_shape=jax.ShapeDtypeStruct((t, d), F32),
        compiler_params=_params(1),
        name="ln_rows",
    )(x2d, g.reshape(1, d), b.reshape(1, d))


def _matmul_body(a_ref, w_ref, o_ref):
    o_ref[...] = _dot(a_ref[...].astype(MXU_DT), w_ref[...]).astype(o_ref.dtype)


def _matmul(a, w, tn, out_dtype):
    m, k = a.shape
    n = w.shape[1]
    return pl.pallas_call(
        _matmul_body,
        grid=(n // tn,),
        in_specs=[_const_spec((m, k)), pl.BlockSpec((k, tn), lambda j: (0, j))],
        out_specs=pl.BlockSpec((m, tn), lambda j: (0, j)),
        out_shape=jax.ShapeDtypeStruct((m, n), out_dtype),
        compiler_params=_params(1),
        name="mem_kv_proj",
    )(a, w)


def _rms_rows(v, g):
    return v * lax.rsqrt(jnp.mean(v * v, -1, keepdims=True) + RMS_EPS) * g


def _inproj_body(x_ref, win_ref, cqg_ref, ckvg_ref, wqa_ref, wqb_ref, wkk_ref, wkv_ref,
                 tcq_ref, tsq_ref, tck_ref, tsk_ref, onehot_ref, posc_ref,
                 q_ref, k_ref, vt_ref, nq_ref, kcvc_ref, kaug_ref, vst_ref, kwaug_ref, vwt_ref,
                 gate_ref):
    xb = x_ref[...].astype(MXU_DT)
    h = _dot(xb, win_ref[...])
    cqn = _rms_rows(h[:, _E_CQ:_E_CKV], cqg_ref[...]).astype(MXU_DT)
    qa = _dot(cqn, wqa_ref[...])
    qb = _dot(cqn, wqb_ref[...])
    ckvn = _rms_rows(h[:, _E_CKV:_E_NQ], ckvg_ref[...]).astype(MXU_DT)
    kn = _dot(ckvn, wkk_ref[...])
    tail = jnp.where(lax.broadcasted_iota(jnp.int32, (VROWS - MLA_V, h.shape[0]), 0) == 0,
                     1.0, 0.0).astype(MXU_DT)
    vt = _dot(ckvn, wkv_ref[...]).T.astype(vt_ref.dtype)
    for hh in range(MLA_HEADS):
        vt_ref[hh * VROWS:hh * VROWS + MLA_V, :] = vt[hh * MLA_V:(hh + 1) * MLA_V, :]
        vt_ref[hh * VROWS + MLA_V:(hh + 1) * VROWS, :] = tail
    ns = onehot_ref.shape[1]
    posc = posc_ref[...]
    lane = lax.broadcasted_iota(jnp.int32, (h.shape[0], HP), 1)
    low = lane < NSA_DH

    def halves(blk):
        return jnp.where(low, blk, 0.0), jnp.where(low, pltpu.roll(blk, NSA_DH, 1), 0.0)

    ks = halves(h[:, _E_KS:_E_VS])
    kw = halves(h[:, _E_KW:_E_VW])
    vst = h[:, _E_VS:_E_KW].T.astype(MXU_DT)
    vwt = h[:, _E_VW:_E_MISC].T.astype(MXU_DT)
    for g in range(NSA_G):
        kaug_ref[g, :, 0:ns] = onehot_ref[...]
        kaug_ref[g, :, ns:ns + HP] = (ks[g] + posc).astype(MXU_DT)
        kwaug_ref[g] = (kw[g] + posc).astype(MXU_DT)
        for t_ref, tv in ((vst_ref, vst), (vwt_ref, vwt)):
            t_ref[g, 0:NSA_DH, :] = tv[g * NSA_DH:(g + 1) * NSA_DH, :]
            t_ref[g, NSA_DH:VROWS, :] = tail
    for j in range(NSA_HEADS // 2):
        pair = halves(h[:, _E_NQ + j * HP:_E_NQ + (j + 1) * HP])
        for i in range(2):
            nq_ref[:, (2 * j + i) * HP:(2 * j + i + 1) * HP] = pair[i].astype(nq_ref.dtype)
    misc = h[:, _E_MISC:_E_END]
    krr = (jnp.where(low, 0.0, pltpu.roll(misc, _M_KROT - _M_KR, 1)) * tck_ref[...]
           + misc * tsk_ref[...])
    tcq = tcq_ref[...]
    tsq = tsq_ref[...]
    for hh in range(MLA_HEADS):
        sl = slice(hh * HP, (hh + 1) * HP)
        q_ref[:, sl] = (qa[:, sl] * tcq + qb[:, sl] * tsq).astype(q_ref.dtype)
        k_ref[:, sl] = (kn[:, sl] + krr).astype(k_ref.dtype)
    kcvc_ref[...] = h[:, _E_KCVC:_E_KS]
    gate_ref[...] = jax.nn.sigmoid(misc)


def _inproj(x2d, lw, tabs, consts, seq, tm):
    t = x2d.shape[0]
    batch = t // seq
    npos = seq // tm
    ns = seq // SEL_BLOCK
    row = lambda i: (i, 0)
    pos = lambda i: (i % npos, 0)
    tok = lambda i: (i // npos, 0, i % npos, 0)
    ttr = lambda i: (i // npos, 0, 0, i % npos)
    hw = MLA_HEADS * HP
    vrows = MLA_HEADS * VROWS
    sds = jax.ShapeDtypeStruct
    outs = [
        (sds((t, hw), MXU_DT), pl.BlockSpec((tm, hw), row)),
        (sds((t, hw), MXU_DT), pl.BlockSpec((tm, hw), row)),
        (sds((batch, vrows, seq), MXU_DT),
         pl.BlockSpec((None, vrows, tm), lambda i: (i // npos, 0, i % npos))),
        (sds((t, NSA_HEADS * HP), MXU_DT), pl.BlockSpec((tm, NSA_HEADS * HP), row)),
        (sds((t, 256), F32), pl.BlockSpec((tm, 256), row)),
        (sds((batch, NSA_G, seq, ns + HP), MXU_DT), pl.BlockSpec((None, NSA_G, tm, ns + HP), tok)),
        (sds((batch, NSA_G, VROWS, seq), MXU_DT), pl.BlockSpec((None, NSA_G, VROWS, tm), ttr)),
        (sds((batch, NSA_G, seq, HP), MXU_DT), pl.BlockSpec((None, NSA_G, tm, HP), tok)),
        (sds((batch, NSA_G, VROWS, seq), MXU_DT), pl.BlockSpec((None, NSA_G, VROWS, tm), ttr)),
        (sds((t, LANE), F32), pl.BlockSpec((tm, LANE), row)),
    ]
    out_shape = [o[0] for o in outs]
    out_specs = [o[1] for o in outs]
    return pl.pallas_call(
        _inproj_body,
        grid=(t // tm,),
        in_specs=[pl.BlockSpec((tm, D_MODEL), row),
                  _const_spec((D_MODEL, _E_END)),
                  _const_spec((1, MLA_Q_RANK)), _const_spec((1, MLA_KV_RANK)),
                  _const_spec((MLA_Q_RANK, hw)), _const_spec((MLA_Q_RANK, hw)),
                  _const_spec((MLA_KV_RANK, hw)), _const_spec((MLA_KV_RANK, MLA_HEADS * MLA_V)),
                  pl.BlockSpec((tm, HP), pos), pl.BlockSpec((tm, HP), pos),
                  pl.BlockSpec((tm, HP), pos), pl.BlockSpec((tm, HP), pos),
                  pl.BlockSpec((tm, ns), pos), pl.BlockSpec((tm, HP), pos)],
        out_specs=out_specs,
        out_shape=out_shape,
        compiler_params=_params(1),
        name="in_proj",
    )(x2d, lw["w_in"], lw["cq_g"], lw["ckv_g"], lw["wqa"], lw["wqb"], lw["wkk"], lw["wkv"],
      tabs["cq"], tabs["sq"], tabs["ck"], tabs["sk"], consts["onehot"], consts["posc"])


def _mla_attn_body(q_ref, k_ref, vt_ref, o_ref, *, tq):
    qi = pl.program_id(2)

    tk2 = 2 * tq

    def step(off, nk, carry, diag_from):
        ss = []
        for h in range(MLA_HPS):
            hs = slice(h * HP, (h + 1) * HP)
            s = _dot_nt(k_ref[pl.ds(off, nk), hs], q_ref[:, hs])
            if diag_from is not None:
                kpos = lax.broadcasted_iota(jnp.int32, (nk, tq), 0) - diag_from
                qpos = lax.broadcasted_iota(jnp.int32, (nk, tq), 1)
                s = jnp.where(kpos <= qpos, s, NEG)
            ss.append(s)
        ms, alphas, ps = [], [], []
        for h in range(MLA_HPS):
            m = carry[h][0]
            m_new = jnp.maximum(m, jnp.max(ss[h], 0, keepdims=True))
            alphas.append(jnp.exp2(m - m_new))
            ps.append(jnp.exp2(ss[h] - m_new).astype(MXU_DT))
            ms.append(m_new)
        out = []
        for h in range(MLA_HPS):
            vt = vt_ref[h * VROWS:(h + 1) * VROWS, pl.ds(off, nk)]
            out.append((ms[h], alphas[h] * carry[h][1] + _dot(vt, ps[h])))
        return tuple(out)

    def finish(carry):
        for h in range(MLA_HPS):
            acc = carry[h][1]
            o_ref[h * MLA_V:(h + 1) * MLA_V, :] = (
                acc[0:MLA_V, :] / acc[MLA_V:MLA_V + 1, :]).astype(o_ref.dtype)

    one = (jnp.full((1, tq), NEG, F32), jnp.zeros((VROWS, tq), F32))
    carry = lax.fori_loop(
        0, qi // 2, lambda kp, c: step(pl.multiple_of(kp * tk2, tk2), tk2, c, None),
        (one,) * MLA_HPS)

    @pl.when(qi % 2 == 1)
    def _():
        finish(step(pl.multiple_of((qi - 1) * tq, tq), tk2, carry, tq))

    @pl.when(qi % 2 == 0)
    def _():
        finish(step(pl.multiple_of(qi * tq, tq), tq, carry, 0))


def _mla_attn(q, k, vt, batch, seq, tq):
    nq = seq // tq
    pw = MLA_HPS * HP
    return pl.pallas_call(
        functools.partial(_mla_attn_body, tq=tq),
        grid=(batch, MLA_HEADS // MLA_HPS, nq),
        in_specs=[pl.BlockSpec((tq, pw), lambda b, h, i: (b * nq + i, h)),
                  pl.BlockSpec((seq, pw), lambda b, h, i: (b, h)),
                  pl.BlockSpec((None, MLA_HPS * VROWS, seq), lambda b, h, i: (b, h, 0))],
        out_specs=pl.BlockSpec((None, MLA_HPS * MLA_V, tq), lambda b, h, i: (b, h, i)),
        out_shape=jax.ShapeDtypeStruct((batch, MLA_HEADS * MLA_V, seq), MXU_DT),
        compiler_params=_params(3),
        name="mla_attn",
    )(q, k, vt)


def _compress_body(c_ref, pe_ref, w1_ref, b1_ref, w2_ref, o_ref):
    nch = c_ref.shape[0] // CMP_STRIDE
    u = jnp.zeros((nch, NSA_G * CMP_HIDDEN), F32)
    w = jnp.zeros((nch, NSA_G * CMP_HIDDEN), F32)
    for p in range(CMP_STRIDE):
        x = c_ref[pl.ds(p, nch, stride=CMP_STRIDE), :]
        u = u + _dot((x + pe_ref[p:p + 1, :]).astype(MXU_DT), w1_ref[p])
        w = w + _dot((x + pe_ref[CMP_STRIDE + p:CMP_STRIDE + p + 1, :]).astype(MXU_DT),
                     w1_ref[CMP_STRIDE + p])
    w = pltpu.roll(w, nch - 1, 0)
    hid = jax.nn.gelu(u + w + b1_ref[...])
    o_ref[...] = _dot(hid.astype(MXU_DT), w2_ref[...]).astype(o_ref.dtype)


def _compress(kcvc, pe, w1, b1, w2, batch, seq):
    nch = seq // CMP_STRIDE
    gw = NSA_G * NSA_DH
    gh = NSA_G * CMP_HIDDEN
    return pl.pallas_call(
        _compress_body,
        grid=(2, batch),
        in_specs=[pl.BlockSpec((seq, gw), lambda s, j: (j, s)),
                  pl.BlockSpec((None, CMP_BLOCK, gw), lambda s, j: (s, 0, 0)),
                  pl.BlockSpec((None, CMP_BLOCK, gw, gh), lambda s, j: (s, 0, 0, 0)),
                  pl.BlockSpec((None, 1, gh), lambda s, j: (s, 0, 0)),
                  pl.BlockSpec((None, gh, gw), lambda s, j: (s, 0, 0))],
        out_specs=pl.BlockSpec((None, None, nch, gw), lambda s, j: (s, j, 0, 0)),
        out_shape=jax.ShapeDtypeStruct((2, batch, nch, gw), MXU_DT),
        compiler_params=_params(2),
        name="nsa_compress",
    )(kcvc, pe, w1, b1, w2)


def _head_blockdiag(w):
    eye = jnp.eye(NSA_G, dtype=w.dtype)
    out = jnp.einsum("gh,...ab->...gahb", eye, w)
    return out.reshape(w.shape[:-2] + (NSA_G * w.shape[-2], NSA_G * w.shape[-1]))


_BIG = 1e30


def _nsa_queries(q_ref, slope_ref, g, q0, tq, first_head):
    lane = lax.broadcasted_iota(jnp.int32, (tq, HP), 1)
    q0f = q0.astype(F32)
    qs = []
    for r in range(NSA_R):
        slope = slope_ref[g * NSA_R + r]
        c = jnp.where(lane == NSA_DH, slope * POS_SPLIT,
                      jnp.where(lane == NSA_DH + 1, slope,
                                jnp.where(lane == NSA_DH + 2, -slope * q0f, 0.0)))
        h0 = (first_head + r) * HP
        qs.append(q_ref[:, h0:h0 + HP] + c.astype(MXU_DT))
    return qs


def _cmp_attn_body(slope_ref, q_ref, kc_ref, vct_ref, ovl_ref, pool_ref,
                   oc_ref, sel_ref, flag_ref, s_scr, e_scr, *, tq, n_sel):
    g = pl.program_id(1)
    q0 = pl.program_id(2) * tq
    ns = sel_ref.shape[1]
    n = NSA_R * tq
    ch = min(CMP_CH, kc_ref.shape[0])
    qc = jnp.concatenate(_nsa_queries(q_ref, slope_ref, g, q0, tq, 0), axis=0)
    t_lane = q0 + (lax.broadcasted_iota(jnp.int32, (1, n), 1) & (tq - 1))
    cend0 = lax.broadcasted_iota(jnp.int32, (ch, n), 0) * CMP_STRIDE + (CMP_BLOCK - 1)
    nproc = jnp.maximum(q0 + tq - CMP_BLOCK, 0) // (ch * CMP_STRIDE) + 1

    def scores(c, m):
        off = pl.multiple_of(c * ch, ch)
        s = _dot_nt(kc_ref[pl.ds(off, ch), :], qc)
        s = jnp.where(cend0 <= t_lane - off * CMP_STRIDE, s, NEG)
        s_scr[pl.ds(off, ch), :] = s
        return jnp.maximum(m, jnp.max(s, 0, keepdims=True))

    m = lax.fori_loop(0, nproc, scores, jnp.full((1, n), NEG, F32))

    def weights(c, carry):
        off = pl.multiple_of(c * ch, ch)
        e_scr[pl.ds(off, ch), :] = jnp.exp(s_scr[pl.ds(off, ch), :] - m).astype(MXU_DT)
        return carry

    def clear(c, carry):
        e_scr[pl.ds(pl.multiple_of(c * ch, ch), ch), :] = jnp.zeros((ch, n), MXU_DT)
        return carry

    lax.fori_loop(0, nproc, weights, 0)
    lax.fori_loop(nproc, kc_ref.shape[0] // ch, clear, 0)
    e = e_scr[...]
    acc = _dot(vct_ref[...], e)
    has_key = t_lane >= CMP_BLOCK - 1
    inv = jnp.where(has_key, 1.0 / acc[NSA_DH:NSA_DH + 1, :], 0.0)
    impn = _dot(ovl_ref[...], e) * inv
    imp = impn[:, 0:tq]
    for r in range(NSA_R):
        oc_ref[r] = acc[0:NSA_DH, r * tq:(r + 1) * tq] * inv[:, r * tq:(r + 1) * tq]
        if r:
            imp = imp + impn[:, r * tq:(r + 1) * tq]

    blk = lax.broadcasted_iota(jnp.int32, (ns, tq), 0)
    cur = (q0 + lax.broadcasted_iota(jnp.int32, (ns, tq), 1)) // SEL_BLOCK
    blkf = blk.astype(F32)
    forced = (blk == 0) | (blk == cur) | (blk == cur - 1)
    score0 = jnp.where((blk <= cur) & jnp.logical_not(forced), imp, -1.0)
    score = score0
    for _ in range(n_sel - N_FORCED):
        mx = jnp.max(score, 0, keepdims=True)
        first = jnp.min(jnp.where(score == mx, blkf, float(ns)), 0, keepdims=True)
        score = jnp.where(blkf == first, -1.0, score)
    sel = jnp.where(forced, 1.0, jnp.where(score != score0, 1.0, 0.0))
    sel_ref[...] = sel.T.astype(sel_ref.dtype)
    ones = jnp.ones((8, NSA_TQ), MXU_DT)
    selb = sel.astype(MXU_DT)
    for j in range(tq // NSA_TQ):
        cnt = _dot_nt(ones, selb[:, j * NSA_TQ:(j + 1) * NSA_TQ])
        flag_ref[j] = _dot(cnt.astype(MXU_DT), pool_ref[...])


def _cmp_attn(slopes, nq, kc, vct, ovl, pool, batch, seq, tq):
    nq_t = seq // tq
    nch = kc.shape[-2]
    ns = seq // SEL_BLOCK
    n_sel = min(N_SEL, ns)
    sub = tq // NSA_TQ
    gw = NSA_R * HP
    n = NSA_R * tq
    return pl.pallas_call(
        functools.partial(_cmp_attn_body, tq=tq, n_sel=n_sel),
        grid=(batch, NSA_G, nq_t),
        in_specs=[pl.BlockSpec(memory_space=pltpu.SMEM),
                  pl.BlockSpec((tq, gw), lambda b, g, i: (b * nq_t + i, g)),
                  pl.BlockSpec((None, None, nch, HP), lambda b, g, i: (b, g, 0, 0)),
                  pl.BlockSpec((None, None, VROWS, nch), lambda b, g, i: (b, g, 0, 0)),
                  _const_spec(ovl.shape), _const_spec(pool.shape)],
        out_specs=[pl.BlockSpec((None, None, NSA_R, NSA_DH, tq), lambda b, g, i: (b, g, 0, 0, i)),
                   pl.BlockSpec((None, None, tq, ns), lambda b, g, i: (b, g, i, 0)),
                   pl.BlockSpec((None, None, sub, 8, LANE), lambda b, g, i: (b, g, i, 0, 0))],
        out_shape=[jax.ShapeDtypeStruct((batch, NSA_G, NSA_R, NSA_DH, seq), F32),
                   jax.ShapeDtypeStruct((batch, NSA_G, seq, ns), MXU_DT),
                   jax.ShapeDtypeStruct((batch, NSA_G, seq // NSA_TQ, 8, LANE), F32)],
        scratch_shapes=[pltpu.VMEM((nch, n), F32), pltpu.VMEM((nch, n), MXU_DT)],
        compiler_params=_params(3),
        name="nsa_cmp_attn_topk",
    )(slopes, nq, kc, vct, ovl, pool)


def _nsa_attn_body(flag_ref, slope_ref, q_ref, sel_ref, kaug_ref, vst_ref, kwaug_ref, vwt_ref,
                   oct_ref, gt_ref, o_ref, list0_scr, list1_scr, os_scr, ow_scr, *, tq, tk, nkt):
    b = pl.program_id(0)
    qi = pl.program_id(1)
    q0 = qi * tq
    n = NSA_R * tq
    wk = WINDOW + tq
    lists = (list0_scr, list1_scr)
    heads = range(NSA_G)

    t_loc = lax.broadcasted_iota(jnp.int32, (1, n), 1) & (tq - 1)
    t_lane = q0 + t_loc
    kd = q0 // tk

    qa, qc, cnt = [], [], []
    for g in heads:
        selneg = ((sel_ref[g].astype(F32) - 1.0) * _BIG).astype(MXU_DT)
        qs = _nsa_queries(q_ref, slope_ref, g, q0, tq, g * NSA_R)
        qc.append(jnp.concatenate(qs, axis=0))
        qa.append(jnp.concatenate([jnp.concatenate([selneg, qr], axis=1) for qr in qs], axis=0))
        fbase = ((b * NSA_G + g) * pl.num_programs(1) + qi) * nkt

        def compact(kt, c, fbase=fbase, lst=lists[g]):
            act = flag_ref[fbase + kt] > 0

            @pl.when(act)
            def _():
                lst[c] = kt
            return c + act.astype(jnp.int32)

        c = lax.fori_loop(0, kd, compact, 0)
        for j in range(NSA_U - 1):
            lists[g][c + j] = kd
        cnt.append(c)
    nfull = [c // NSA_U for c in cnt]

    def scores(g, idx):
        offs = [pl.multiple_of(i * tk, tk) for i in idx]
        return [_dot_nt(kaug_ref[g, pl.ds(off, tk), :], qa[g]) for off in offs], offs

    def soft(ss, m_old):
        m_new = m_old
        for s in ss:
            m_new = jnp.maximum(m_new, jnp.max(s, 0, keepdims=True))
        return m_new, jnp.exp(m_old - m_new), [jnp.exp(s - m_new).astype(MXU_DT) for s in ss]

    def pv(g, ps, offs, alpha, acc):
        acc = alpha * acc
        for p, off in zip(ps, offs):
            acc = acc + _dot(vst_ref[g, :, pl.ds(off, tk)], p)
        return acc

    def groups(gs, gi, carry):
        sc = [scores(g, [lists[g][gi * NSA_U + j] for j in range(NSA_U)]) for g in gs]
        sm = [soft(sc[k][0], carry[k][0]) for k in range(len(gs))]
        return tuple((sm[k][0], pv(g, sm[k][2], sc[k][1], sm[k][1], carry[k][1]))
                     for k, g in enumerate(gs))

    init = (jnp.full((1, n), NEG, F32), jnp.zeros((VROWS, n), F32))
    nboth = jnp.minimum(nfull[0], nfull[1])
    carry = lax.fori_loop(0, nboth, lambda gi, c: groups((0, 1), gi, c), (init, init))
    carry = [lax.fori_loop(nboth, nfull[g], lambda gi, c, g=g: groups((g,), gi, (c,))[0], carry[g])
             for g in heads]

    k_loc = lax.broadcasted_iota(jnp.int32, (tq, n), 0)

    def tail(start, banded):
        sc = [scores(g, [lists[g][nfull[g] * NSA_U + j] for j in range(NSA_U - 1)] + [kd])
              for g in heads]
        sw = [_dot_nt(kwaug_ref[g, pl.ds(start, wk), :], qc[g]) for g in heads]
        parts = []
        for g in heads:
            ss, offs = sc[g]
            for j in range(NSA_U - 1):
                ss[j] = jnp.where(nfull[g] * NSA_U + j < cnt[g], ss[j], NEG)
            kpos = offs[-1] + lax.broadcasted_iota(jnp.int32, (tk, n), 0)
            ss[-1] = jnp.where(kpos <= t_lane, ss[-1], NEG)
            if banded:
                parts.append([jnp.where(k_loc > t_loc, sw[g][0:tq], NEG), sw[g][tq:WINDOW],
                              jnp.where(k_loc <= t_loc, sw[g][WINDOW:wk], NEG)])
            else:
                dist = t_lane - (start + lax.broadcasted_iota(jnp.int32, (wk, n), 0))
                parts.append([jnp.where((dist >= 0) & (dist < WINDOW), sw[g], NEG)])
        sm, es = [], []
        for g in heads:
            sm.append(soft(sc[g][0], carry[g][0]))
            mw = jnp.max(parts[g][0], 0, keepdims=True)
            for p in parts[g][1:]:
                mw = jnp.maximum(mw, jnp.max(p, 0, keepdims=True))
            es.append(jnp.concatenate([jnp.exp(p - mw).astype(MXU_DT) for p in parts[g]], axis=0))
        for g in heads:
            acc = pv(g, sm[g][2], sc[g][1], sm[g][1], carry[g][1])
            ow = _dot(vwt_ref[g, :, pl.ds(start, wk)], es[g])
            os_scr[g] = acc[0:NSA_DH, :] / acc[NSA_DH:NSA_DH + 1, :]
            ow_scr[g] = ow[0:NSA_DH, :] / ow[NSA_DH:NSA_DH + 1, :]

    @pl.when(q0 >= WINDOW)
    def _():
        tail(pl.multiple_of(q0 - WINDOW, LANE), True)

    @pl.when(q0 < WINDOW)
    def _():
        tail(0, False)

    for g in heads:
        o_sel = os_scr[g]
        o_win = ow_scr[g]
        for r in range(NSA_R):
            ls = slice(r * tq, (r + 1) * tq)
            o_ref[g, r] = (gt_ref[g, 0, r:r + 1, :] * oct_ref[g, r]
                           + gt_ref[g, 1, r:r + 1, :] * o_sel[:, ls]
                           + gt_ref[g, 2, r:r + 1, :] * o_win[:, ls]).astype(o_ref.dtype)


def _nsa_attn(flags, slopes, nq, sel, kaug, vst, kwaug, vwt, oct_, gt, batch, seq, tq, tk):
    nq_t = seq // tq
    ns = seq // SEL_BLOCK
    nkt = seq // tk
    n = NSA_R * tq
    once = dict(pipeline_mode=pl.Buffered(1))
    res = lambda b, i, f: (b, 0, 0, 0)
    til = lambda b, i, f: (b, 0, 0, 0, i)
    grid_spec = pltpu.PrefetchScalarGridSpec(
        num_scalar_prefetch=1,
        grid=(batch, nq_t),
        in_specs=[pl.BlockSpec(memory_space=pltpu.SMEM),
                  pl.BlockSpec((tq, NSA_HEADS * HP), lambda b, i, f: (b * nq_t + i, 0)),
                  pl.BlockSpec((None, NSA_G, tq, ns), lambda b, i, f: (b, 0, i, 0)),
                  pl.BlockSpec((None, NSA_G, seq, ns + HP), res, **once),
                  pl.BlockSpec((None, NSA_G, VROWS, seq), res, **once),
                  pl.BlockSpec((None, NSA_G, seq, HP), res, **once),
                  pl.BlockSpec((None, NSA_G, VROWS, seq), res, **once),
                  pl.BlockSpec((None, NSA_G, NSA_R, NSA_DH, tq), til),
                  pl.BlockSpec((None, NSA_G, 3, NSA_R, tq), til)],
        out_specs=pl.BlockSpec((None, NSA_G, NSA_R, NSA_DH, tq), til),
        scratch_shapes=[pltpu.SMEM((nkt + NSA_U,), jnp.int32), pltpu.SMEM((nkt + NSA_U,), jnp.int32),
                        pltpu.VMEM((NSA_G, NSA_DH, n), F32), pltpu.VMEM((NSA_G, NSA_DH, n), F32)])
    return pl.pallas_call(
        functools.partial(_nsa_attn_body, tq=tq, tk=tk, nkt=nkt),
        grid_spec=grid_spec,
        out_shape=jax.ShapeDtypeStruct((batch, NSA_G, NSA_R, NSA_DH, seq), MXU_DT),
        compiler_params=_params(2),
        name="nsa_sel_win_attn",
    )(flags, slopes, nq, sel, kaug, vst, kwaug, vwt, oct_, gt)


_TN = (((0,), (0,)), ((), ()))


def _outproj_body(x_ref, oa_ref, ob_ref, wo_ref, g_ref, b_ref, y_ref):
    wa = oa_ref.shape[0]
    mix = lax.dot_general(oa_ref[...].astype(MXU_DT), wo_ref[0:wa, :], _TN,
                          preferred_element_type=F32)
    mix = mix + lax.dot_general(ob_ref[...].astype(MXU_DT), wo_ref[wa:, :], _TN,
                                preferred_element_type=F32)
    y_ref[...] = _ln_rows(ALPHA * x_ref[...] + mix, g_ref[...], b_ref[...])


def _outproj(x2d, oat, obt, wo, g, b, tm):
    batch, _, seq = oat.shape
    ns = seq // tm
    row = lambda bb, i: (bb * ns + i, 0)
    col = lambda bb, i: (bb, 0, i)
    return pl.pallas_call(
        _outproj_body,
        grid=(batch, ns),
        in_specs=[pl.BlockSpec((tm, D_MODEL), row),
                  pl.BlockSpec((None, oat.shape[1], tm), col),
                  pl.BlockSpec((None, obt.shape[1], tm), col),
                  _const_spec(wo.shape),
                  _const_spec((1, D_MODEL)), _const_spec((1, D_MODEL))],
        out_specs=pl.BlockSpec((tm, D_MODEL), row),
        out_shape=jax.ShapeDtypeStruct((batch * seq, D_MODEL), F32),
        compiler_params=_params(2),
        name="out_proj_ln",
    )(x2d, oat, obt, wo, g.reshape(1, -1), b.reshape(1, -1))


def _xattn_body(x_ref, wq_ref, kv_ref, wo_ref, g_ref, b_ref, y_ref):
    x = x_ref[...]
    q = (_dot(x.astype(MXU_DT), wq_ref[...]) * (XA_DH ** -0.5)).astype(MXU_DT)
    hw = XA_HEADS * XA_DH
    sls = [slice(h * XA_DH, (h + 1) * XA_DH) for h in range(XA_HEADS)]
    ss = [_dot_nt(q[:, sl], kv_ref[:, sl]) for sl in sls]
    ps = []
    for s in ss:
        e = jnp.exp(s - jnp.max(s, -1, keepdims=True))
        ps.append((e / jnp.sum(e, -1, keepdims=True)).astype(MXU_DT))
    o = [_dot(p, kv_ref[:, hw + h * XA_DH:hw + (h + 1) * XA_DH]).astype(MXU_DT)
         for h, p in enumerate(ps)]
    y = _dot(jnp.concatenate(o, axis=1), wo_ref[...])
    y_ref[...] = _ln_rows(ALPHA * x + y, g_ref[...], b_ref[...])


def _xattn(x2d, wq, memkv, wo, g, b, batch, seq, tm):
    ns = seq // tm
    mem = memkv.shape[0] // batch
    row = lambda bb, i: (bb * ns + i, 0)
    return pl.pallas_call(
        _xattn_body,
        grid=(batch, ns),
        in_specs=[pl.BlockSpec((tm, D_MODEL), row),
                  _const_spec(wq.shape),
                  pl.BlockSpec((mem, memkv.shape[1]), lambda bb, i: (bb, 0)),
                  _const_spec(wo.shape),
                  _const_spec((1, D_MODEL)), _const_spec((1, D_MODEL))],
        out_specs=pl.BlockSpec((tm, D_MODEL), row),
        out_shape=jax.ShapeDtypeStruct((batch * seq, D_MODEL), F32),
        compiler_params=_params(2),
        name="mem_xattn_ln",
    )(x2d, wq, memkv, wo, g.reshape(1, -1), b.reshape(1, -1))


_FF_CHUNK = D_FF
_HALO = 8


def _ffn_body(x_ref, xp_ref, wup_ref, cw_ref, cb_ref, wdn_ref, g_ref, b_ref, y_ref):
    first = pl.program_id(1) == 0
    x = x_ref[...]
    xp = jnp.where(first, 0.0, xp_ref[...])
    xe = jnp.concatenate([xp, x], axis=0).astype(MXU_DT)

    def conv(c0):
        u = _dot(xe, wup_ref[:, c0:c0 + _FF_CHUNK])
        u1 = pltpu.roll(u, 1, 0)
        u2 = pltpu.roll(u, 2, 0)
        cw = cw_ref[:, c0:c0 + _FF_CHUNK]
        r = u * cw[2:3, :] + u1 * cw[1:2, :] + u2 * cw[0:1, :] + cb_ref[:, c0:c0 + _FF_CHUNK]
        return r[_HALO:, :]

    acc = jnp.zeros_like(x)
    for c in range(D_FF // _FF_CHUNK):
        a = conv(c * _FF_CHUNK)
        gg = conv(D_FF + c * _FF_CHUNK)
        act = (a * jax.nn.silu(gg)).astype(MXU_DT)
        acc = acc + _dot(act, wdn_ref[c * _FF_CHUNK:(c + 1) * _FF_CHUNK, :])
    y_ref[...] = _ln_rows(ALPHA * x + acc, g_ref[...], b_ref[...])


def _ffn(x2d, wup, cw, cb, wdn, g, b, batch, seq, tm):
    ns = seq // tm
    hb = tm // _HALO
    row = lambda bb, i: (bb * ns + i, 0)
    prev = lambda bb, i: (jnp.maximum((bb * ns + i) * hb - 1, 0), 0)
    return pl.pallas_call(
        _ffn_body,
        grid=(batch, ns),
        in_specs=[pl.BlockSpec((tm, D_MODEL), row),
                  pl.BlockSpec((_HALO, D_MODEL), prev),
                  _const_spec(wup.shape), _const_spec(cw.shape), _const_spec(cb.shape),
                  _const_spec(wdn.shape),
                  _const_spec((1, D_MODEL)), _const_spec((1, D_MODEL))],
        out_specs=pl.BlockSpec((tm, D_MODEL), row),
        out_shape=jax.ShapeDtypeStruct((batch * seq, D_MODEL), F32),
        compiler_params=_params(2),
        name="conv_ffn_ln",
    )(x2d, x2d, wup, cw, cb, wdn, g.reshape(1, -1), b.reshape(1, -1))


def _rot_half_cols(w):
    half = MLA_ROPE // 2
    return jnp.concatenate([-w[..., half:], w[..., :half]], axis=-1)


def _prep_layer(w_in, cq_g, w_uq, ckv_g, w_ukv):
    d = w_in.shape[0]
    z = lambda n: jnp.zeros((d, n), F32)
    kr = w_in[:, _O_KR:_O_NQ]
    ngl = _IN_COLS - _O_GL
    w_ext = jnp.concatenate([
        w_in[:, _O_CQ:_O_CKV], w_in[:, _O_CKV:_O_KR],
        w_in[:, _O_NQ:_O_KV6] * (NSA_DH ** -0.5),
        w_in[:, _O_KV6:_O_GL],
        kr, w_in[:, _O_GL:_IN_COLS], z(_M_KROT - _M_GL - ngl),
        _rot_half_cols(kr), z(HP - _M_KROT - MLA_ROPE)], axis=1)
    wq = w_uq.reshape(MLA_Q_RANK, MLA_HEADS, MLA_NOPE + MLA_ROPE)
    zq = jnp.zeros((MLA_Q_RANK, MLA_HEADS, HP - MLA_NOPE - MLA_ROPE), F32)
    wqa = jnp.concatenate([wq, zq], axis=-1).reshape(MLA_Q_RANK, MLA_HEADS * HP)
    wqb = jnp.concatenate([jnp.zeros_like(wq[..., :MLA_NOPE]), _rot_half_cols(wq[..., MLA_NOPE:]), zq],
                          axis=-1).reshape(MLA_Q_RANK, MLA_HEADS * HP)
    wkv = w_ukv.reshape(MLA_KV_RANK, MLA_HEADS, MLA_NOPE + MLA_V)
    zk = jnp.zeros((MLA_KV_RANK, MLA_HEADS, HP - MLA_NOPE), F32)
    wkk = jnp.concatenate([wkv[..., :MLA_NOPE], zk], axis=-1).reshape(MLA_KV_RANK, MLA_HEADS * HP)
    wvv = wkv[..., MLA_NOPE:].reshape(MLA_KV_RANK, MLA_HEADS * MLA_V)
    return dict(w_in=w_ext.astype(MXU_DT), cq_g=cq_g.reshape(1, -1), ckv_g=ckv_g.reshape(1, -1),
                wqa=wqa.astype(MXU_DT), wqb=wqb.astype(MXU_DT),
                wkk=wkk.astype(MXU_DT), wkv=wvv.astype(MXU_DT))


def _rope_tabs(seq):
    inv = ROPE_BASE ** (-jnp.arange(0, MLA_ROPE, 2, dtype=F32) / MLA_ROPE)
    ang = jnp.arange(seq, dtype=F32)[:, None] * inv[None, :]
    cos, sin = jnp.cos(ang), jnp.sin(ang)
    pad = jnp.zeros((seq, HP - MLA_NOPE - MLA_ROPE), F32)
    tc = jnp.concatenate([jnp.ones((seq, MLA_NOPE), F32), cos, cos, pad], axis=1)
    ts = jnp.concatenate([jnp.zeros((seq, MLA_NOPE), F32), sin, sin, pad], axis=1)
    scale = (MLA_NOPE + MLA_ROPE) ** -0.5
    return dict(cq=tc * (scale * LOG2E), sq=ts * (scale * LOG2E), ck=tc, sk=ts)


def _pos_cols(pos, width, first):
    lane = jnp.arange(width)[None, :]
    p = pos[:, None]
    return jnp.where(lane == first, p // POS_SPLIT,
                     jnp.where(lane == first + 1, p % POS_SPLIT,
                               jnp.where(lane == first + 2, 1, 0))).astype(F32)


def _ones_row(n):
    return (jnp.arange(VROWS - NSA_DH)[:, None] == jnp.zeros((1, n), jnp.int32)).astype(MXU_DT)


def _nsa_consts(seq):
    nch = seq // CMP_STRIDE
    ns = seq // SEL_BLOCK
    c_start = jnp.arange(nch) * CMP_STRIDE
    s_start = jnp.arange(ns) * SEL_BLOCK
    ovl = jnp.clip(jnp.minimum(c_start[None, :] + CMP_BLOCK, s_start[:, None] + SEL_BLOCK)
                   - jnp.maximum(c_start[None, :], s_start[:, None]), 0).astype(F32) / CMP_BLOCK
    pos = jnp.arange(seq)
    onehot = (pos[:, None] // SEL_BLOCK == jnp.arange(ns)[None, :]).astype(MXU_DT)
    pool = (jnp.arange(ns)[:, None] // (NSA_TK // SEL_BLOCK) == jnp.arange(LANE)[None, :])
    slopes = 2.0 ** (-8.0 * jnp.arange(1, NSA_HEADS + 1, dtype=F32) / NSA_HEADS)
    return dict(ovl=ovl.astype(MXU_DT),
                onehot=onehot,
                posc=_pos_cols(pos, HP, NSA_DH),
                cendc=_pos_cols(c_start + CMP_BLOCK - 1, HP - NSA_DH, 0).astype(MXU_DT),
                pool=pool.astype(MXU_DT), slopes=slopes)


def kernel(x, mem, ln_in_g, ln_in_b, ln_mem_g, ln_mem_b, w_in, mla_cq_g, mla_w_uq, mla_ckv_g, mla_w_ukv, nsa_pe_k, nsa_w1_k, nsa_b1_k, nsa_w2_k, nsa_pe_v, nsa_w1_v, nsa_b1_v, nsa_w2_v, w_out, ln1_g, ln1_b, xa_wq, xa_wkv, xa_wo, ln2_g, ln2_b, ffn_w_up, ffn_conv_w, ffn_conv_b, ffn_w_down, ln3_g, ln3_b):
    batch, seq, d = x.shape
    t = batch * seq
    depth = w_in.shape[0]
    tm = min(TM, seq)
    nsa_tq, nsa_tk = min(NSA_TQ, seq), min(NSA_TK, seq)
    tabs = _rope_tabs(seq)
    consts = _nsa_consts(seq)
    nch = seq // CMP_STRIDE
    nkt = seq // nsa_tk
    bg = (batch, NSA_G)
    bcast = lambda a: jnp.broadcast_to(a, bg + a.shape)

    xs = _layer_norm(x.reshape(t, d), ln_in_g, ln_in_b, tm)
    memn = _layer_norm(mem.reshape(-1, d), ln_mem_g, ln_mem_b, mem.shape[0] * mem.shape[1])

    for l in range(depth):
        lw = _prep_layer(w_in[l], mla_cq_g[l], mla_w_uq[l], mla_ckv_g[l], mla_w_ukv[l])
        q, k, vt, nq, kcvc, kaug, vst, kwaug, vwt, gates = _inproj(xs, lw, tabs, consts, seq, tm)

        oa = _mla_attn(q, k, vt, batch, seq, min(MLA_T, seq))

        pe = jnp.tile(jnp.stack([nsa_pe_k[l], nsa_pe_v[l]]), (1, 1, NSA_G))
        w1 = jnp.stack([nsa_w1_k[l], nsa_w1_v[l]]).reshape(2, CMP_BLOCK, NSA_DH, CMP_HIDDEN)
        w1 = _head_blockdiag(w1).astype(MXU_DT)
        b1 = jnp.tile(jnp.stack([nsa_b1_k[l], nsa_b1_v[l]]).reshape(2, 1, CMP_HIDDEN), (1, 1, NSA_G))
        w2 = _head_blockdiag(jnp.stack([nsa_w2_k[l], nsa_w2_v[l]])).astype(MXU_DT)
        cmp = _compress(kcvc, pe, w1, b1, w2, batch, seq)
        cmp = cmp.reshape(2, batch, nch, NSA_G, NSA_DH).transpose(0, 1, 3, 2, 4)
        kcaug = jnp.concatenate([cmp[0], bcast(consts["cendc"])], axis=-1)
        vct = jnp.concatenate([cmp[1].transpose(0, 1, 3, 2), bcast(_ones_row(nch))], axis=2)
        oc_t, sel, cnt = _cmp_attn(consts["slopes"], nq, kcaug, vct, consts["ovl"], consts["pool"],
                                   batch, seq, min(CMP_TQ, seq))
        flags = (cnt[:, :, :, 0, :nkt] > 0).astype(jnp.int32).reshape(-1)
        gt = gates[:, _M_GL:_M_GL + 3 * NSA_HEADS].reshape(batch, seq, NSA_G, NSA_R, 3)
        gt = gt.transpose(0, 2, 4, 3, 1)
        ob = _nsa_attn(flags, consts["slopes"], nq, sel, kaug, vst, kwaug, vwt, oc_t, gt,
                       batch, seq, nsa_tq, nsa_tk)
        ob = ob.reshape(batch, NSA_HEADS * NSA_DH, seq)

        xs = _outproj(xs, oa, ob, w_out[l].astype(MXU_DT), ln1_g[l], ln1_b[l], tm)

        memkv = _matmul(memn, xa_wkv[l].astype(MXU_DT), 512, MXU_DT)
        xs = _xattn(xs, xa_wq[l].astype(MXU_DT), memkv, xa_wo[l].astype(MXU_DT),
                    ln2_g[l], ln2_b[l], batch, seq, tm)

        xs = _ffn(xs, ffn_w_up[l].astype(MXU_DT), ffn_conv_w[l], ffn_conv_b[l].reshape(1, -1),
                  ffn_w_down[l].astype(MXU_DT), ln3_g[l], ln3_b[l], batch, seq, tm)

    return xs.reshape(batch, seq, d)
```

```python
import functools

import jax
import jax.numpy as jnp
from jax import lax
from jax.experimental import pallas as pl
from jax.experimental.pallas import tpu as pltpu

F32 = jnp.float32
MXU_DT = jnp.bfloat16

D_MODEL = 1024
MLA_HEADS = 8
MLA_NOPE = 64
MLA_ROPE = 32
MLA_V = 64
MLA_Q_RANK = 384
MLA_KV_RANK = 256
ROPE_BASE = 10000.0
NSA_HEADS = 8
NSA_G = 2
NSA_R = 4
NSA_DH = 64
CMP_STRIDE = 16
CMP_BLOCK = 32
CMP_HIDDEN = 128
SEL_BLOCK = 64
N_SEL = 16
WINDOW = 512
XA_HEADS = 4
XA_DH = 256
D_FF = 2816
DEPTH = 2
ALPHA = (2.0 * DEPTH) ** 0.25
LN_EPS = 1e-5
RMS_EPS = 1e-6
NEG = -1e30

LANE = 128
HP = 128
VMEM_LIMIT = 56 * 1024 * 1024

TM = 512
MLA_T = 512
MLA_HPS = 4
CMP_TQ = 1024
CMP_CH = 128
NSA_TQ = 256
NSA_TK = 256
NSA_U = 2
POS_SPLIT = 128
VROWS = 80
N_FORCED = 3
LOG2E = 1.4426950408889634

_O_CQ, _O_CKV, _O_KR, _O_NQ = 0, 384, 640, 672
_O_KV6, _O_GL, _IN_COLS = 1184, 1952, 1976
_E_CQ, _E_CKV, _E_NQ, _E_KCVC, _E_KS, _E_VS, _E_KW, _E_VW, _E_MISC, _E_END = (
    0, 384, 640, 1152, 1408, 1536, 1664, 1792, 1920, 2048)
_M_KR, _M_GL, _M_KROT = 0, 32, 64

_NT = (((1,), (1,)), ((), ()))


def _params(n_axes=1):
    return pltpu.CompilerParams(
        dimension_semantics=("arbitrary",) * n_axes,
        vmem_limit_bytes=VMEM_LIMIT)


def _const_spec(shape):
    nd = len(shape)
    return pl.BlockSpec(shape, lambda *_: (0,) * nd)


def _ln_rows(v, g, b):
    mu = jnp.mean(v, -1, keepdims=True)
    d = v - mu
    var = jnp.mean(d * d, -1, keepdims=True)
    return d * lax.rsqrt(var + LN_EPS) * g + b


def _dot(a, b):
    return jnp.dot(a, b, preferred_element_type=F32)


def _dot_nt(a, b):
    return lax.dot_general(a, b, _NT, preferred_element_type=F32)


def _ln_body(x_ref, g_ref, b_ref, o_ref):
    o_ref[...] = _ln_rows(x_ref[...], g_ref[...], b_ref[...])


def _layer_norm(x2d, g, b, tm):
    t, d = x2d.shape
    return pl.pallas_call(
        _ln_body,
        grid=(t // tm,),
        in_specs=[pl.BlockSpec((tm, d), lambda i: (i, 0)),
                  _const_spec((1, d)), _const_spec((1, d))],
        out_specs=pl.BlockSpec((tm, d), lambda i: (i, 0)),
        out_shape=jax.ShapeDtypeStruct((t, d), F32),
        compiler_params=_params(1),
        name="ln_rows",
    )(x2d, g.reshape(1, d), b.reshape(1, d))


def _matmul_body(a_ref, w_ref, o_ref):
    o_ref[...] = _dot(a_ref[...].astype(MXU_DT), w_ref[...]).astype(o_ref.dtype)


def _matmul(a, w, tn, out_dtype):
    m, k = a.shape
    n = w.shape[1]
    return pl.pallas_call(
        _matmul_body,
        grid=(n // tn,),
        in_specs=[_const_spec((m, k)), pl.BlockSpec((k, tn), lambda j: (0, j))],
        out_specs=pl.BlockSpec((m, tn), lambda j: (0, j)),
        out_shape=jax.ShapeDtypeStruct((m, n), out_dtype),
        compiler_params=_params(1),
        name="mem_kv_proj",
    )(a, w)


def _rms_rows(v, g):
    return v * lax.rsqrt(jnp.mean(v * v, -1, keepdims=True) + RMS_EPS) * g


def _inproj_body(x_ref, win_ref, cqg_ref, ckvg_ref, wqa_ref, wqb_ref, wkk_ref, wkv_ref,
                 tcq_ref, tsq_ref, tck_ref, tsk_ref, onehot_ref, posc_ref,
                 q_ref, k_ref, vt_ref, nq_ref, kcvc_ref, kaug_ref, vst_ref, kwaug_ref, vwt_ref,
                 gate_ref):
    xb = x_ref[...].astype(MXU_DT)
    h = _dot(xb, win_ref[...])
    cqn = _rms_rows(h[:, _E_CQ:_E_CKV], cqg_ref[...]).astype(MXU_DT)
    qa = _dot(cqn, wqa_ref[...])
    qb = _dot(cqn, wqb_ref[...])
    ckvn = _rms_rows(h[:, _E_CKV:_E_NQ], ckvg_ref[...]).astype(MXU_DT)
    kn = _dot(ckvn, wkk_ref[...])
    tail = jnp.where(lax.broadcasted_iota(jnp.int32, (VROWS - MLA_V, h.shape[0]), 0) == 0,
                     1.0, 0.0).astype(MXU_DT)
    vt = _dot(ckvn, wkv_ref[...]).T.astype(vt_ref.dtype)
    for hh in range(MLA_HEADS):
        vt_ref[hh * VROWS:hh * VROWS + MLA_V, :] = vt[hh * MLA_V:(hh + 1) * MLA_V, :]
        vt_ref[hh * VROWS + MLA_V:(hh + 1) * VROWS, :] = tail
    ns = onehot_ref.shape[1]
    posc = posc_ref[...]
    lane = lax.broadcasted_iota(jnp.int32, (h.shape[0], HP), 1)
    low = lane < NSA_DH

    def halves(blk):
        return jnp.where(low, blk, 0.0), jnp.where(low, pltpu.roll(blk, NSA_DH, 1), 0.0)

    ks = halves(h[:, _E_KS:_E_VS])
    kw = halves(h[:, _E_KW:_E_VW])
    vst = h[:, _E_VS:_E_KW].T.astype(MXU_DT)
    vwt = h[:, _E_VW:_E_MISC].T.astype(MXU_DT)
    for g in range(NSA_G):
        kaug_ref[g, :, 0:ns] = onehot_ref[...]
        kaug_ref[g, :, ns:ns + HP] = (ks[g] + posc).astype(MXU_DT)
        kwaug_ref[g] = (kw[g] + posc).astype(MXU_DT)
        for t_ref, tv in ((vst_ref, vst), (vwt_ref, vwt)):
            t_ref[g, 0:NSA_DH, :] = tv[g * NSA_DH:(g + 1) * NSA_DH, :]
            t_ref[g, NSA_DH:VROWS, :] = tail
    for j in range(NSA_HEADS // 2):
        pair = halves(h[:, _E_NQ + j * HP:_E_NQ + (j + 1) * HP])
        for i in range(2):
            nq_ref[:, (2 * j + i) * HP:(2 * j + i + 1) * HP] = pair[i].astype(nq_ref.dtype)
    misc = h[:, _E_MISC:_E_END]
    krr = (jnp.where(low, 0.0, pltpu.roll(misc, _M_KROT - _M_KR, 1)) * tck_ref[...]
           + misc * tsk_ref[...])
    tcq = tcq_ref[...]
    tsq = tsq_ref[...]
    for hh in range(MLA_HEADS):
        sl = slice(hh * HP, (hh + 1) * HP)
        q_ref[:, sl] = (qa[:, sl] * tcq + qb[:, sl] * tsq).astype(q_ref.dtype)
        k_ref[:, sl] = (kn[:, sl] + krr).astype(k_ref.dtype)
    kcvc_ref[...] = h[:, _E_KCVC:_E_KS]
    gate_ref[...] = jax.nn.sigmoid(misc)


def _inproj(x2d, lw, tabs, consts, seq, tm):
    t = x2d.shape[0]
    batch = t // seq
    npos = seq // tm
    ns = seq // SEL_BLOCK
    row = lambda i: (i, 0)
    pos = lambda i: (i % npos, 0)
    tok = lambda i: (i // npos, 0, i % npos, 0)
    ttr = lambda i: (i // npos, 0, 0, i % npos)
    hw = MLA_HEADS * HP
    vrows = MLA_HEADS * VROWS
    sds = jax.ShapeDtypeStruct
    outs = [
        (sds((t, hw), MXU_DT), pl.BlockSpec((tm, hw), row)),
        (sds((t, hw), MXU_DT), pl.BlockSpec((tm, hw), row)),
        (sds((batch, vrows, seq), MXU_DT),
         pl.BlockSpec((None, vrows, tm), lambda i: (i // npos, 0, i % npos))),
        (sds((t, NSA_HEADS * HP), MXU_DT), pl.BlockSpec((tm, NSA_HEADS * HP), row)),
        (sds((t, 256), F32), pl.BlockSpec((tm, 256), row)),
        (sds((batch, NSA_G, seq, ns + HP), MXU_DT), pl.BlockSpec((None, NSA_G, tm, ns + HP), tok)),
        (sds((batch, NSA_G, VROWS, seq), MXU_DT), pl.BlockSpec((None, NSA_G, VROWS, tm), ttr)),
        (sds((batch, NSA_G, seq, HP), MXU_DT), pl.BlockSpec((None, NSA_G, tm, HP), tok)),
        (sds((batch, NSA_G, VROWS, seq), MXU_DT), pl.BlockSpec((None, NSA_G, VROWS, tm), ttr)),
        (sds((t, LANE), F32), pl.BlockSpec((tm, LANE), row)),
    ]
    out_shape = [o[0] for o in outs]
    out_specs = [o[1] for o in outs]
    return pl.pallas_call(
        _inproj_body,
        grid=(t // tm,),
        in_specs=[pl.BlockSpec((tm, D_MODEL), row),
                  _const_spec((D_MODEL, _E_END)),
                  _const_spec((1, MLA_Q_RANK)), _const_spec((1, MLA_KV_RANK)),
                  _const_spec((MLA_Q_RANK, hw)), _const_spec((MLA_Q_RANK, hw)),
                  _const_spec((MLA_KV_RANK, hw)), _const_spec((MLA_KV_RANK, MLA_HEADS * MLA_V)),
                  pl.BlockSpec((tm, HP), pos), pl.BlockSpec((tm, HP), pos),
                  pl.BlockSpec((tm, HP), pos), pl.BlockSpec((tm, HP), pos),
                  pl.BlockSpec((tm, ns), pos), pl.BlockSpec((tm, HP), pos)],
        out_specs=out_specs,
        out_shape=out_shape,
        compiler_params=_params(1),
        name="in_proj",
    )(x2d, lw["w_in"], lw["cq_g"], lw["ckv_g"], lw["wqa"], lw["wqb"], lw["wkk"], lw["wkv"],
      tabs["cq"], tabs["sq"], tabs["ck"], tabs["sk"], consts["onehot"], consts["posc"])


def _mla_attn_body(q_ref, k_ref, vt_ref, o_ref, *, tq):
    qi = pl.program_id(2)

    tk2 = 2 * tq

    def step(off, nk, carry, diag_from):
        ss = []
        for h in range(MLA_HPS):
            hs = slice(h * HP, (h + 1) * HP)
            s = _dot_nt(k_ref[pl.ds(off, nk), hs], q_ref[:, hs])
            if diag_from is not None:
                kpos = lax.broadcasted_iota(jnp.int32, (nk, tq), 0) - diag_from
                qpos = lax.broadcasted_iota(jnp.int32, (nk, tq), 1)
                s = jnp.where(kpos <= qpos, s, NEG)
            ss.append(s)
        ms, alphas, ps = [], [], []
        for h in range(MLA_HPS):
            m = carry[h][0]
            m_new = jnp.maximum(m, jnp.max(ss[h], 0, keepdims=True))
            alphas.append(jnp.exp2(m - m_new))
            ps.append(jnp.exp2(ss[h] - m_new).astype(MXU_DT))
            ms.append(m_new)
        out = []
        for h in range(MLA_HPS):
            vt = vt_ref[h * VROWS:(h + 1) * VROWS, pl.ds(off, nk)]
            out.append((ms[h], alphas[h] * carry[h][1] + _dot(vt, ps[h])))
        return tuple(out)

    def finish(carry):
        for h in range(MLA_HPS):
            acc = carry[h][1]
            o_ref[h * MLA_V:(h + 1) * MLA_V, :] = (
                acc[0:MLA_V, :] / acc[MLA_V:MLA_V + 1, :]).astype(o_ref.dtype)

    one = (jnp.full((1, tq), NEG, F32), jnp.zeros((VROWS, tq), F32))
    carry = lax.fori_loop(
        0, qi // 2, lambda kp, c: step(pl.multiple_of(kp * tk2, tk2), tk2, c, None),
        (one,) * MLA_HPS)

    @pl.when(qi % 2 == 1)
    def _():
        finish(step(pl.multiple_of((qi - 1) * tq, tq), tk2, carry, tq))

    @pl.when(qi % 2 == 0)
    def _():
        finish(step(pl.multiple_of(qi * tq, tq), tq, carry, 0))


def _mla_attn(q, k, vt, batch, seq, tq):
    nq = seq // tq
    pw = MLA_HPS * HP
    return pl.pallas_call(
        functools.partial(_mla_attn_body, tq=tq),
        grid=(batch, MLA_HEADS // MLA_HPS, nq),
        in_specs=[pl.BlockSpec((tq, pw), lambda b, h, i: (b * nq + i, h)),
                  pl.BlockSpec((seq, pw), lambda b, h, i: (b, h)),
                  pl.BlockSpec((None, MLA_HPS * VROWS, seq), lambda b, h, i: (b, h, 0))],
        out_specs=pl.BlockSpec((None, MLA_HPS * MLA_V, tq), lambda b, h, i: (b, h, i)),
        out_shape=jax.ShapeDtypeStruct((batch, MLA_HEADS * MLA_V, seq), MXU_DT),
        compiler_params=_params(3),
        name="mla_attn",
    )(q, k, vt)


def _compress_body(c_ref, pe_ref, w1_ref, b1_ref, w2_ref, o_ref):
    nch = c_ref.shape[0] // CMP_STRIDE
    u = jnp.zeros((nch, NSA_G * CMP_HIDDEN), F32)
    w = jnp.zeros((nch, NSA_G * CMP_HIDDEN), F32)
    for p in range(CMP_STRIDE):
        x = c_ref[pl.ds(p, nch, stride=CMP_STRIDE), :]
        u = u + _dot((x + pe_ref[p:p + 1, :]).astype(MXU_DT), w1_ref[p])
        w = w + _dot((x + pe_ref[CMP_STRIDE + p:CMP_STRIDE + p + 1, :]).astype(MXU_DT),
                     w1_ref[CMP_STRIDE + p])
    w = pltpu.roll(w, nch - 1, 0)
    hid = jax.nn.gelu(u + w + b1_ref[...])
    o_ref[...] = _dot(hid.astype(MXU_DT), w2_ref[...]).astype(o_ref.dtype)


def _compress(kcvc, pe, w1, b1, w2, batch, seq):
    nch = seq // CMP_STRIDE
    gw = NSA_G * NSA_DH
    gh = NSA_G * CMP_HIDDEN
    return pl.pallas_call(
        _compress_body,
        grid=(2, batch),
        in_specs=[pl.BlockSpec((seq, gw), lambda s, j: (j, s)),
                  pl.BlockSpec((None, CMP_BLOCK, gw), lambda s, j: (s, 0, 0)),
                  pl.BlockSpec((None, CMP_BLOCK, gw, gh), lambda s, j: (s, 0, 0, 0)),
                  pl.BlockSpec((None, 1, gh), lambda s, j: (s, 0, 0)),
                  pl.BlockSpec((None, gh, gw), lambda s, j: (s, 0, 0))],
        out_specs=pl.BlockSpec((None, None, nch, gw), lambda s, j: (s, j, 0, 0)),
        out_shape=jax.ShapeDtypeStruct((2, batch, nch, gw), MXU_DT),
        compiler_params=_params(2),
        name="nsa_compress",
    )(kcvc, pe, w1, b1, w2)


def _head_blockdiag(w):
    eye = jnp.eye(NSA_G, dtype=w.dtype)
    out = jnp.einsum("gh,...ab->...gahb", eye, w)
    return out.reshape(w.shape[:-2] + (NSA_G * w.shape[-2], NSA_G * w.shape[-1]))


_BIG = 1e30


def _nsa_queries(q_ref, slope_ref, g, q0, tq, first_head):
    lane = lax.broadcasted_iota(jnp.int32, (tq, HP), 1)
    q0f = q0.astype(F32)
    qs = []
    for r in range(NSA_R):
        slope = slope_ref[g * NSA_R + r]
        c = jnp.where(lane == NSA_DH, slope * POS_SPLIT,
                      jnp.where(lane == NSA_DH + 1, slope,
                                jnp.where(lane == NSA_DH + 2, -slope * q0f, 0.0)))
        h0 = (first_head + r) * HP
        qs.append(q_ref[:, h0:h0 + HP] + c.astype(MXU_DT))
    return qs


def _cmp_attn_body(slope_ref, q_ref, kc_ref, vct_ref, ovl_ref, pool_ref,
                   oc_ref, sel_ref, flag_ref, s_scr, e_scr, *, tq, n_sel):
    g = pl.program_id(1)
    q0 = pl.program_id(2) * tq
    ns = sel_ref.shape[1]
    n = NSA_R * tq
    ch = min(CMP_CH, kc_ref.shape[0])
    qc = jnp.concatenate(_nsa_queries(q_ref, slope_ref, g, q0, tq, 0), axis=0)
    t_lane = q0 + (lax.broadcasted_iota(jnp.int32, (1, n), 1) & (tq - 1))
    cend0 = lax.broadcasted_iota(jnp.int32, (ch, n), 0) * CMP_STRIDE + (CMP_BLOCK - 1)
    nproc = jnp.maximum(q0 + tq - CMP_BLOCK, 0) // (ch * CMP_STRIDE) + 1

    def scores(c, m):
        off = pl.multiple_of(c * ch, ch)
        s = _dot_nt(kc_ref[pl.ds(off, ch), :], qc)
        s = jnp.where(cend0 <= t_lane - off * CMP_STRIDE, s, NEG)
        s_scr[pl.ds(off, ch), :] = s
        return jnp.maximum(m, jnp.max(s, 0, keepdims=True))

    m = lax.fori_loop(0, nproc, scores, jnp.full((1, n), NEG, F32))

    def weights(c, carry):
        off = pl.multiple_of(c * ch, ch)
        e_scr[pl.ds(off, ch), :] = jnp.exp(s_scr[pl.ds(off, ch), :] - m).astype(MXU_DT)
        return carry

    def clear(c, carry):
        e_scr[pl.ds(pl.multiple_of(c * ch, ch), ch), :] = jnp.zeros((ch, n), MXU_DT)
        return carry

    lax.fori_loop(0, nproc, weights, 0)
    lax.fori_loop(nproc, kc_ref.shape[0] // ch, clear, 0)
    e = e_scr[...]
    acc = _dot(vct_ref[...], e)
    has_key = t_lane >= CMP_BLOCK - 1
    inv = jnp.where(has_key, 1.0 / acc[NSA_DH:NSA_DH + 1, :], 0.0)
    impn = _dot(ovl_ref[...], e) * inv
    imp = impn[:, 0:tq]
    for r in range(NSA_R):
        oc_ref[r] = acc[0:NSA_DH, r * tq:(r + 1) * tq] * inv[:, r * tq:(r + 1) * tq]
        if r:
            imp = imp + impn[:, r * tq:(r + 1) * tq]

    blk = lax.broadcasted_iota(jnp.int32, (ns, tq), 0)
    cur = (q0 + lax.broadcasted_iota(jnp.int32, (ns, tq), 1)) // SEL_BLOCK
    blkf = blk.astype(F32)
    forced = (blk == 0) | (blk == cur) | (blk == cur - 1)
    score0 = jnp.where((blk <= cur) & jnp.logical_not(forced), imp, -1.0)
    score = score0
    for _ in range(n_sel - N_FORCED):
        mx = jnp.max(score, 0, keepdims=True)
        first = jnp.min(jnp.where(score == mx, blkf, float(ns)), 0, keepdims=True)
        score = jnp.where(blkf == first, -1.0, score)
    sel = jnp.where(forced, 1.0, jnp.where(score != score0, 1.0, 0.0))
    sel_ref[...] = sel.T.astype(sel_ref.dtype)
    ones = jnp.ones((8, NSA_TQ), MXU_DT)
    selb = sel.astype(MXU_DT)
    for j in range(tq // NSA_TQ):
        cnt = _dot_nt(ones, selb[:, j * NSA_TQ:(j + 1) * NSA_TQ])
        flag_ref[j] = _dot(cnt.astype(MXU_DT), pool_ref[...])


def _cmp_attn(slopes, nq, kc, vct, ovl, pool, batch, seq, tq):
    nq_t = seq // tq
    nch = kc.shape[-2]
    ns = seq // SEL_BLOCK
    n_sel = min(N_SEL, ns)
    sub = tq // NSA_TQ
    gw = NSA_R * HP
    n = NSA_R * tq
    return pl.pallas_call(
        functools.partial(_cmp_attn_body, tq=tq, n_sel=n_sel),
        grid=(batch, NSA_G, nq_t),
        in_specs=[pl.BlockSpec(memory_space=pltpu.SMEM),
                  pl.BlockSpec((tq, gw), lambda b, g, i: (b * nq_t + i, g)),
                  pl.BlockSpec((None, None, nch, HP), lambda b, g, i: (b, g, 0, 0)),
                  pl.BlockSpec((None, None, VROWS, nch), lambda b, g, i: (b, g, 0, 0)),
                  _const_spec(ovl.shape), _const_spec(pool.shape)],
        out_specs=[pl.BlockSpec((None, None, NSA_R, NSA_DH, tq), lambda b, g, i: (b, g, 0, 0, i)),
                   pl.BlockSpec((None, None, tq, ns), lambda b, g, i: (b, g, i, 0)),
                   pl.BlockSpec((None, None, sub, 8, LANE), lambda b, g, i: (b, g, i, 0, 0))],
        out_shape=[jax.ShapeDtypeStruct((batch, NSA_G, NSA_R, NSA_DH, seq), F32),
                   jax.ShapeDtypeStruct((batch, NSA_G, seq, ns), MXU_DT),
                   jax.ShapeDtypeStruct((batch, NSA_G, seq // NSA_TQ, 8, LANE), F32)],
        scratch_shapes=[pltpu.VMEM((nch, n), F32), pltpu.VMEM((nch, n), MXU_DT)],
        compiler_params=_params(3),
        name="nsa_cmp_attn_topk",
    )(slopes, nq, kc, vct, ovl, pool)


def _nsa_attn_body(flag_ref, slope_ref, q_ref, sel_ref, kaug_ref, vst_ref, kwaug_ref, vwt_ref,
                   oct_ref, gt_ref, o_ref, list0_scr, list1_scr, os_scr, ow_scr, *, tq, tk, nkt):
    b = pl.program_id(0)
    qi = pl.program_id(1)
    q0 = qi * tq
    n = NSA_R * tq
    wk = WINDOW + tq
    lists = (list0_scr, list1_scr)
    heads = range(NSA_G)

    t_loc = lax.broadcasted_iota(jnp.int32, (1, n), 1) & (tq - 1)
    t_lane = q0 + t_loc
    kd = q0 // tk

    qa, qc, cnt = [], [], []
    for g in heads:
        selneg = ((sel_ref[g].astype(F32) - 1.0) * _BIG).astype(MXU_DT)
        qs = _nsa_queries(q_ref, slope_ref, g, q0, tq, g * NSA_R)
        qc.append(jnp.concatenate(qs, axis=0))
        qa.append(jnp.concatenate([jnp.concatenate([selneg, qr], axis=1) for qr in qs], axis=0))
        fbase = ((b * NSA_G + g) * pl.num_programs(1) + qi) * nkt

        def compact(kt, c, fbase=fbase, lst=lists[g]):
            act = flag_ref[fbase + kt] > 0

            @pl.when(act)
            def _():
                lst[c] = kt
            return c + act.astype(jnp.int32)

        c = lax.fori_loop(0, kd, compact, 0)
        for j in range(NSA_U - 1):
            lists[g][c + j] = kd
        cnt.append(c)
    nfull = [c // NSA_U for c in cnt]

    def scores(g, idx):
        offs = [pl.multiple_of(i * tk, tk) for i in idx]
        return [_dot_nt(kaug_ref[g, pl.ds(off, tk), :], qa[g]) for off in offs], offs

    def soft(ss, m_old):
        m_new = m_old
        for s in ss:
            m_new = jnp.maximum(m_new, jnp.max(s, 0, keepdims=True))
        return m_new, jnp.exp(m_old - m_new), [jnp.exp(s - m_new).astype(MXU_DT) for s in ss]

    def pv(g, ps, offs, alpha, acc):
        acc = alpha * acc
        for p, off in zip(ps, offs):
            acc = acc + _dot(vst_ref[g, :, pl.ds(off, tk)], p)
        return acc

    def groups(gs, gi, carry):
        sc = [scores(g, [lists[g][gi * NSA_U + j] for j in range(NSA_U)]) for g in gs]
        sm = [soft(sc[k][0], carry[k][0]) for k in range(len(gs))]
        return tuple((sm[k][0], pv(g, sm[k][2], sc[k][1], sm[k][1], carry[k][1]))
                     for k, g in enumerate(gs))

    init = (jnp.full((1, n), NEG, F32), jnp.zeros((VROWS, n), F32))
    nboth = jnp.minimum(nfull[0], nfull[1])
    carry = lax.fori_loop(0, nboth, lambda gi, c: groups((0, 1), gi, c), (init, init))
    carry = [lax.fori_loop(nboth, nfull[g], lambda gi, c, g=g: groups((g,), gi, (c,))[0], carry[g])
             for g in heads]

    k_loc = lax.broadcasted_iota(jnp.int32, (tq, n), 0)

    def tail(start, banded):
        sc = [scores(g, [lists[g][nfull[g] * NSA_U + j] for j in range(NSA_U - 1)] + [kd])
              for g in heads]
        sw = [_dot_nt(kwaug_ref[g, pl.ds(start, wk), :], qc[g]) for g in heads]
        parts = []
        for g in heads:
            ss, offs = sc[g]
            for j in range(NSA_U - 1):
                ss[j] = jnp.where(nfull[g] * NSA_U + j < cnt[g], ss[j], NEG)
            kpos = offs[-1] + lax.broadcasted_iota(jnp.int32, (tk, n), 0)
            ss[-1] = jnp.where(kpos <= t_lane, ss[-1], NEG)
            if banded:
                parts.append([jnp.where(k_loc > t_loc, sw[g][0:tq], NEG), sw[g][tq:WINDOW],
                              jnp.where(k_loc <= t_loc, sw[g][WINDOW:wk], NEG)])
            else:
                dist = t_lane - (start + lax.broadcasted_iota(jnp.int32, (wk, n), 0))
                parts.append([jnp.where((dist >= 0) & (dist < WINDOW), sw[g], NEG)])
        sm, es = [], []
        for g in heads:
            sm.append(soft(sc[g][0], carry[g][0]))
            mw = jnp.max(parts[g][0], 0, keepdims=True)
            for p in parts[g][1:]:
                mw = jnp.maximum(mw, jnp.max(p, 0, keepdims=True))
            es.append(jnp.concatenate([jnp.exp(p - mw).astype(MXU_DT) for p in parts[g]], axis=0))
        for g in heads:
            acc = pv(g, sm[g][2], sc[g][1], sm[g][1], carry[g][1])
            ow = _dot(vwt_ref[g, :, pl.ds(start, wk)], es[g])
            os_scr[g] = acc[0:NSA_DH, :] / acc[NSA_DH:NSA_DH + 1, :]
            ow_scr[g] = ow[0:NSA_DH, :] / ow[NSA_DH:NSA_DH + 1, :]

    @pl.when(q0 >= WINDOW)
    def _():
        tail(pl.multiple_of(q0 - WINDOW, LANE), True)

    @pl.when(q0 < WINDOW)
    def _():
        tail(0, False)

    for g in heads:
        o_sel = os_scr[g]
        o_win = ow_scr[g]
        for r in range(NSA_R):
            ls = slice(r * tq, (r + 1) * tq)
            o_ref[g, r] = (gt_ref[g, 0, r:r + 1, :] * oct_ref[g, r]
                           + gt_ref[g, 1, r:r + 1, :] * o_sel[:, ls]
                           + gt_ref[g, 2, r:r + 1, :] * o_win[:, ls]).astype(o_ref.dtype)


def _nsa_attn(flags, slopes, nq, sel, kaug, vst, kwaug, vwt, oct_, gt, batch, seq, tq, tk):
    nq_t = seq // tq
    ns = seq // SEL_BLOCK
    nkt = seq // tk
    n = NSA_R * tq
    once = dict(pipeline_mode=pl.Buffered(1))
    res = lambda b, i, f: (b, 0, 0, 0)
    til = lambda b, i, f: (b, 0, 0, 0, i)
    grid_spec = pltpu.PrefetchScalarGridSpec(
        num_scalar_prefetch=1,
        grid=(batch, nq_t),
        in_specs=[pl.BlockSpec(memory_space=pltpu.SMEM),
                  pl.BlockSpec((tq, NSA_HEADS * HP), lambda b, i, f: (b * nq_t + i, 0)),
                  pl.BlockSpec((None, NSA_G, tq, ns), lambda b, i, f: (b, 0, i, 0)),
                  pl.BlockSpec((None, NSA_G, seq, ns + HP), res, **once),
                  pl.BlockSpec((None, NSA_G, VROWS, seq), res, **once),
                  pl.BlockSpec((None, NSA_G, seq, HP), res, **once),
                  pl.BlockSpec((None, NSA_G, VROWS, seq), res, **once),
                  pl.BlockSpec((None, NSA_G, NSA_R, NSA_DH, tq), til),
                  pl.BlockSpec((None, NSA_G, 3, NSA_R, tq), til)],
        out_specs=pl.BlockSpec((None, NSA_G, NSA_R, NSA_DH, tq), til),
        scratch_shapes=[pltpu.SMEM((nkt + NSA_U,), jnp.int32), pltpu.SMEM((nkt + NSA_U,), jnp.int32),
                        pltpu.VMEM((NSA_G, NSA_DH, n), F32), pltpu.VMEM((NSA_G, NSA_DH, n), F32)])
    return pl.pallas_call(
        functools.partial(_nsa_attn_body, tq=tq, tk=tk, nkt=nkt),
        grid_spec=grid_spec,
        out_shape=jax.ShapeDtypeStruct((batch, NSA_G, NSA_R, NSA_DH, seq), MXU_DT),
        compiler_params=_params(2),
        name="nsa_sel_win_attn",
    )(flags, slopes, nq, sel, kaug, vst, kwaug, vwt, oct_, gt)


_TN = (((0,), (0,)), ((), ()))


def _outproj_body(x_ref, oa_ref, ob_ref, wo_ref, g_ref, b_ref, y_ref):
    wa = oa_ref.shape[0]
    mix = lax.dot_general(oa_ref[...].astype(MXU_DT), wo_ref[0:wa, :], _TN,
                          preferred_element_type=F32)
    mix = mix + lax.dot_general(ob_ref[...].astype(MXU_DT), wo_ref[wa:, :], _TN,
                                preferred_element_type=F32)
    y_ref[...] = _ln_rows(ALPHA * x_ref[...] + mix, g_ref[...], b_ref[...])


def _outproj(x2d, oat, obt, wo, g, b, tm):
    batch, _, seq = oat.shape
    ns = seq // tm
    row = lambda bb, i: (bb * ns + i, 0)
    col = lambda bb, i: (bb, 0, i)
    return pl.pallas_call(
        _outproj_body,
        grid=(batch, ns),
        in_specs=[pl.BlockSpec((tm, D_MODEL), row),
                  pl.BlockSpec((None, oat.shape[1], tm), col),
                  pl.BlockSpec((None, obt.shape[1], tm), col),
                  _const_spec(wo.shape),
                  _const_spec((1, D_MODEL)), _const_spec((1, D_MODEL))],
        out_specs=pl.BlockSpec((tm, D_MODEL), row),
        out_shape=jax.ShapeDtypeStruct((batch * seq, D_MODEL), F32),
        compiler_params=_params(2),
        name="out_proj_ln",
    )(x2d, oat, obt, wo, g.reshape(1, -1), b.reshape(1, -1))


def _xattn_body(x_ref, wq_ref, kv_ref, wo_ref, g_ref, b_ref, y_ref):
    x = x_ref[...]
    q = (_dot(x.astype(MXU_DT), wq_ref[...]) * (XA_DH ** -0.5)).astype(MXU_DT)
    hw = XA_HEADS * XA_DH
    sls = [slice(h * XA_DH, (h + 1) * XA_DH) for h in range(XA_HEADS)]
    ss = [_dot_nt(q[:, sl], kv_ref[:, sl]) for sl in sls]
    ps = []
    for s in ss:
        e = jnp.exp(s - jnp.max(s, -1, keepdims=True))
        ps.append((e / jnp.sum(e, -1, keepdims=True)).astype(MXU_DT))
    o = [_dot(p, kv_ref[:, hw + h * XA_DH:hw + (h + 1) * XA_DH]).astype(MXU_DT)
         for h, p in enumerate(ps)]
    y = _dot(jnp.concatenate(o, axis=1), wo_ref[...])
    y_ref[...] = _ln_rows(ALPHA * x + y, g_ref[...], b_ref[...])


def _xattn(x2d, wq, memkv, wo, g, b, batch, seq, tm):
    ns = seq // tm
    mem = memkv.shape[0] // batch
    row = lambda bb, i: (bb * ns + i, 0)
    return pl.pallas_call(
        _xattn_body,
        grid=(batch, ns),
        in_specs=[pl.BlockSpec((tm, D_MODEL), row),
                  _const_spec(wq.shape),
                  pl.BlockSpec((mem, memkv.shape[1]), lambda bb, i: (bb, 0)),
                  _const_spec(wo.shape),
                  _const_spec((1, D_MODEL)), _const_spec((1, D_MODEL))],
        out_specs=pl.BlockSpec((tm, D_MODEL), row),
        out_shape=jax.ShapeDtypeStruct((batch * seq, D_MODEL), F32),
        compiler_params=_params(2),
        name="mem_xattn_ln",
    )(x2d, wq, memkv, wo, g.reshape(1, -1), b.reshape(1, -1))


_FF_CHUNK = D_FF
_HALO = 8


def _ffn_body(x_ref, xp_ref, wup_ref, cw_ref, cb_ref, wdn_ref, g_ref, b_ref, y_ref):
    first = pl.program_id(1) == 0
    x = x_ref[...]
    xp = jnp.where(first, 0.0, xp_ref[...])
    xe = jnp.concatenate([xp, x], axis=0).astype(MXU_DT)

    def conv(c0):
        u = _dot(xe, wup_ref[:, c0:c0 + _FF_CHUNK])
        u1 = pltpu.roll(u, 1, 0)
        u2 = pltpu.roll(u, 2, 0)
        cw = cw_ref[:, c0:c0 + _FF_CHUNK]
        r = u * cw[2:3, :] + u1 * cw[1:2, :] + u2 * cw[0:1, :] + cb_ref[:, c0:c0 + _FF_CHUNK]
        return r[_HALO:, :]

    acc = jnp.zeros_like(x)
    for c in range(D_FF // _FF_CHUNK):
        a = conv(c * _FF_CHUNK)
        gg = conv(D_FF + c * _FF_CHUNK)
        act = (a * jax.nn.silu(gg)).astype(MXU_DT)
        acc = acc + _dot(act, wdn_ref[c * _FF_CHUNK:(c + 1) * _FF_CHUNK, :])
    y_ref[...] = _ln_rows(ALPHA * x + acc, g_ref[...], b_ref[...])


def _ffn(x2d, wup, cw, cb, wdn, g, b, batch, seq, tm):
    ns = seq // tm
    hb = tm // _HALO
    row = lambda bb, i: (bb * ns + i, 0)
    prev = lambda bb, i: (jnp.maximum((bb * ns + i) * hb - 1, 0), 0)
    return pl.pallas_call(
        _ffn_body,
        grid=(batch, ns),
        in_specs=[pl.BlockSpec((tm, D_MODEL), row),
                  pl.BlockSpec((_HALO, D_MODEL), prev),
                  _const_spec(wup.shape), _const_spec(cw.shape), _const_spec(cb.shape),
                  _const_spec(wdn.shape),
                  _const_spec((1, D_MODEL)), _const_spec((1, D_MODEL))],
        out_specs=pl.BlockSpec((tm, D_MODEL), row),
        out_shape=jax.ShapeDtypeStruct((batch * seq, D_MODEL), F32),
        compiler_params=_params(2),
        name="conv_ffn_ln",
    )(x2d, x2d, wup, cw, cb, wdn, g.reshape(1, -1), b.reshape(1, -1))


def _rot_half_cols(w):
    half = MLA_ROPE // 2
    return jnp.concatenate([-w[..., half:], w[..., :half]], axis=-1)


def _prep_layer(w_in, cq_g, w_uq, ckv_g, w_ukv):
    d = w_in.shape[0]
    z = lambda n: jnp.zeros((d, n), F32)
    kr = w_in[:, _O_KR:_O_NQ]
    ngl = _IN_COLS - _O_GL
    w_ext = jnp.concatenate([
        w_in[:, _O_CQ:_O_CKV], w_in[:, _O_CKV:_O_KR],
        w_in[:, _O_NQ:_O_KV6] * (NSA_DH ** -0.5),
        w_in[:, _O_KV6:_O_GL],
        kr, w_in[:, _O_GL:_IN_COLS], z(_M_KROT - _M_GL - ngl),
        _rot_half_cols(kr), z(HP - _M_KROT - MLA_ROPE)], axis=1)
    wq = w_uq.reshape(MLA_Q_RANK, MLA_HEADS, MLA_NOPE + MLA_ROPE)
    zq = jnp.zeros((MLA_Q_RANK, MLA_HEADS, HP - MLA_NOPE - MLA_ROPE), F32)
    wqa = jnp.concatenate([wq, zq], axis=-1).reshape(MLA_Q_RANK, MLA_HEADS * HP)
    wqb = jnp.concatenate([jnp.zeros_like(wq[..., :MLA_NOPE]), _rot_half_cols(wq[..., MLA_NOPE:]), zq],
                          axis=-1).reshape(MLA_Q_RANK, MLA_HEADS * HP)
    wkv = w_ukv.reshape(MLA_KV_RANK, MLA_HEADS, MLA_NOPE + MLA_V)
    zk = jnp.zeros((MLA_KV_RANK, MLA_HEADS, HP - MLA_NOPE), F32)
    wkk = jnp.concatenate([wkv[..., :MLA_NOPE], zk], axis=-1).reshape(MLA_KV_RANK, MLA_HEADS * HP)
    wvv = wkv[..., MLA_NOPE:].reshape(MLA_KV_RANK, MLA_HEADS * MLA_V)
    return dict(w_in=w_ext.astype(MXU_DT), cq_g=cq_g.reshape(1, -1), ckv_g=ckv_g.reshape(1, -1),
                wqa=wqa.astype(MXU_DT), wqb=wqb.astype(MXU_DT),
                wkk=wkk.astype(MXU_DT), wkv=wvv.astype(MXU_DT))


def _rope_tabs(seq):
    inv = ROPE_BASE ** (-jnp.arange(0, MLA_ROPE, 2, dtype=F32) / MLA_ROPE)
    ang = jnp.arange(seq, dtype=F32)[:, None] * inv[None, :]
    cos, sin = jnp.cos(ang), jnp.sin(ang)
    pad = jnp.zeros((seq, HP - MLA_NOPE - MLA_ROPE), F32)
    tc = jnp.concatenate([jnp.ones((seq, MLA_NOPE), F32), cos, cos, pad], axis=1)
    ts = jnp.concatenate([jnp.zeros((seq, MLA_NOPE), F32), sin, sin, pad], axis=1)
    scale = (MLA_NOPE + MLA_ROPE) ** -0.5
    return dict(cq=tc * (scale * LOG2E), sq=ts * (scale * LOG2E), ck=tc, sk=ts)


def _pos_cols(pos, width, first):
    lane = jnp.arange(width)[None, :]
    p = pos[:, None]
    return jnp.where(lane == first, p // POS_SPLIT,
                     jnp.where(lane == first + 1, p % POS_SPLIT,
                               jnp.where(lane == first + 2, 1, 0))).astype(F32)


def _ones_row(n):
    return (jnp.arange(VROWS - NSA_DH)[:, None] == jnp.zeros((1, n), jnp.int32)).astype(MXU_DT)


def _nsa_consts(seq):
    nch = seq // CMP_STRIDE
    ns = seq // SEL_BLOCK
    c_start = jnp.arange(nch) * CMP_STRIDE
    s_start = jnp.arange(ns) * SEL_BLOCK
    ovl = jnp.clip(jnp.minimum(c_start[None, :] + CMP_BLOCK, s_start[:, None] + SEL_BLOCK)
                   - jnp.maximum(c_start[None, :], s_start[:, None]), 0).astype(F32) / CMP_BLOCK
    pos = jnp.arange(seq)
    onehot = (pos[:, None] // SEL_BLOCK == jnp.arange(ns)[None, :]).astype(MXU_DT)
    pool = (jnp.arange(ns)[:, None] // (NSA_TK // SEL_BLOCK) == jnp.arange(LANE)[None, :])
    slopes = 2.0 ** (-8.0 * jnp.arange(1, NSA_HEADS + 1, dtype=F32) / NSA_HEADS)
    return dict(ovl=ovl.astype(MXU_DT),
                onehot=onehot,
                posc=_pos_cols(pos, HP, NSA_DH),
                cendc=_pos_cols(c_start + CMP_BLOCK - 1, HP - NSA_DH, 0).astype(MXU_DT),
                pool=pool.astype(MXU_DT), slopes=slopes)


def kernel(x, mem, ln_in_g, ln_in_b, ln_mem_g, ln_mem_b, w_in, mla_cq_g, mla_w_uq, mla_ckv_g, mla_w_ukv, nsa_pe_k, nsa_w1_k, nsa_b1_k, nsa_w2_k, nsa_pe_v, nsa_w1_v, nsa_b1_v, nsa_w2_v, w_out, ln1_g, ln1_b, xa_wq, xa_wkv, xa_wo, ln2_g, ln2_b, ffn_w_up, ffn_conv_w, ffn_conv_b, ffn_w_down, ln3_g, ln3_b):
    batch, seq, d = x.shape
    t = batch * seq
    depth = w_in.shape[0]
    tm = min(TM, seq)
    nsa_tq, nsa_tk = min(NSA_TQ, seq), min(NSA_TK, seq)
    tabs = _rope_tabs(seq)
    consts = _nsa_consts(seq)
    nch = seq // CMP_STRIDE
    nkt = seq // nsa_tk
    bg = (batch, NSA_G)
    bcast = lambda a: jnp.broadcast_to(a, bg + a.shape)

    xs = _layer_norm(x.reshape(t, d), ln_in_g, ln_in_b, tm)
    memn = _layer_norm(mem.reshape(-1, d), ln_mem_g, ln_mem_b, mem.shape[0] * mem.shape[1])

    for l in range(depth):
        lw = _prep_layer(w_in[l], mla_cq_g[l], mla_w_uq[l], mla_ckv_g[l], mla_w_ukv[l])
        q, k, vt, nq, kcvc, kaug, vst, kwaug, vwt, gates = _inproj(xs, lw, tabs, consts, seq, tm)

        oa = _mla_attn(q, k, vt, batch, seq, min(MLA_T, seq))

        pe = jnp.tile(jnp.stack([nsa_pe_k[l], nsa_pe_v[l]]), (1, 1, NSA_G))
        w1 = jnp.stack([nsa_w1_k[l], nsa_w1_v[l]]).reshape(2, CMP_BLOCK, NSA_DH, CMP_HIDDEN)
        w1 = _head_blockdiag(w1).astype(MXU_DT)
        b1 = jnp.tile(jnp.stack([nsa_b1_k[l], nsa_b1_v[l]]).reshape(2, 1, CMP_HIDDEN), (1, 1, NSA_G))
        w2 = _head_blockdiag(jnp.stack([nsa_w2_k[l], nsa_w2_v[l]])).astype(MXU_DT)
        cmp = _compress(kcvc, pe, w1, b1, w2, batch, seq)
        cmp = cmp.reshape(2, batch, nch, NSA_G, NSA_DH).transpose(0, 1, 3, 2, 4)
        kcaug = jnp.concatenate([cmp[0], bcast(consts["cendc"])], axis=-1)
        vct = jnp.concatenate([cmp[1].transpose(0, 1, 3, 2), bcast(_ones_row(nch))], axis=2)
        oc_t, sel, cnt = _cmp_attn(consts["slopes"], nq, kcaug, vct, consts["ovl"], consts["pool"],
                                   batch, seq, min(CMP_TQ, seq))
        flags = (cnt[:, :, :, 0, :nkt] > 0).astype(jnp.int32).reshape(-1)
        gt = gates[:, _M_GL:_M_GL + 3 * NSA_HEADS].reshape(batch, seq, NSA_G, NSA_R, 3)
        gt = gt.transpose(0, 2, 4, 3, 1)
        ob = _nsa_attn(flags, consts["slopes"], nq, sel, kaug, vst, kwaug, vwt, oc_t, gt,
                       batch, seq, nsa_tq, nsa_tk)
        ob = ob.reshape(batch, NSA_HEADS * NSA_DH, seq)

        xs = _outproj(xs, oa, ob, w_out[l].astype(MXU_DT), ln1_g[l], ln1_b[l], tm)

        memkv = _matmul(memn, xa_wkv[l].astype(MXU_DT), 512, MXU_DT)
        xs = _xattn(xs, xa_wq[l].astype(MXU_DT), memkv, xa_wo[l].astype(MXU_DT),
                    ln2_g[l], ln2_b[l], batch, seq, tm)

        xs = _ffn(xs, ffn_w_up[l].astype(MXU_DT), ffn_conv_w[l], ffn_conv_b[l].reshape(1, -1),
                  ffn_w_down[l].astype(MXU_DT), ln3_g[l], ln3_b[l], batch, seq, tm)

    return xs.reshape(batch, seq, d)
```
